```python
import jax, jax.numpy as jnp
from jax import lax
import numpy as np

D_MODEL = 1024
BATCH = 2
SEQ = 16384
DEPTH = 2
DEC_BATCH = 8
DEC_SEQ = 2048
PAST_LEN = 128

HEAD_DIM = 64
ATT_HEADS = 8
WINDOW_DILATIONS = ((128, 1), (512, 4), (2048, 16))
N_ATT_GROUPS = len(WINDOW_DILATIONS)
ATT_WIDTH = ATT_HEADS * HEAD_DIM
ATT_COLS = N_ATT_GROUPS * ATT_WIDTH
ROPE_THETA = 10000.0
HGRN_HEADS = 4
HGRN_DK = 128
HGRN_DV = 128
HGRN_KEY_COLS = HGRN_HEADS * HGRN_DK
HGRN_WIDTH = HGRN_HEADS * HGRN_DV
HGRN_CHUNK = 64
FNET_GROUPS = 4
FNET_GROUP_DIM = 128
FNET_WIDTH = FNET_GROUPS * FNET_GROUP_DIM
N_BRANCHES = 3
D_FF = 2816
EPS = 1e-6
NEG_INF = -1e30

SPLIT_SIZES = (ATT_COLS, ATT_COLS, ATT_COLS, HGRN_KEY_COLS, HGRN_KEY_COLS, HGRN_KEY_COLS,
               HGRN_WIDTH, HGRN_WIDTH, FNET_WIDTH, N_BRANCHES * D_MODEL)
IN_COLS = sum(SPLIT_SIZES)
SPLIT_POINTS = tuple(int(s) for s in np.cumsum(SPLIT_SIZES)[:-1])

kernel_name = 'hybrid_dilated_hgrn2_fnet_encoder'


def rms_norm(x, g):
    xf = x.astype(jnp.float32)
    y = xf * lax.rsqrt(jnp.mean(xf * xf, axis=-1, keepdims=True) + EPS)
    return (y * g.astype(jnp.float32)).astype(x.dtype)


def swiglu(x, w_gate, w_up, w_down):
    return (jax.nn.silu(x @ w_gate) * (x @ w_up)) @ w_down


def rope(x, pos):
    half = HEAD_DIM // 2
    inv_freq = ROPE_THETA ** (-jnp.arange(half, dtype=jnp.float32) / half)
    ang = pos.astype(jnp.float32)[:, None] * inv_freq[None, :]
    cos = jnp.cos(ang)[None, :, None, :]
    sin = jnp.sin(ang)[None, :, None, :]
    xf = x.astype(jnp.float32)
    x1, x2 = xf[..., :half], xf[..., half:]
    return jnp.concatenate([x1 * cos - x2 * sin, x2 * cos + x1 * sin], axis=-1).astype(x.dtype)


def banded_attention(q, k, v, n):
    G, L, H, hd = q.shape
    nb = -(-L // n)
    Lp = nb * n
    pad = Lp - L
    qp = jnp.pad(q, ((0, 0), (0, pad), (0, 0), (0, 0))).reshape(G, nb, n, H, hd)
    kp = jnp.pad(k, ((0, 0), (n, pad + n), (0, 0), (0, 0))).reshape(G, nb + 2, n, H, hd)
    vp = jnp.pad(v, ((0, 0), (n, pad + n), (0, 0), (0, 0))).reshape(G, nb + 2, n, H, hd)
    kw = jnp.concatenate([kp[:, :-2], kp[:, 1:-1], kp[:, 2:]], axis=2)
    vw = jnp.concatenate([vp[:, :-2], vp[:, 1:-1], vp[:, 2:]], axis=2)
    blk = jnp.arange(nb)[:, None] * n
    qpos = blk + jnp.arange(n)[None, :]
    kpos = blk - n + jnp.arange(3 * n)[None, :]
    rel = kpos[:, None, :] - qpos[:, :, None]
    valid = (jnp.abs(rel) <= n) & (kpos[:, None, :] >= 0) & (kpos[:, None, :] < L)
    scores = jnp.einsum('gbqhd,gbkhd->gbhqk', qp, kw).astype(jnp.float32) * (HEAD_DIM ** -0.5)
    scores = jnp.where(valid[None, :, None], scores, NEG_INF)
    lse = jax.nn.logsumexp(scores, axis=-1, keepdims=True)
    p = jnp.exp(scores - lse).astype(v.dtype)
    out = jnp.einsum('gbhqk,gbkhd->gbqhd', p, vw).reshape(G, Lp, H, hd)[:, :L]
    lse = lse[..., 0].transpose(0, 1, 3, 2).reshape(G, Lp, H)[:, :L]
    return out, lse


def dilated_attention(q, k, v, r, n):
    B, S, H, hd = q.shape
    L = S // r

    def to_classes(t):
        return t.reshape(B, L, r, H, hd).transpose(0, 2, 1, 3, 4).reshape(B * r, L, H, hd)

    out, lse = banded_attention(to_classes(q), to_classes(k), to_classes(v), n)
    out = out.reshape(B, r, L, H, hd).transpose(0, 2, 1, 3, 4).reshape(B, S, H, hd)
    lse = lse.reshape(B, r, L, H).transpose(0, 2, 1, 3).reshape(B, S, H)
    return out, lse


def attention_branch(qa, ka, va, q_gain, k_gain, pos):
    B, S, _ = qa.shape
    nh = N_ATT_GROUPS * ATT_HEADS
    q = rope(rms_norm(qa.reshape(B, S, nh, HEAD_DIM), q_gain), pos).reshape(B, S, N_ATT_GROUPS, ATT_HEADS, HEAD_DIM)
    k = rope(rms_norm(ka.reshape(B, S, nh, HEAD_DIM), k_gain), pos).reshape(B, S, N_ATT_GROUPS, ATT_HEADS, HEAD_DIM)
    v = va.reshape(B, S, N_ATT_GROUPS, ATT_HEADS, HEAD_DIM)
    outs, lses = [], []
    for gi, (w, r) in enumerate(WINDOW_DILATIONS):
        o, l = dilated_attention(q[:, :, gi], k[:, :, gi], v[:, :, gi], r, w // (2 * r))
        outs.append(o)
        lses.append(l)
    alpha = jax.nn.softmax(jnp.stack(lses, axis=0), axis=0)[..., None]
    att = jnp.sum(alpha * jnp.stack(outs, axis=0).astype(jnp.float32), axis=0)
    return att.reshape(B, S, ATT_WIDTH).astype(qa.dtype)


def hgrn2_scan(z, q, i, lb):
    B, S, H, dk = z.shape
    dv = i.shape[-1]
    C = HGRN_CHUNK
    nc = S // C
    zf = z.astype(jnp.float32)
    log_f = jnp.logaddexp(jnp.log(lb), jnp.log1p(-lb) + jax.nn.log_sigmoid(zf))
    kf = (1.0 - lb) * jax.nn.sigmoid(-zf)

    def chunks(t):
        return t.reshape(B, nc, C, H, t.shape[-1]).transpose(1, 0, 2, 3, 4)

    xs = (chunks(q.astype(jnp.float32)), chunks(kf), chunks(i.astype(jnp.float32)), chunks(log_f))
    causal = jnp.tril(jnp.ones((C, C), dtype=bool))[None, :, :, None, None]

    def step(state, inp):
        qc, kc, ic, lfc = inp
        b = jnp.cumsum(lfc, axis=1)
        diff = b[:, :, None] - b[:, None, :]
        decay = jnp.exp(jnp.where(causal, diff, -jnp.inf))
        a = jnp.einsum('bthk,bshk,btshk->bhts', qc, kc, decay)
        o = jnp.einsum('bhts,bshv->bthv', a, ic) + jnp.einsum('bthk,bhkv->bthv', qc * jnp.exp(b), state)
        b_end = b[:, -1]
        k_end = kc * jnp.exp(b_end[:, None] - b)
        state = jnp.exp(b_end)[..., None] * state + jnp.einsum('bshk,bshv->bhkv', k_end, ic)
        return state, o

    state0 = jnp.zeros((B, H, dk, dv), jnp.float32)
    _, o = lax.scan(step, state0, xs)
    return o.transpose(1, 0, 2, 3, 4).reshape(B, S, H, dv)


def hgrn_branch(zf, zb, qh, ih, gh, lb_fwd, lb_bwd, out_gain):
    B, S, _ = qh.shape
    z_f = zf.reshape(B, S, HGRN_HEADS, HGRN_DK)
    z_b = zb.reshape(B, S, HGRN_HEADS, HGRN_DK)
    q = qh.reshape(B, S, HGRN_HEADS, HGRN_DK)
    i = ih.reshape(B, S, HGRN_HEADS, HGRN_DV)
    lbf = lb_fwd.reshape(HGRN_HEADS, HGRN_DK)
    lbb = lb_bwd.reshape(HGRN_HEADS, HGRN_DK)
    o_fwd = hgrn2_scan(z_f, q, i, lbf)
    o_bwd = jnp.flip(hgrn2_scan(jnp.flip(z_b, 1), jnp.flip(q, 1), jnp.flip(i, 1), lbb), axis=1)
    o = rms_norm(o_fwd + o_bwd, out_gain) * jax.nn.silu(gh.reshape(B, S, HGRN_HEADS, HGRN_DV).astype(jnp.float32))
    return o.reshape(B, S, HGRN_WIDTH).astype(qh.dtype)


def fourier_branch(u):
    B, S, _ = u.shape
    uf = u.astype(jnp.float32).reshape(B, S, FNET_GROUPS, FNET_GROUP_DIM).transpose(0, 2, 1, 3)
    y = jnp.fft.fft2(uf, axes=(-2, -1), norm='ortho').real
    return y.transpose(0, 2, 1, 3).reshape(B, S, FNET_WIDTH).astype(u.dtype)


def encoder_layer(x, lb_fwd, lb_bwd, ffn1_norm, ffn1_w_gate, ffn1_w_up, ffn1_w_down,
                  mix_norm, w_in, q_norm, k_norm, w_att_out, hgrn_out_norm, w_hgrn_out,
                  w_fnet_out, w_out, ffn2_norm, ffn2_w_gate, ffn2_w_up, ffn2_w_down):
    B, S, _ = x.shape
    pos = jnp.arange(S)
    x = x + 0.5 * swiglu(rms_norm(x, ffn1_norm), ffn1_w_gate, ffn1_w_up, ffn1_w_down)
    h = rms_norm(x, mix_norm)
    proj = h @ w_in
    qa, ka, va, zf, zb, qh, ih, gh, uf, gl = jnp.split(proj, SPLIT_POINTS, axis=-1)
    a = attention_branch(qa, ka, va, q_norm, k_norm, pos) @ w_att_out
    bh = hgrn_branch(zf, zb, qh, ih, gh, lb_fwd, lb_bwd, hgrn_out_norm) @ w_hgrn_out
    c = fourier_branch(uf) @ w_fnet_out
    gates = jax.nn.sigmoid(gl.reshape(B, S, N_BRANCHES, D_MODEL))
    merged = gates[:, :, 0] * a + gates[:, :, 1] * bh + gates[:, :, 2] * c
    x = x + merged @ w_out
    x = x + 0.5 * swiglu(rms_norm(x, ffn2_norm), ffn2_w_gate, ffn2_w_up, ffn2_w_down)
    return x


def setup_inputs(seed: int = 0) -> dict:
    key = jax.random.key(seed)
    ks = jax.random.split(key, 24)
    f32 = jnp.float32

    def nrm(k, shape, fan_in):
        return jax.random.normal(k, shape, f32) * (fan_in ** -0.5)

    def gain(k, shape):
        return 1.0 + 0.02 * jax.random.normal(k, shape, f32)

    return {
        'x_prompt': jax.random.normal(ks[0], (BATCH, SEQ, D_MODEL), f32),
        'x_sample': jax.random.normal(ks[1], (DEC_BATCH, DEC_SEQ, D_MODEL), f32),
        'hgrn_lb_logits': 0.1 * jax.random.normal(ks[2], (DEPTH, 2, HGRN_KEY_COLS), f32),
        'ffn1_norm': gain(ks[3], (DEPTH, D_MODEL)),
        'ffn1_w_gate': nrm(ks[4], (DEPTH, D_MODEL, D_FF), D_MODEL),
        'ffn1_w_up': nrm(ks[5], (DEPTH, D_MODEL, D_FF), D_MODEL),
        'ffn1_w_down': nrm(ks[6], (DEPTH, D_FF, D_MODEL), D_FF),
        'mix_norm': gain(ks[7], (DEPTH, D_MODEL)),
        'w_in': nrm(ks[8], (DEPTH, D_MODEL, IN_COLS), D_MODEL),
        'q_norm': gain(ks[9], (DEPTH, HEAD_DIM)),
        'k_norm': gain(ks[10], (DEPTH, HEAD_DIM)),
        'w_att_out': nrm(ks[11], (DEPTH, ATT_WIDTH, D_MODEL), ATT_WIDTH),
        'hgrn_out_norm': gain(ks[12], (DEPTH, HGRN_DV)),
        'w_hgrn_out': nrm(ks[13], (DEPTH, HGRN_WIDTH, D_MODEL), HGRN_WIDTH),
        'w_fnet_out': nrm(ks[14], (DEPTH, FNET_WIDTH, D_MODEL), FNET_WIDTH),
        'w_out': nrm(ks[15], (DEPTH, D_MODEL, D_MODEL), D_MODEL),
        'ffn2_norm': gain(ks[16], (DEPTH, D_MODEL)),
        'ffn2_w_gate': nrm(ks[17], (DEPTH, D_MODEL, D_FF), D_MODEL),
        'ffn2_w_up': nrm(ks[18], (DEPTH, D_MODEL, D_FF), D_MODEL),
        'ffn2_w_down': nrm(ks[19], (DEPTH, D_FF, D_MODEL), D_FF),
    }


def reference(x_prompt, x_sample, hgrn_lb_logits, ffn1_norm, ffn1_w_gate, ffn1_w_up, ffn1_w_down,
              mix_norm, w_in, q_norm, k_norm, w_att_out, hgrn_out_norm, w_hgrn_out, w_fnet_out,
              w_out, ffn2_norm, ffn2_w_gate, ffn2_w_up, ffn2_w_down):
    p = jax.nn.softmax(hgrn_lb_logits.astype(jnp.float32), axis=0)
    cum = jnp.cumsum(p, axis=0)
    lb = cum - cum[0:1]

    def trunk(x):
        for l in range(DEPTH):
            x = encoder_layer(x, lb[l, 0], lb[l, 1], ffn1_norm[l], ffn1_w_gate[l], ffn1_w_up[l], ffn1_w_down[l],
                              mix_norm[l], w_in[l], q_norm[l], k_norm[l], w_att_out[l], hgrn_out_norm[l],
                              w_hgrn_out[l], w_fnet_out[l], w_out[l], ffn2_norm[l], ffn2_w_gate[l],
                              ffn2_w_up[l], ffn2_w_down[l])
        return x

    y_prompt = trunk(x_prompt)
    y_sample = trunk(x_sample)
    return (y_prompt, y_sample)
```

```python
import functools

import numpy as np
import jax
import jax.numpy as jnp
from jax import lax
from jax.experimental import pallas as pl
from jax.experimental.pallas import tpu as pltpu

F32 = jnp.float32
BF16 = jnp.bfloat16

D_MODEL = 1024
D_FF = 2816
HEAD_DIM = 64
ATT_HEADS = 8
WINDOW_DILATIONS = ((128, 1), (512, 4), (2048, 16))
ATT_WIDTH = ATT_HEADS * HEAD_DIM
ATT_COLS = len(WINDOW_DILATIONS) * ATT_WIDTH
ROPE_THETA = 10000.0
HGRN_HEADS = 4
HGRN_DK = 128
HGRN_WIDTH = 512
HGRN_CHUNK = 64
HGRN_SUB = 16
FNET_WIDTH = 512
FNET_GROUP_DIM = 128
EPS = 1e-6
NEG_INF = -1e30
IN_COLS = 10752
BAND = 64

QK_COLS = 2 * ATT_COLS
VZ_COLS = ATT_COLS + 2 * HGRN_WIDTH
BD_COLS = IN_COLS - QK_COLS - VZ_COLS

V7X_VMEM_LIMIT_BYTES = 56 * 1024 * 1024
FF_CHUNKS = ((0, 1536), (1536, 2816))


def _cparams(sem):
    return pltpu.CompilerParams(dimension_semantics=sem, vmem_limit_bytes=V7X_VMEM_LIMIT_BYTES)


def _rms_rows(x, g):
    ms = jnp.mean(x * x, axis=-1, keepdims=True)
    return x * lax.rsqrt(ms + EPS) * g


def _ffn_body(x_ref, g_ref, wg_ref, wu_ref, wd_ref, o_ref):
    x = x_ref[...]
    h = _rms_rows(x, g_ref[...]).astype(BF16)
    acc = None
    for lo, hi in FF_CHUNKS:
        gate = jnp.dot(h, wg_ref[:, lo:hi], preferred_element_type=F32)
        up = jnp.dot(h, wu_ref[:, lo:hi], preferred_element_type=F32)
        a = (gate * jax.nn.sigmoid(gate) * up).astype(BF16)
        part = jnp.dot(a, wd_ref[lo:hi, :], preferred_element_type=F32)
        acc = part if acc is None else acc + part
    o_ref[...] = x + 0.5 * acc


def _ffn(x2d, g, wg, wu, wd, tm):
    t = x2d.shape[0]
    const = lambda i: (0, 0)
    one = pl.Buffered(1)
    return pl.pallas_call(
        _ffn_body,
        grid=(t // tm,),
        in_specs=[
            pl.BlockSpec((tm, D_MODEL), lambda i: (i, 0)),
            pl.BlockSpec((1, D_MODEL), const),
            pl.BlockSpec((D_MODEL, D_FF), const, pipeline_mode=one),
            pl.BlockSpec((D_MODEL, D_FF), const, pipeline_mode=one),
            pl.BlockSpec((D_FF, D_MODEL), const, pipeline_mode=one),
        ],
        out_specs=pl.BlockSpec((tm, D_MODEL), lambda i: (i, 0)),
        out_shape=jax.ShapeDtypeStruct((t, D_MODEL), F32),
        compiler_params=_cparams(("parallel",)),
        name="ffn",
    )(x2d, g, wg, wu, wd)


def _proj_plain_body(x_ref, g_ref, w_ref, o_ref, h_scr):
    @pl.when(pl.program_id(1) == 0)
    def _():
        h_scr[...] = _rms_rows(x_ref[...], g_ref[...]).astype(BF16)

    o_ref[...] = jnp.dot(h_scr[...], w_ref[...], preferred_element_type=F32).astype(o_ref.dtype)


def _proj_rope_body(x_ref, g_ref, w_ref, gain_ref, cos_ref, sin_ref, bd_ref, o_ref, h_scr):
    @pl.when(pl.program_id(1) == 0)
    def _():
        h_scr[...] = _rms_rows(x_ref[...], g_ref[...]).astype(BF16)

    acc = jnp.dot(h_scr[...], w_ref[...], preferred_element_type=F32)
    tm, tn = acc.shape
    lane = lax.broadcasted_iota(jnp.int32, (tm, 128), 1)
    first_half = (lane % HEAD_DIM) < (HEAD_DIM // 2)
    cos = cos_ref[...]
    sin = sin_ref[...]
    for s in range(tn // 256):
        a = acc[:, s * 256:(s + 1) * 256]
        ms = jnp.dot((a * a).astype(BF16), bd_ref[...], preferred_element_type=F32)
        y = a * lax.rsqrt(ms + EPS) * gain_ref[:, s * 256:(s + 1) * 256]
        for t in range(2):
            y1 = y[:, t * 128:(t + 1) * 128]
            yr = jnp.where(first_half, pltpu.roll(y1, 128 - HEAD_DIM // 2, 1), pltpu.roll(y1, HEAD_DIM // 2, 1))
            c0 = s * 256 + t * 128
            o_ref[:, c0:c0 + 128] = y1 * cos + yr * sin


def _proj(x2d, g, w, col_off, n_cols, out_dtype, tm, tn, rope=None):
    t = x2d.shape[0]
    off = col_off // tn
    in_specs = [
        pl.BlockSpec((tm, D_MODEL), lambda i, j: (i, 0)),
        pl.BlockSpec((1, D_MODEL), lambda i, j: (0, 0)),
        pl.BlockSpec((D_MODEL, tn), lambda i, j: (0, j + off)),
    ]
    args = [x2d, g, w]
    if rope is None:
        body = _proj_plain_body
    else:
        gains, cos, sin, bd, seq = rope
        nsb = seq // tm
        in_specs += [
            pl.BlockSpec((None, 1, tn), lambda i, j: (j, 0, 0)),
            pl.BlockSpec((tm, 128), lambda i, j: (i % nsb, 0)),
            pl.BlockSpec((tm, 128), lambda i, j: (i % nsb, 0)),
            pl.BlockSpec((256, 256), lambda i, j: (0, 0)),
        ]
        args += [gains, cos, sin, bd]
        body = _proj_rope_body
    return pl.pallas_call(
        body,
        grid=(t // tm, n_cols // tn),
        in_specs=in_specs,
        out_specs=pl.BlockSpec((tm, tn), lambda i, j: (i, j)),
        out_shape=jax.ShapeDtypeStruct((t, n_cols), out_dtype),
        scratch_shapes=[pltpu.VMEM((tm, D_MODEL), BF16)],
        compiler_params=_cparams(("parallel", "arbitrary")),
        name="proj_rope" if rope is not None else "proj",
    )(*args)


def _attn_body(q_ref, kp_ref, kc_ref, kn_ref, vp_ref, vc_ref, vn_ref, o_ref, lse_ref,
               qs, ks, vs, os_, ls, tmp, *, r, tl, sq, seq_l):
    blk = pl.program_id(1)
    scale = HEAD_DIM ** -0.5
    nslab = ATT_WIDTH // 128

    def split_classes(src_ref, dst_ref, row0, n, mul=None):
        if r == 1:
            v = src_ref[...]
            dst_ref[0, row0:row0 + n, :] = (v if mul is None else v * mul).astype(BF16)
            return
        for j in range(nslab):
            tmp[j, 0:n * r, :] = src_ref[:, j * 128:(j + 1) * 128]
        for c in range(r):
            for j in range(nslab):
                v = tmp[j, pl.ds(c, n, stride=r), :]
                dst_ref[c, row0:row0 + n, j * 128:(j + 1) * 128] = (v if mul is None else v * mul).astype(BF16)

    split_classes(q_ref, qs, 0, tl, scale)
    split_classes(kp_ref, ks, 0, BAND)
    split_classes(kc_ref, ks, BAND, tl)
    split_classes(kn_ref, ks, BAND + tl, BAND)
    split_classes(vp_ref, vs, 0, BAND)
    split_classes(vc_ref, vs, BAND, tl)
    split_classes(vn_ref, vs, BAND + tl, BAND)

    nsub = tl // sq
    kw = sq + 2 * BAND
    qi = lax.broadcasted_iota(jnp.int32, (sq, kw), 0)
    ki = lax.broadcasted_iota(jnp.int32, (sq, kw), 1)
    in_band = jnp.abs(ki - BAND - qi) <= BAND
    lane = lax.broadcasted_iota(jnp.int32, (sq, 128), 1)
    lane_lo = lane < HEAD_DIM
    lane_grp = lane // 16

    def unit(u, carry):
        c = u // nsub
        q0 = pl.multiple_of((u % nsub) * sq, sq)
        kpos = blk * tl + q0 - BAND + ki
        valid = in_band & (kpos >= 0) & (kpos < seq_l)
        q2 = qs[c, pl.ds(q0, sq), :]
        k2 = ks[c, pl.ds(q0, kw), :]
        v2 = vs[c, pl.ds(q0, kw), :]
        lse_tile = jnp.zeros((sq, 128), F32)
        for hp in range(ATT_HEADS // 2):
            sl = slice(hp * 128, (hp + 1) * 128)
            q128, k128, v128 = q2[:, sl], k2[:, sl], v2[:, sl]
            outs = []
            for hh in range(2):
                qm = jnp.where(lane_lo if hh == 0 else ~lane_lo, q128, jnp.zeros_like(q128))
                s = lax.dot_general(qm, k128, (((1,), (1,)), ((), ())), preferred_element_type=F32)
                s = jnp.where(valid, s, NEG_INF)
                m = jnp.max(s, axis=-1, keepdims=True)
                p = jnp.exp(s - m)
                l = jnp.sum(p, axis=-1, keepdims=True)
                pv = jnp.dot(p.astype(BF16), v128, preferred_element_type=F32)
                outs.append(pv / l)
                lse_tile = jnp.where(lane_grp == 2 * hp + hh, m + jnp.log(l), lse_tile)
            os_[c, pl.ds(q0, sq), sl] = jnp.where(lane_lo, outs[0], outs[1])
        ls[c, pl.ds(q0, sq), :] = lse_tile
        return carry

    lax.fori_loop(0, r * nsub, unit, 0)

    if r == 1:
        o_ref[...] = os_[0]
        lse_ref[...] = ls[0]
    else:
        for c in range(r):
            for j in range(nslab):
                tmp[j, pl.ds(c, tl, stride=r), :] = os_[c, :, j * 128:(j + 1) * 128]
            lse_ref[pl.ds(c, tl, stride=r), :] = ls[c]
        for j in range(nslab):
            o_ref[:, j * 128:(j + 1) * 128] = tmp[j]


def _attention_group(qk, vz, gi, r, ts):
    b, s, _ = qk.shape
    halo = BAND * r
    tl = ts // r
    sq = min(128, tl)
    nh = ts // halo
    last = s // halo - 1
    kcol = len(WINDOW_DILATIONS) + gi
    cur = lambda col: pl.BlockSpec((None, ts, ATT_WIDTH), lambda bi, i: (bi, i, col))
    single = pl.Buffered(1)
    prv = lambda col: pl.BlockSpec((None, halo, ATT_WIDTH), lambda bi, i: (bi, jnp.maximum(i * nh - 1, 0), col),
                                   pipeline_mode=single)
    nxt = lambda col: pl.BlockSpec((None, halo, ATT_WIDTH), lambda bi, i: (bi, jnp.minimum((i + 1) * nh, last), col),
                                   pipeline_mode=single)
    body = functools.partial(_attn_body, r=r, tl=tl, sq=sq, seq_l=s // r)
    return pl.pallas_call(
        body,
        grid=(b, s // ts),
        in_specs=[cur(gi), prv(kcol), cur(kcol), nxt(kcol), prv(gi), cur(gi), nxt(gi)],
        out_specs=[pl.BlockSpec((None, ts, ATT_WIDTH), lambda bi, i: (bi, i, 0)),
                   pl.BlockSpec((None, ts, 128), lambda bi, i: (bi, i, 0))],
        out_shape=[jax.ShapeDtypeStruct((b, s, ATT_WIDTH), F32), jax.ShapeDtypeStruct((b, s, 128), F32)],
        scratch_shapes=[
            pltpu.VMEM((r, tl, ATT_WIDTH), BF16),
            pltpu.VMEM((r, tl + 2 * BAND, ATT_WIDTH), BF16),
            pltpu.VMEM((r, tl + 2 * BAND, ATT_WIDTH), BF16),
            pltpu.VMEM((r, tl, ATT_WIDTH), F32),
            pltpu.VMEM((r, tl, 128), F32),
            pltpu.VMEM((ATT_WIDTH // 128, ts, 128), F32),
        ],
        compiler_params=_cparams(("parallel", "parallel")),
        name=f"attn_r{r}",
    )(qk, qk, qk, qk, vz, vz, vz)


def _hgrn_body(z_ref, q_ref, i_ref, lbc_ref, o_ref, st_ref, *, rev, nch):
    @pl.when(pl.program_id(1) == 0)
    def _():
        st_ref[...] = jnp.zeros_like(st_ref)

    ch, sb = HGRN_CHUNK, HGRN_SUB
    row = lax.broadcasted_iota(jnp.int32, (ch, ch), 0)
    col = lax.broadcasted_iota(jnp.int32, (ch, ch), 1)
    tri = jnp.where((col >= row) if rev else (col <= row), 1.0, 0.0).astype(BF16)
    ones = jnp.ones((HGRN_DK, HGRN_DK), BF16)
    tmod = lax.broadcasted_iota(jnp.int32, (ch, HGRN_DK), 0) % sb
    nt = (((1,), (1,)), ((), ()))
    tn = (((0,), (0,)), ((), ()))

    def chunk(ci, carry):
        c = (nch - 1 - ci) if rev else ci
        r0 = pl.multiple_of(c * ch, ch)
        for h in range(HGRN_HEADS):
            sl = slice(h * HGRN_DK, (h + 1) * HGRN_DK)
            z = z_ref[pl.ds(r0, ch), sl]
            q = q_ref[pl.ds(r0, ch), sl].astype(F32)
            iv = i_ref[pl.ds(r0, ch), sl]
            ivf = iv.astype(F32)
            log_lb, log1m_lb, one_m_lb = lbc_ref[0:1, sl], lbc_ref[1:2, sl], lbc_ref[2:3, sl]
            e = jnp.exp(-jnp.abs(z))
            log_sig = jnp.minimum(z, 0.0) - jnp.log1p(e)
            cc = log1m_lb + log_sig
            lf = jnp.maximum(log_lb, cc) + jnp.log1p(jnp.exp(-jnp.abs(log_lb - cc)))
            kf = one_m_lb * (jnp.where(z >= 0, e, 1.0) / (1.0 + e))
            hi = lf.astype(BF16)
            lo = (lf - hi.astype(F32)).astype(BF16)
            b = jnp.dot(tri, hi, preferred_element_type=F32) + jnp.dot(tri, lo, preferred_element_type=F32)

            o = jnp.zeros((ch, HGRN_DK), F32)
            for d in range(sb):
                if d == 0:
                    p, i_d = q * kf, ivf
                else:
                    sh = (ch - d) if rev else d
                    k_d, b_d, i_d = pltpu.roll(kf, sh, 0), pltpu.roll(b, sh, 0), pltpu.roll(ivf, sh, 0)
                    ok = (tmod + d <= sb - 1) if rev else (tmod >= d)
                    p = q * k_d * jnp.exp(jnp.where(ok, b - b_d, -jnp.inf))
                o = o + jnp.dot(p.astype(BF16), ones, preferred_element_type=F32) * i_d

            parts = []
            for blk in range(ch // sb):
                lo_r, hi_r = blk * sb, (blk + 1) * sb
                if rev:
                    e0, e1, ref_row = hi_r, ch, hi_r
                else:
                    e0, e1, ref_row = 0, lo_r, lo_r - 1
                if e1 - e0 <= 0:
                    parts.append(jnp.zeros((sb, HGRN_DK), F32))
                    continue
                b_ref = b[ref_row:ref_row + 1, :]
                q_s = (q[lo_r:hi_r] * jnp.exp(b[lo_r:hi_r] - b_ref)).astype(BF16)
                k_s = (kf[e0:e1] * jnp.exp(b_ref - b[e0:e1])).astype(BF16)
                a = lax.dot_general(q_s, k_s, nt, preferred_element_type=F32)
                parts.append(jnp.dot(a.astype(BF16), iv[e0:e1], preferred_element_type=F32))
            o = o + jnp.concatenate(parts, axis=0)

            st = st_ref[h]
            b_end = b[0:1, :] if rev else b[ch - 1:ch, :]
            q_e = (q * jnp.exp(b)).astype(BF16)
            o = o + lax.dot_general(q_e, st.astype(BF16), nt, preferred_element_type=F32)
            k_e = (kf * jnp.exp(b_end - b)).astype(BF16)
            st_ref[h] = st * jnp.exp(b_end) + lax.dot_general(iv, k_e, tn, preferred_element_type=F32)
            o_ref[pl.ds(r0, ch), sl] = o
        return carry

    lax.fori_loop(0, nch, chunk, 0)


def _hgrn_scan(vz, bd, lbc, rev, ts):
    b, s, _ = vz.shape
    nb = s // ts
    order = (lambda j: nb - 1 - j) if rev else (lambda j: j)
    zcol = ATT_COLS // HGRN_WIDTH + (1 if rev else 0)
    spec = lambda col: pl.BlockSpec((None, ts, HGRN_WIDTH), lambda bi, j: (bi, order(j), col))
    body = functools.partial(_hgrn_body, rev=rev, nch=ts // HGRN_CHUNK)
    return pl.pallas_call(
        body,
        grid=(b, nb),
        in_specs=[spec(zcol), spec(0), spec(1), pl.BlockSpec((3, HGRN_WIDTH), lambda bi, j: (0, 0))],
        out_specs=spec(0),
        out_shape=jax.ShapeDtypeStruct((b, s, HGRN_WIDTH), F32),
        scratch_shapes=[pltpu.VMEM((HGRN_HEADS, HGRN_DK, HGRN_DK), F32)],
        compiler_params=_cparams(("parallel", "arbitrary")),
        name="hgrn_bwd" if rev else "hgrn_fwd",
    )(vz, bd, bd, lbc)


def _fnet_a_body(u_ref, cs_ref, m_ref, o_ref, *, n1, nt):
    ngrp = FNET_WIDTH // FNET_GROUP_DIM
    for s in range(nt):
        zr, zi = [], []
        for g in range(ngrp):
            c0 = s * FNET_WIDTH + g * FNET_GROUP_DIM
            zz = jnp.dot(u_ref[:, c0:c0 + FNET_GROUP_DIM], cs_ref[...], preferred_element_type=F32)
            zr.append(zz[:, :FNET_GROUP_DIM])
            zi.append(zz[:, FNET_GROUP_DIM:])
        zcat = jnp.concatenate([jnp.concatenate(zr, axis=1), jnp.concatenate(zi, axis=1)], axis=0).astype(BF16)
        res = jnp.dot(m_ref[s], zcat, preferred_element_type=F32)
        o_ref[:, s * FNET_WIDTH:(s + 1) * FNET_WIDTH] = res.astype(o_ref.dtype)


def _fnet_c_body(b_ref, cs_ref, o_ref, *, kt):
    for k in range(kt):
        bcat = jnp.concatenate([b_ref[0, k], b_ref[1, k]], axis=0)
        o_ref[:, k * FNET_WIDTH:(k + 1) * FNET_WIDTH] = jnp.dot(cs_ref[...], bcat, preferred_element_type=F32)


def _fnet_tables(n1, n2):
    n = n1 * n2
    k = np.arange(FNET_GROUP_DIM)
    ang = 2.0 * np.pi * ((k[:, None] * k[None, :]) % FNET_GROUP_DIM) / FNET_GROUP_DIM
    cs_ch = np.concatenate([np.cos(ang), -np.sin(ang)], axis=1) / np.sqrt(FNET_GROUP_DIM)
    k1 = np.arange(n1)[None, :, None]
    m1 = np.arange(n1)[None, None, :]
    s2 = np.arange(n2)[:, None, None]
    idx = (k1 * m1 * n2 + s2 * k1) % n
    th = 2.0 * np.pi * idx / n
    gc, gs = np.cos(th) / np.sqrt(n1), np.sin(th) / np.sqrt(n1)
    m = np.concatenate([np.concatenate([gc, gs], axis=2), np.concatenate([-gs, gc], axis=2)], axis=1)
    k2 = np.arange(n2)
    ang2 = 2.0 * np.pi * ((k2[:, None] * k2[None, :]) % n2) / n2
    cs_seq = np.concatenate([np.cos(ang2), np.sin(ang2)], axis=1) / np.sqrt(n2)
    return (jnp.asarray(cs_ch, BF16), jnp.asarray(m, BF16), jnp.asarray(cs_seq, BF16))


def _fourier_branch(u, n1, n2):
    b, s, _ = u.shape
    cs_ch, m, cs_seq = _fnet_tables(n1, n2)
    nt = 4
    kt = 8
    u2 = u.reshape(b, n1, n2 * FNET_WIDTH)
    a = pl.pallas_call(
        functools.partial(_fnet_a_body, n1=n1, nt=nt),
        grid=(b, n2 // nt),
        in_specs=[
            pl.BlockSpec((None, n1, nt * FNET_WIDTH), lambda bi, j: (bi, 0, j)),
            pl.BlockSpec((FNET_GROUP_DIM, 2 * FNET_GROUP_DIM), lambda bi, j: (0, 0)),
            pl.BlockSpec((nt, 2 * n1, 2 * n1), lambda bi, j: (j, 0, 0)),
        ],
        out_specs=pl.BlockSpec((None, 2 * n1, nt * FNET_WIDTH), lambda bi, j: (bi, 0, j)),
        out_shape=jax.ShapeDtypeStruct((b, 2 * n1, n2 * FNET_WIDTH), BF16),
        compiler_params=_cparams(("parallel", "parallel")),
        name="fnet_a",
    )(u2, cs_ch, m)
    a5 = a.reshape(b, 2, n1, n2, FNET_WIDTH)
    y = pl.pallas_call(
        functools.partial(_fnet_c_body, kt=kt),
        grid=(b, n1 // kt),
        in_specs=[
            pl.BlockSpec((None, 2, kt, n2, FNET_WIDTH), lambda bi, j: (bi, 0, j, 0, 0)),
            pl.BlockSpec((n2, 2 * n2), lambda bi, j: (0, 0)),
        ],
        out_specs=pl.BlockSpec((None, n2, kt * FNET_WIDTH), lambda bi, j: (bi, 0, j)),
        out_shape=jax.ShapeDtypeStruct((b, n2, n1 * FNET_WIDTH), F32),
        compiler_params=_cparams(("parallel", "parallel")),
        name="fnet_c",
    )(a5, cs_seq)
    return y.reshape(b, s, FNET_WIDTH)


def _merge_body(x_ref, o0_ref, o1_ref, o2_ref, l0_ref, l1_ref, l2_ref, hf_ref, hb_ref, gh_ref, y_ref,
                ga_ref, gb_ref, gc_ref, wa_ref, wh_ref, wf_ref, wo_ref, hg_ref, ex_ref, out_ref):
    lses = [l0_ref[...], l1_ref[...], l2_ref[...]]
    m = jnp.maximum(jnp.maximum(lses[0], lses[1]), lses[2])
    ws = [jnp.exp(l - m) for l in lses]
    inv = 1.0 / (ws[0] + ws[1] + ws[2])
    att = None
    for w, o_ref in zip(ws, (o0_ref, o1_ref, o2_ref)):
        alpha = jnp.dot((w * inv).astype(BF16), ex_ref[...], preferred_element_type=F32)
        term = alpha * o_ref[...]
        att = term if att is None else att + term
    a = jnp.dot(att.astype(BF16), wa_ref[...], preferred_element_type=F32)
    oh = hf_ref[...] + hb_ref[...]
    gh = gh_ref[...].astype(F32)
    heads = []
    for h in range(HGRN_HEADS):
        sl = slice(h * HGRN_DK, (h + 1) * HGRN_DK)
        gate = gh[:, sl]
        heads.append(_rms_rows(oh[:, sl], hg_ref[...]) * (gate * jax.nn.sigmoid(gate)))
    bh = jnp.dot(jnp.concatenate(heads, axis=1).astype(BF16), wh_ref[...], preferred_element_type=F32)
    c = jnp.dot(y_ref[...].astype(BF16), wf_ref[...], preferred_element_type=F32)
    sig = lambda ref: jax.nn.sigmoid(ref[...].astype(F32))
    merged = sig(ga_ref) * a + sig(gb_ref) * bh + sig(gc_ref) * c
    out_ref[...] = x_ref[...] + jnp.dot(merged.astype(BF16), wo_ref[...], preferred_element_type=F32)


def _merge(x2d, att_outs, att_lses, hf, hb, bd, y, wa, wh, wf, wo, hgain, expand, tm):
    t = x2d.shape[0]
    row = lambda w: pl.BlockSpec((tm, w), lambda i: (i, 0))
    const = lambda shape: pl.BlockSpec(shape, lambda i: (0, 0))
    gh_col = 2 * HGRN_WIDTH // HGRN_WIDTH
    gate_col0 = 4 * HGRN_WIDTH // D_MODEL
    gate = lambda k: pl.BlockSpec((tm, D_MODEL), lambda i: (i, gate_col0 + k))
    in_specs = ([row(D_MODEL)] + [row(ATT_WIDTH)] * 3 + [row(128)] * 3 + [row(HGRN_WIDTH)] * 2
                + [pl.BlockSpec((tm, HGRN_WIDTH), lambda i: (i, gh_col)),
                   row(FNET_WIDTH), gate(0), gate(1), gate(2),
                   const((ATT_WIDTH, D_MODEL)), const((HGRN_WIDTH, D_MODEL)), const((FNET_WIDTH, D_MODEL)),
                   const((D_MODEL, D_MODEL)), const((1, HGRN_DK)), const((128, ATT_WIDTH))])
    return pl.pallas_call(
        _merge_body,
        grid=(t // tm,),
        in_specs=in_specs,
        out_specs=row(D_MODEL),
        out_shape=jax.ShapeDtypeStruct((t, D_MODEL), F32),
        compiler_params=_cparams(("parallel",)),
        name="merge",
    )(x2d, *att_outs, *att_lses, hf, hb, bd, y, bd, bd, bd, wa, wh, wf, wo, hgain, expand)


def _rope_tables(seq):
    half = HEAD_DIM // 2
    inv_freq = ROPE_THETA ** (-jnp.arange(half, dtype=F32) / half)
    ang = jnp.arange(seq).astype(F32)[:, None] * inv_freq[None, :]
    cos, sin = jnp.cos(ang), jnp.sin(ang)
    cos128 = jnp.concatenate([cos, cos, cos, cos], axis=1)
    sin128 = jnp.concatenate([-sin, sin, -sin, sin], axis=1)
    return cos128, sin128


def _layer_weights(l, lb, ffn1_norm, ffn1_w_gate, ffn1_w_up, ffn1_w_down, mix_norm, w_in, q_norm, k_norm,
                   w_att_out, hgrn_out_norm, w_hgrn_out, w_fnet_out, w_out, ffn2_norm, ffn2_w_gate, ffn2_w_up,
                   ffn2_w_down):
    bf = lambda w: w[l].astype(BF16)
    vec = lambda v: v[l].astype(F32).reshape(1, -1)
    lbl = lb[l]
    lbc = jnp.stack([jnp.log(lbl), jnp.log1p(-lbl), 1.0 - lbl], axis=1)
    heads_per_tile = 512 // HEAD_DIM
    gains = jnp.stack([jnp.tile(q_norm[l], heads_per_tile)] * 3 + [jnp.tile(k_norm[l], heads_per_tile)] * 3)
    return dict(
        ffn1=(vec(ffn1_norm), bf(ffn1_w_gate), bf(ffn1_w_up), bf(ffn1_w_down)),
        ffn2=(vec(ffn2_norm), bf(ffn2_w_gate), bf(ffn2_w_up), bf(ffn2_w_down)),
        mix_norm=vec(mix_norm), w_in=bf(w_in), qk_gains=gains.astype(F32).reshape(6, 1, 512), lbc=lbc,
        wa=bf(w_att_out), wh=bf(w_hgrn_out), wf=bf(w_fnet_out), wo=bf(w_out), hgain=vec(hgrn_out_norm),
    )


def _encoder_layer(x, wts, consts):
    b, s, _ = x.shape
    t = b * s
    x2d = x.reshape(t, D_MODEL)
    x2d = _ffn(x2d, *wts["ffn1"], tm=512)
    rope = (wts["qk_gains"], consts["cos"], consts["sin"], consts["head_bd"], s)
    qk = _proj(x2d, wts["mix_norm"], wts["w_in"], 0, QK_COLS, F32, 1024, 512, rope=rope)
    vz = _proj(x2d, wts["mix_norm"], wts["w_in"], QK_COLS, VZ_COLS, F32, 1024, 512)
    bd = _proj(x2d, wts["mix_norm"], wts["w_in"], QK_COLS + VZ_COLS, BD_COLS, BF16, 1024, 512)
    qk3, vz3, bd3 = qk.reshape(b, s, QK_COLS), vz.reshape(b, s, VZ_COLS), bd.reshape(b, s, BD_COLS)
    outs, lses = [], []
    for gi, (w, r) in enumerate(WINDOW_DILATIONS):
        assert w // (2 * r) == BAND
        o, lse = _attention_group(qk3, vz3, gi, r, 1024)
        outs.append(o.reshape(t, ATT_WIDTH))
        lses.append(lse.reshape(t, 128))
    hf = _hgrn_scan(vz3, bd3, wts["lbc"][0], False, 512).reshape(t, HGRN_WIDTH)
    hb = _hgrn_scan(vz3, bd3, wts["lbc"][1], True, 512).reshape(t, HGRN_WIDTH)
    u = bd3[:, :, 3 * HGRN_WIDTH:3 * HGRN_WIDTH + FNET_WIDTH]
    y = _fourier_branch(u, consts["n1"], s // consts["n1"]).reshape(t, FNET_WIDTH)
    x2d = _merge(x2d, outs, lses, hf, hb, bd, y, wts["wa"], wts["wh"], wts["wf"], wts["wo"], wts["hgain"],
                 consts["expand"], tm=512)
    x2d = _ffn(x2d, *wts["ffn2"], tm=512)
    return x2d.reshape(b, s, D_MODEL)


def _trunk_consts(seq):
    cos, sin = _rope_tables(seq)
    hd = np.arange(256) // HEAD_DIM
    head_bd = jnp.asarray((hd[:, None] == hd[None, :]) / HEAD_DIM, BF16)
    lanes = np.arange(128)
    cols = np.arange(ATT_WIDTH)
    expand = jnp.asarray((lanes[:, None] == 16 * (cols[None, :] // HEAD_DIM)), BF16)
    return dict(cos=cos, sin=sin, head_bd=head_bd, expand=expand, n1=128)


def kernel(x_prompt, x_sample, hgrn_lb_logits, ffn1_norm, ffn1_w_gate, ffn1_w_up, ffn1_w_down, mix_norm, w_in,
           q_norm, k_norm, w_att_out, hgrn_out_norm, w_hgrn_out, w_fnet_out, w_out, ffn2_norm, ffn2_w_gate,
           ffn2_w_up, ffn2_w_down):
    depth = w_in.shape[0]
    p = jax.nn.softmax(hgrn_lb_logits.astype(F32), axis=0)
    cum = jnp.cumsum(p, axis=0)
    lb = cum - cum[0:1]
    layers = [_layer_weights(l, lb, ffn1_norm, ffn1_w_gate, ffn1_w_up, ffn1_w_down, mix_norm, w_in, q_norm,
                             k_norm, w_att_out, hgrn_out_norm, w_hgrn_out, w_fnet_out, w_out, ffn2_norm,
                             ffn2_w_gate, ffn2_w_up, ffn2_w_down) for l in range(depth)]

    def trunk(x):
        consts = _trunk_consts(x.shape[1])
        for wts in layers:
            x = _encoder_layer(x, wts, consts)
        return x

    return (trunk(x_prompt), trunk(x_sample))
```

```python
import functools

import numpy as np
import jax
import jax.numpy as jnp
from jax import lax
from jax.experimental import pallas as pl
from jax.experimental.pallas import tpu as pltpu

F32 = jnp.float32
BF16 = jnp.bfloat16

D_MODEL = 1024
D_FF = 2816
HEAD_DIM = 64
ATT_HEADS = 8
WINDOW_DILATIONS = ((128, 1), (512, 4), (2048, 16))
N_GROUPS = len(WINDOW_DILATIONS)
ATT_WIDTH = ATT_HEADS * HEAD_DIM
ATT_COLS = N_GROUPS * ATT_WIDTH
ROPE_THETA = 10000.0
HGRN_HEADS = 4
HGRN_DK = 128
HGRN_WIDTH = 512
HGRN_CHUNK = 64
HGRN_SUB = 16
FNET_WIDTH = 512
FNET_GROUP_DIM = 128
EPS = 1e-6
NEG_INF = -1e30
IN_COLS = 10752
BAND = 64
PERM = 256

QKV_COLS = 3 * ATT_COLS
Z_COLS = 2 * HGRN_WIDTH
BD_COLS = IN_COLS - QKV_COLS - Z_COLS

V7X_VMEM_LIMIT_BYTES = 56 * 1024 * 1024
FF_CHUNKS = ((0, 1536), (1536, 2816))
TOKEN_TILE = 512
ATT_CLASS_ROWS = {1: 1024, 4: 256, 16: 128}
HGRN_TILE = 512
ATT_ROWS_PER_TILE = 64


def _cparams(sem):
    return pltpu.CompilerParams(dimension_semantics=sem, vmem_limit_bytes=V7X_VMEM_LIMIT_BYTES)


def _rms_rows(x, g):
    ms = jnp.mean(x * x, axis=-1, keepdims=True)
    return x * lax.rsqrt(ms + EPS) * g


def _ffn_body(x_ref, g_ref, wg_ref, wu_ref, wd_ref, o_ref):
    x = x_ref[...]
    h = _rms_rows(x, g_ref[...]).astype(BF16)
    acc = None
    for lo, hi in FF_CHUNKS:
        gate = jnp.dot(h, wg_ref[:, lo:hi], preferred_element_type=F32)
        up = jnp.dot(h, wu_ref[:, lo:hi], preferred_element_type=F32)
        a = (gate * jax.nn.sigmoid(gate) * up).astype(BF16)
        part = jnp.dot(a, wd_ref[lo:hi, :], preferred_element_type=F32)
        acc = part if acc is None else acc + part
    o_ref[...] = x + 0.5 * acc


def _ffn(x2d, g, wg, wu, wd):
    t, tm = x2d.shape[0], TOKEN_TILE
    const = lambda i: (0, 0)
    one = pl.Buffered(1)
    return pl.pallas_call(
        _ffn_body,
        grid=(t // tm,),
        in_specs=[
            pl.BlockSpec((tm, D_MODEL), lambda i: (i, 0)),
            pl.BlockSpec((1, D_MODEL), const),
            pl.BlockSpec((D_MODEL, D_FF), const, pipeline_mode=one),
            pl.BlockSpec((D_MODEL, D_FF), const, pipeline_mode=one),
            pl.BlockSpec((D_FF, D_MODEL), const, pipeline_mode=one),
        ],
        out_specs=pl.BlockSpec((tm, D_MODEL), lambda i: (i, 0)),
        out_shape=jax.ShapeDtypeStruct((t, D_MODEL), F32),
        compiler_params=_cparams(("parallel",)),
        name="ffn",
    )(x2d, g, wg, wu, wd)


def _head_norm_rope(acc, gain, cos, sin, bd_ref):
    tm = acc.shape[0]
    lane = lax.broadcasted_iota(jnp.int32, (tm, 128), 1)
    first_half = (lane % HEAD_DIM) < (HEAD_DIM // 2)
    slabs = []
    for s in range(ATT_WIDTH // 256):
        a = acc[:, s * 256:(s + 1) * 256]
        ms = jnp.dot((a * a).astype(BF16), bd_ref[...], preferred_element_type=F32)
        y = a * lax.rsqrt(ms + EPS) * gain[:, s * 256:(s + 1) * 256]
        for t in range(2):
            y1 = y[:, t * 128:(t + 1) * 128]
            yr = jnp.where(first_half, pltpu.roll(y1, 128 - HEAD_DIM // 2, 1), pltpu.roll(y1, HEAD_DIM // 2, 1))
            slabs.append(y1 * cos + yr * sin)
    return jnp.concatenate(slabs, axis=1)


def _proj_qkv_body(x_ref, g_ref, w_ref, gq_ref, gk_ref, c1_ref, s1_ref, c4_ref, s4_ref, c16_ref, s16_ref,
                   bd_ref, p4_ref, p16_ref, *out_refs):
    tm = x_ref.shape[0]
    h = _rms_rows(x_ref[...], g_ref[...]).astype(BF16)

    def permuted(p_ref):
        blocks = [jnp.dot(p_ref[...], h[b * PERM:(b + 1) * PERM], preferred_element_type=F32).astype(BF16)
                  for b in range(tm // PERM)]
        return jnp.concatenate(blocks, axis=0)

    lhs = (h, permuted(p4_ref), permuted(p16_ref))
    tables = ((c1_ref, s1_ref), (c4_ref, s4_ref), (c16_ref, s16_ref))
    gains = (gq_ref[...] * HEAD_DIM ** -0.5, gk_ref[...])
    for gi, (_, r) in enumerate(WINDOW_DILATIONS):
        cos, sin = tables[gi][0][...], tables[gi][1][...]
        n = PERM // r
        for ti in range(3):
            col = ti * ATT_COLS + gi * ATT_WIDTH
            y = jnp.dot(lhs[gi], w_ref[:, col:col + ATT_WIDTH], preferred_element_type=F32)
            if ti < 2:
                y = _head_norm_rope(y, gains[ti], cos, sin, bd_ref)
            y = y.astype(BF16)
            o_ref = out_refs[3 * gi + ti]
            if r == 1:
                o_ref[0] = y
            else:
                for b in range(tm // PERM):
                    for c in range(r):
                        o_ref[c, b * n:(b + 1) * n, :] = y[b * PERM + c * n:b * PERM + (c + 1) * n, :]


def _proj_rest_body(x_ref, g_ref, w_ref, z_ref, bd_ref):
    h = _rms_rows(x_ref[...], g_ref[...]).astype(BF16)
    tn = 512
    for k in range(Z_COLS // tn):
        z_ref[:, k * tn:(k + 1) * tn] = jnp.dot(h, w_ref[:, k * tn:(k + 1) * tn], preferred_element_type=F32)
    for k in range(BD_COLS // tn):
        c0 = Z_COLS + k * tn
        bd_ref[:, k * tn:(k + 1) * tn] = jnp.dot(h, w_ref[:, c0:c0 + tn], preferred_element_type=F32).astype(BF16)


def _proj_qkv(x2d, g, w_qkv, gq, gk, consts, batch, seq):
    t, tm = x2d.shape[0], TOKEN_TILE
    nsb = seq // tm
    const = lambda i: (0, 0)
    tab = pl.BlockSpec((tm, 128), lambda i: (i % nsb, 0))
    in_specs = [
        pl.BlockSpec((tm, D_MODEL), lambda i: (i, 0)),
        pl.BlockSpec((1, D_MODEL), const),
        pl.BlockSpec((D_MODEL, QKV_COLS), const, pipeline_mode=pl.Buffered(1)),
        pl.BlockSpec((1, ATT_WIDTH), const), pl.BlockSpec((1, ATT_WIDTH), const),
        tab, tab, tab, tab, tab, tab,
        pl.BlockSpec((256, 256), const), pl.BlockSpec((PERM, PERM), const), pl.BlockSpec((PERM, PERM), const),
    ]
    out_specs, out_shape = [], []
    for _, r in WINDOW_DILATIONS:
        for _ in range(3):
            out_specs.append(pl.BlockSpec((None, r, tm // r, ATT_WIDTH), lambda i: (i // nsb, 0, i % nsb, 0)))
            out_shape.append(jax.ShapeDtypeStruct((batch, r, seq // r, ATT_WIDTH), BF16))
    rope = consts["rope"]
    return pl.pallas_call(
        _proj_qkv_body,
        grid=(t // tm,),
        in_specs=in_specs,
        out_specs=out_specs,
        out_shape=out_shape,
        compiler_params=_cparams(("parallel",)),
        name="proj_qkv",
    )(x2d, g, w_qkv, gq, gk, rope[1][0], rope[1][1], rope[4][0], rope[4][1], rope[16][0], rope[16][1],
      consts["head_bd"], consts["perm"][4], consts["perm"][16])


def _proj_rest(x2d, g, w_rest):
    t, tm = x2d.shape[0], TOKEN_TILE
    const = lambda i: (0, 0)
    return pl.pallas_call(
        _proj_rest_body,
        grid=(t // tm,),
        in_specs=[
            pl.BlockSpec((tm, D_MODEL), lambda i: (i, 0)),
            pl.BlockSpec((1, D_MODEL), const),
            pl.BlockSpec((D_MODEL, Z_COLS + BD_COLS), const, pipeline_mode=pl.Buffered(1)),
        ],
        out_specs=[pl.BlockSpec((tm, Z_COLS), lambda i: (i, 0)), pl.BlockSpec((tm, BD_COLS), lambda i: (i, 0))],
        out_shape=[jax.ShapeDtypeStruct((t, Z_COLS), F32), jax.ShapeDtypeStruct((t, BD_COLS), BF16)],
        compiler_params=_cparams(("parallel",)),
        name="proj_rest",
    )(x2d, g, w_rest)


def _lse_lane(h):
    return HEAD_DIM * (h % 2) + 16 * (h // 2)


def _attn_body(q_ref, kp_ref, kc_ref, kn_ref, vp_ref, vc_ref, vn_ref, pt_ref, o_ref, lse_ref,
               ks, vs, os_, ls, *, r, tl, seq_l):
    blk = pl.program_id(1)
    sq, kw = 128, 128 + 2 * BAND
    ks[:, 0:BAND, :] = kp_ref[...]
    ks[:, BAND:BAND + tl, :] = kc_ref[...]
    ks[:, BAND + tl:2 * BAND + tl, :] = kn_ref[...]
    vs[:, 0:BAND, :] = vp_ref[...]
    vs[:, BAND:BAND + tl, :] = vc_ref[...]
    vs[:, BAND + tl:2 * BAND + tl, :] = vn_ref[...]

    nsub = tl // sq
    hq = ATT_ROWS_PER_TILE
    qi = lax.broadcasted_iota(jnp.int32, (sq, kw), 0)
    ki = lax.broadcasted_iota(jnp.int32, (sq, kw), 1)
    in_band = jnp.abs(ki - BAND - qi) <= BAND
    lane = lax.broadcasted_iota(jnp.int32, (hq, 128), 1)
    lane_lo = lane < HEAD_DIM
    lane_pair = (lane % HEAD_DIM) // 16
    lane_k = lax.broadcasted_iota(jnp.int32, (kw, 128), 1) < HEAD_DIM
    nt = (((1,), (1,)), ((), ()))

    def unit(u, carry):
        c = u // nsub
        q0 = pl.multiple_of((u % nsub) * sq, sq)
        kpos = blk * tl + q0 - BAND + ki
        valid = in_band & (kpos >= 0) & (kpos < seq_l)
        q2 = q_ref[c, pl.ds(q0, sq), :]
        k2 = ks[c, pl.ds(q0, kw), :]
        v2 = vs[c, pl.ds(q0, kw), :]
        for half in range(sq // hq):
            r0, r1 = half * hq, (half + 1) * hq
            vmask = jnp.concatenate([valid[r0:r1], valid[r0:r1]], axis=0)
            lse_tile = jnp.zeros((hq, 128), F32)
            for hp in range(ATT_HEADS // 2):
                sl = slice(hp * 128, (hp + 1) * 128)
                q128, k128, v128 = q2[r0:r1, sl], k2[:, sl], v2[:, sl]
                zero = jnp.zeros_like(q128)
                qstack = jnp.concatenate([jnp.where(lane_lo, q128, zero), jnp.where(lane_lo, zero, q128)], axis=0)
                s = lax.dot_general(qstack, k128, nt, preferred_element_type=F32)
                s = jnp.where(vmask, s, NEG_INF)
                m = jnp.max(s, axis=-1, keepdims=True)
                p = jnp.exp(s - m).astype(BF16)
                one = jnp.ones_like(v128)
                pv0 = jnp.dot(p[0:hq], jnp.where(lane_k, v128, one), preferred_element_type=F32)
                pv1 = jnp.dot(p[hq:2 * hq], jnp.where(lane_k, one, v128), preferred_element_type=F32)
                num = jnp.where(lane_lo, pv0, pv1)
                den = pltpu.roll(jnp.where(lane_lo, pv1, pv0), HEAD_DIM, 1)
                os_[c, pl.ds(q0 + r0, hq), sl] = (num / den).astype(BF16)
                lse_pair = jnp.where(lane_lo, m[0:hq], m[hq:2 * hq]) + jnp.log(den)
                lse_tile = jnp.where(lane_pair == hp, lse_pair, lse_tile)
            ls[c, pl.ds(q0 + r0, hq), :] = lse_tile
        return carry

    lax.fori_loop(0, r * nsub, unit, 0)

    if r == 1:
        o_ref[...] = os_[0]
        lse_ref[...] = ls[0]
    else:
        n = PERM // r
        for b in range(r * tl // PERM):
            cat = jnp.concatenate([os_[c, b * n:(b + 1) * n, :] for c in range(r)], axis=0)
            o_ref[b * PERM:(b + 1) * PERM, :] = jnp.dot(pt_ref[...], cat, preferred_element_type=F32).astype(BF16)
        for c in range(r):
            lse_ref[pl.ds(c, tl, stride=r), :] = ls[c]


def _attention_group(q, k, v, perm_t, r):
    b, _, l, _ = q.shape
    tl = min(ATT_CLASS_ROWS[r], l)
    ts = r * tl
    nh = tl // BAND
    last = l // BAND - 1
    cur = pl.BlockSpec((None, r, tl, ATT_WIDTH), lambda bi, i: (bi, 0, i, 0))
    prv = pl.BlockSpec((None, r, BAND, ATT_WIDTH), lambda bi, i: (bi, 0, jnp.maximum(i * nh - 1, 0), 0))
    nxt = pl.BlockSpec((None, r, BAND, ATT_WIDTH), lambda bi, i: (bi, 0, jnp.minimum((i + 1) * nh, last), 0))
    body = functools.partial(_attn_body, r=r, tl=tl, seq_l=l)
    return pl.pallas_call(
        body,
        grid=(b, l // tl),
        in_specs=[cur, prv, cur, nxt, prv, cur, nxt, pl.BlockSpec((PERM, PERM), lambda bi, i: (0, 0))],
        out_specs=[pl.BlockSpec((None, ts, ATT_WIDTH), lambda bi, i: (bi, i, 0)),
                   pl.BlockSpec((None, ts, 128), lambda bi, i: (bi, i, 0))],
        out_shape=[jax.ShapeDtypeStruct((b, r * l, ATT_WIDTH), BF16), jax.ShapeDtypeStruct((b, r * l, 128), F32)],
        scratch_shapes=[
            pltpu.VMEM((r, tl + 2 * BAND, ATT_WIDTH), BF16),
            pltpu.VMEM((r, tl + 2 * BAND, ATT_WIDTH), BF16),
            pltpu.VMEM((r, tl, ATT_WIDTH), BF16),
            pltpu.VMEM((r, tl, 128), F32),
        ],
        compiler_params=_cparams(("parallel", "parallel")),
        name=f"attn_r{r}",
    )(q, k, k, k, v, v, v, perm_t)


def _hgrn_body(z_ref, q_ref, i_ref, lbc_ref, o_ref, st_ref, *, rev, nch):
    @pl.when(pl.program_id(1) == 0)
    def _():
        st_ref[...] = jnp.zeros_like(st_ref)

    ch, sb = HGRN_CHUNK, HGRN_SUB
    row = lax.broadcasted_iota(jnp.int32, (ch, ch), 0)
    col = lax.broadcasted_iota(jnp.int32, (ch, ch), 1)
    tri = jnp.where((col >= row) if rev else (col <= row), 1.0, 0.0).astype(BF16)
    dist = (col - row) if rev else (row - col)
    same_sub = (row // sb) == (col // sb)
    pair_dist = jnp.where(same_sub & (dist >= 0), dist, -1)
    earlier_sub = ((col // sb) > (row // sb)) if rev else ((col // sb) < (row // sb))
    nt = (((1,), (1,)), ((), ()))
    tn = (((0,), (0,)), ((), ()))
    step = (ch - 1) if rev else 1

    def chunk(ci, carry):
        c = (nch - 1 - ci) if rev else ci
        r0 = pl.multiple_of(c * ch, ch)
        for h in range(HGRN_HEADS):
            sl = slice(h * HGRN_DK, (h + 1) * HGRN_DK)
            z = z_ref[pl.ds(r0, ch), sl]
            q = q_ref[pl.ds(r0, ch), sl].astype(F32)
            iv = i_ref[pl.ds(r0, ch), sl]
            log_lb, log1m_lb = lbc_ref[0:1, sl], lbc_ref[1:2, sl]
            one_m_lb, lb = lbc_ref[2:3, sl], lbc_ref[3:4, sl]
            e = jnp.exp(-jnp.abs(z))
            inv = 1.0 / (1.0 + e)
            pos = z >= 0
            f = lb + one_m_lb * (jnp.where(pos, 1.0, e) * inv)
            kf = one_m_lb * (jnp.where(pos, e, 1.0) * inv)
            cc = log1m_lb + (jnp.minimum(z, 0.0) - jnp.log1p(e))
            lf = jnp.maximum(log_lb, cc) + jnp.log1p(jnp.exp(-jnp.abs(log_lb - cc)))
            hi = lf.astype(BF16)
            lo = (lf - hi.astype(F32)).astype(BF16)
            b = jnp.dot(tri, hi, preferred_element_type=F32) + jnp.dot(tri, lo, preferred_element_type=F32)

            a = jnp.zeros((ch, ch), F32)
            g = kf
            for d in range(sb):
                if d > 0:
                    g = pltpu.roll(g, step, 0) * f
                a = jnp.where(pair_dist == d, jnp.sum(q * g, axis=-1, keepdims=True), a)

            rows_off = []
            for blk in range(ch // sb):
                lo_r, hi_r = blk * sb, (blk + 1) * sb
                ref_row = hi_r if rev else lo_r - 1
                if ref_row < 0 or ref_row >= ch:
                    rows_off.append(jnp.zeros((sb, ch), F32))
                    continue
                b_ref = b[ref_row:ref_row + 1, :]
                q_s = (q[lo_r:hi_r] * jnp.exp(b[lo_r:hi_r] - b_ref)).astype(BF16)
                k_s = (kf * jnp.exp(jnp.minimum(b_ref - b, 0.0))).astype(BF16)
                rows_off.append(lax.dot_general(q_s, k_s, nt, preferred_element_type=F32))
            a = jnp.where(earlier_sub, jnp.concatenate(rows_off, axis=0), a)
            o = jnp.dot(a.astype(BF16), iv, preferred_element_type=F32)

            st = st_ref[h]
            b_end = b[0:1, :] if rev else b[ch - 1:ch, :]
            q_e = (q * jnp.exp(b)).astype(BF16)
            o = o + lax.dot_general(q_e, st.astype(BF16), nt, preferred_element_type=F32)
            k_e = (kf * jnp.exp(b_end - b)).astype(BF16)
            st_ref[h] = st * jnp.exp(b_end) + lax.dot_general(iv, k_e, tn, preferred_element_type=F32)
            o_ref[pl.ds(r0, ch), sl] = o
        return carry

    lax.fori_loop(0, nch, chunk, 0)


def _hgrn_scan(z, bd, lbc, rev):
    b, s, _ = z.shape
    ts = HGRN_TILE
    nb = s // ts
    order = (lambda j: nb - 1 - j) if rev else (lambda j: j)
    spec = lambda col: pl.BlockSpec((None, ts, HGRN_WIDTH), lambda bi, j: (bi, order(j), col))
    body = functools.partial(_hgrn_body, rev=rev, nch=ts // HGRN_CHUNK)
    return pl.pallas_call(
        body,
        grid=(b, nb),
        in_specs=[spec(1 if rev else 0), spec(0), spec(1), pl.BlockSpec((4, HGRN_WIDTH), lambda bi, j: (0, 0))],
        out_specs=spec(0),
        out_shape=jax.ShapeDtypeStruct((b, s, HGRN_WIDTH), F32),
        scratch_shapes=[pltpu.VMEM((HGRN_HEADS, HGRN_DK, HGRN_DK), F32)],
        compiler_params=_cparams(("parallel", "arbitrary")),
        name="hgrn_bwd" if rev else "hgrn_fwd",
    )(z, bd, bd, lbc)


def _fnet_a_body(u_ref, cs_ref, m_ref, o_ref, *, n1, nt):
    ngrp = FNET_WIDTH // FNET_GROUP_DIM
    for s in range(nt):
        zr, zi = [], []
        for g in range(ngrp):
            c0 = s * FNET_WIDTH + g * FNET_GROUP_DIM
            zz = jnp.dot(u_ref[:, c0:c0 + FNET_GROUP_DIM], cs_ref[...], preferred_element_type=F32)
            zr.append(zz[:, :FNET_GROUP_DIM])
            zi.append(zz[:, FNET_GROUP_DIM:])
        zcat = jnp.concatenate([jnp.concatenate(zr, axis=1), jnp.concatenate(zi, axis=1)], axis=0).astype(BF16)
        res = jnp.dot(m_ref[s], zcat, preferred_element_type=F32)
        o_ref[:, s * FNET_WIDTH:(s + 1) * FNET_WIDTH] = res.astype(o_ref.dtype)


def _fnet_c_body(b_ref, cs_ref, o_ref, *, kt):
    for k in range(kt):
        bcat = jnp.concatenate([b_ref[0, k], b_ref[1, k]], axis=0)
        o_ref[:, k * FNET_WIDTH:(k + 1) * FNET_WIDTH] = jnp.dot(cs_ref[...], bcat, preferred_element_type=F32)


def _fnet_tables(n1, n2):
    n = n1 * n2
    k = np.arange(FNET_GROUP_DIM)
    ang = 2.0 * np.pi * ((k[:, None] * k[None, :]) % FNET_GROUP_DIM) / FNET_GROUP_DIM
    cs_ch = np.concatenate([np.cos(ang), -np.sin(ang)], axis=1) / np.sqrt(FNET_GROUP_DIM)
    k1 = np.arange(n1)[None, :, None]
    m1 = np.arange(n1)[None, None, :]
    s2 = np.arange(n2)[:, None, None]
    idx = (k1 * m1 * n2 + s2 * k1) % n
    th = 2.0 * np.pi * idx / n
    gc, gs = np.cos(th) / np.sqrt(n1), np.sin(th) / np.sqrt(n1)
    m = np.concatenate([np.concatenate([gc, gs], axis=2), np.concatenate([-gs, gc], axis=2)], axis=1)
    k2 = np.arange(n2)
    ang2 = 2.0 * np.pi * ((k2[:, None] * k2[None, :]) % n2) / n2
    cs_seq = np.concatenate([np.cos(ang2), np.sin(ang2)], axis=1) / np.sqrt(n2)
    return (jnp.asarray(cs_ch, BF16), jnp.asarray(m, BF16), jnp.asarray(cs_seq, BF16))


def _fourier_branch(u, n1, n2):
    b, s, _ = u.shape
    cs_ch, m, cs_seq = _fnet_tables(n1, n2)
    nt = 4
    kt = 8
    u2 = u.reshape(b, n1, n2 * FNET_WIDTH)
    a = pl.pallas_call(
        functools.partial(_fnet_a_body, n1=n1, nt=nt),
        grid=(b, n2 // nt),
        in_specs=[
            pl.BlockSpec((None, n1, nt * FNET_WIDTH), lambda bi, j: (bi, 0, j)),
            pl.BlockSpec((FNET_GROUP_DIM, 2 * FNET_GROUP_DIM), lambda bi, j: (0, 0)),
            pl.BlockSpec((nt, 2 * n1, 2 * n1), lambda bi, j: (j, 0, 0)),
        ],
        out_specs=pl.BlockSpec((None, 2 * n1, nt * FNET_WIDTH), lambda bi, j: (bi, 0, j)),
        out_shape=jax.ShapeDtypeStruct((b, 2 * n1, n2 * FNET_WIDTH), BF16),
        compiler_params=_cparams(("parallel", "parallel")),
        name="fnet_a",
    )(u2, cs_ch, m)
    a5 = a.reshape(b, 2, n1, n2, FNET_WIDTH)
    y = pl.pallas_call(
        functools.partial(_fnet_c_body, kt=kt),
        grid=(b, n1 // kt),
        in_specs=[
            pl.BlockSpec((None, 2, kt, n2, FNET_WIDTH), lambda bi, j: (bi, 0, j, 0, 0)),
            pl.BlockSpec((n2, 2 * n2), lambda bi, j: (0, 0)),
        ],
        out_specs=pl.BlockSpec((None, n2, kt * FNET_WIDTH), lambda bi, j: (bi, 0, j)),
        out_shape=jax.ShapeDtypeStruct((b, n2, n1 * FNET_WIDTH), F32),
        compiler_params=_cparams(("parallel", "parallel")),
        name="fnet_c",
    )(a5, cs_seq)
    return y.reshape(b, s, FNET_WIDTH)


def _merge_body(x_ref, o0_ref, o1_ref, o2_ref, l0_ref, l1_ref, l2_ref, hf_ref, hb_ref, gh_ref, y_ref,
                ga_ref, gb_ref, gc_ref, wa_ref, wh_ref, wf_ref, wo_ref, hg_ref, ex_ref, out_ref):
    lses = [l0_ref[...], l1_ref[...], l2_ref[...]]
    m = jnp.maximum(jnp.maximum(lses[0], lses[1]), lses[2])
    ws = [jnp.exp(l - m) for l in lses]
    inv = 1.0 / (ws[0] + ws[1] + ws[2])
    att = None
    for w, o_ref in zip(ws, (o0_ref, o1_ref, o2_ref)):
        alpha = jnp.dot((w * inv).astype(BF16), ex_ref[...], preferred_element_type=F32)
        term = alpha * o_ref[...].astype(F32)
        att = term if att is None else att + term
    a = jnp.dot(att.astype(BF16), wa_ref[...], preferred_element_type=F32)
    oh = hf_ref[...] + hb_ref[...]
    gh = gh_ref[...].astype(F32)
    heads = []
    for h in range(HGRN_HEADS):
        sl = slice(h * HGRN_DK, (h + 1) * HGRN_DK)
        gate = gh[:, sl]
        heads.append(_rms_rows(oh[:, sl], hg_ref[...]) * (gate * jax.nn.sigmoid(gate)))
    bh = jnp.dot(jnp.concatenate(heads, axis=1).astype(BF16), wh_ref[...], preferred_element_type=F32)
    c = jnp.dot(y_ref[...].astype(BF16), wf_ref[...], preferred_element_type=F32)
    sig = lambda ref: jax.nn.sigmoid(ref[...].astype(F32))
    merged = sig(ga_ref) * a + sig(gb_ref) * bh + sig(gc_ref) * c
    out_ref[...] = x_ref[...] + jnp.dot(merged.astype(BF16), wo_ref[...], preferred_element_type=F32)


def _merge(x2d, att_outs, att_lses, hf, hb, bd, y, wa, wh, wf, wo, hgain, expand):
    t, tm = x2d.shape[0], TOKEN_TILE
    row = lambda w: pl.BlockSpec((tm, w), lambda i: (i, 0))
    const = lambda shape: pl.BlockSpec(shape, lambda i: (0, 0))
    gh_col = 2 * HGRN_WIDTH // HGRN_WIDTH
    gate_col0 = 4 * HGRN_WIDTH // D_MODEL
    gate = lambda k: pl.BlockSpec((tm, D_MODEL), lambda i: (i, gate_col0 + k))
    in_specs = ([row(D_MODEL)] + [row(ATT_WIDTH)] * 3 + [row(128)] * 3 + [row(HGRN_WIDTH)] * 2
                + [pl.BlockSpec((tm, HGRN_WIDTH), lambda i: (i, gh_col)),
                   row(FNET_WIDTH), gate(0), gate(1), gate(2),
                   const((ATT_WIDTH, D_MODEL)), const((HGRN_WIDTH, D_MODEL)), const((FNET_WIDTH, D_MODEL)),
                   const((D_MODEL, D_MODEL)), const((1, HGRN_DK)), const((128, ATT_WIDTH))])
    return pl.pallas_call(
        _merge_body,
        grid=(t // tm,),
        in_specs=in_specs,
        out_specs=row(D_MODEL),
        out_shape=jax.ShapeDtypeStruct((t, D_MODEL), F32),
        compiler_params=_cparams(("parallel",)),
        name="merge",
    )(x2d, *att_outs, *att_lses, hf, hb, bd, y, bd, bd, bd, wa, wh, wf, wo, hgain, expand)


def _class_order(tab, r):
    s, w = tab.shape
    return tab.reshape(s // PERM, PERM // r, r, w).transpose(0, 2, 1, 3).reshape(s, w)


def _trunk_consts(seq):
    half = HEAD_DIM // 2
    inv_freq = ROPE_THETA ** (-jnp.arange(half, dtype=F32) / half)
    ang = jnp.arange(seq).astype(F32)[:, None] * inv_freq[None, :]
    cos, sin = jnp.cos(ang), jnp.sin(ang)
    cos128 = jnp.concatenate([cos, cos, cos, cos], axis=1)
    sin128 = jnp.concatenate([-sin, sin, -sin, sin], axis=1)
    rope = {r: (_class_order(cos128, r), _class_order(sin128, r)) for _, r in WINDOW_DILATIONS}
    hd = np.arange(256) // HEAD_DIM
    head_bd = jnp.asarray((hd[:, None] == hd[None, :]) / HEAD_DIM, BF16)
    perm, perm_t = {}, {}
    for _, r in WINDOW_DILATIONS:
        idx = np.arange(PERM)
        n = PERM // r
        p = np.zeros((PERM, PERM), np.float32)
        p[idx, (idx % n) * r + idx // n] = 1.0
        perm[r], perm_t[r] = jnp.asarray(p, BF16), jnp.asarray(p.T, BF16)
    expand = np.zeros((128, ATT_WIDTH), np.float32)
    for h in range(ATT_HEADS):
        expand[_lse_lane(h), h * HEAD_DIM:(h + 1) * HEAD_DIM] = 1.0
    return dict(rope=rope, head_bd=head_bd, perm=perm, perm_t=perm_t, expand=jnp.asarray(expand, BF16), n1=128)


def _layer_weights(l, lb, ffn1_norm, ffn1_w_gate, ffn1_w_up, ffn1_w_down, mix_norm, w_in, q_norm, k_norm,
                   w_att_out, hgrn_out_norm, w_hgrn_out, w_fnet_out, w_out, ffn2_norm, ffn2_w_gate, ffn2_w_up,
                   ffn2_w_down):
    bf = lambda w: w[l].astype(BF16)
    vec = lambda v: v[l].astype(F32).reshape(1, -1)
    lbl = lb[l]
    lbc = jnp.stack([jnp.log(lbl), jnp.log1p(-lbl), 1.0 - lbl, lbl], axis=1)
    heads_per_tile = ATT_WIDTH // HEAD_DIM
    w_in_bf = bf(w_in)
    return dict(
        ffn1=(vec(ffn1_norm), bf(ffn1_w_gate), bf(ffn1_w_up), bf(ffn1_w_down)),
        ffn2=(vec(ffn2_norm), bf(ffn2_w_gate), bf(ffn2_w_up), bf(ffn2_w_down)),
        mix_norm=vec(mix_norm), w_qkv=w_in_bf[:, :QKV_COLS], w_rest=w_in_bf[:, QKV_COLS:],
        gq=jnp.tile(q_norm[l].astype(F32), heads_per_tile).reshape(1, -1),
        gk=jnp.tile(k_norm[l].astype(F32), heads_per_tile).reshape(1, -1), lbc=lbc,
        wa=bf(w_att_out), wh=bf(w_hgrn_out), wf=bf(w_fnet_out), wo=bf(w_out), hgain=vec(hgrn_out_norm),
    )


def _encoder_layer(x, wts, consts):
    b, s, _ = x.shape
    t = b * s
    x2d = x.reshape(t, D_MODEL)
    x2d = _ffn(x2d, *wts["ffn1"])
    qkv = _proj_qkv(x2d, wts["mix_norm"], wts["w_qkv"], wts["gq"], wts["gk"], consts, b, s)
    z, bd = _proj_rest(x2d, wts["mix_norm"], wts["w_rest"])
    z3, bd3 = z.reshape(b, s, Z_COLS), bd.reshape(b, s, BD_COLS)
    outs, lses = [], []
    for gi, (w, r) in enumerate(WINDOW_DILATIONS):
        assert w // (2 * r) == BAND
        o, lse = _attention_group(qkv[3 * gi], qkv[3 * gi + 1], qkv[3 * gi + 2], consts["perm_t"][r], r)
        outs.append(o.reshape(t, ATT_WIDTH))
        lses.append(lse.reshape(t, 128))
    hf = _hgrn_scan(z3, bd3, wts["lbc"][0], False).reshape(t, HGRN_WIDTH)
    hb = _hgrn_scan(z3, bd3, wts["lbc"][1], True).reshape(t, HGRN_WIDTH)
    u = bd3[:, :, 3 * HGRN_WIDTH:3 * HGRN_WIDTH + FNET_WIDTH]
    y = _fourier_branch(u, consts["n1"], s // consts["n1"]).reshape(t, FNET_WIDTH)
    x2d = _merge(x2d, outs, lses, hf, hb, bd, y, wts["wa"], wts["wh"], wts["wf"], wts["wo"], wts["hgain"],
                 consts["expand"])
    x2d = _ffn(x2d, *wts["ffn2"])
    return x2d.reshape(b, s, D_MODEL)


def kernel(x_prompt, x_sample, hgrn_lb_logits, ffn1_norm, ffn1_w_gate, ffn1_w_up, ffn1_w_down, mix_norm, w_in,
           q_norm, k_norm, w_att_out, hgrn_out_norm, w_hgrn_out, w_fnet_out, w_out, ffn2_norm, ffn2_w_gate,
           ffn2_w_up, ffn2_w_down):
    depth = w_in.shape[0]
    p = jax.nn.softmax(hgrn_lb_logits.astype(F32), axis=0)
    cum = jnp.cumsum(p, axis=0)
    lb = cum - cum[0:1]
    layers = [_layer_weights(l, lb, ffn1_norm, ffn1_w_gate, ffn1_w_up, ffn1_w_down, mix_norm, w_in, q_norm,
                             k_norm, w_att_out, hgrn_out_norm, w_hgrn_out, w_fnet_out, w_out, ffn2_norm,
                             ffn2_w_gate, ffn2_w_up, ffn2_w_down) for l in range(depth)]

    def trunk(x):
        consts = _trunk_consts(x.shape[1])
        for wts in layers:
            x = _encoder_layer(x, wts, consts)
        return x

    return (trunk(x_prompt), trunk(x_sample))
```

```python
import functools

import numpy as np
import jax
import jax.numpy as jnp
from jax import lax
from jax.experimental import pallas as pl
from jax.experimental.pallas import tpu as pltpu

F32 = jnp.float32
BF16 = jnp.bfloat16

D_MODEL = 1024
D_FF = 2816
HEAD_DIM = 64
ATT_HEADS = 8
WINDOW_DILATIONS = ((128, 1), (512, 4), (2048, 16))
N_GROUPS = len(WINDOW_DILATIONS)
ATT_WIDTH = ATT_HEADS * HEAD_DIM
ATT_COLS = N_GROUPS * ATT_WIDTH
ROPE_THETA = 10000.0
HGRN_HEADS = 4
HGRN_DK = 128
HGRN_WIDTH = 512
HGRN_CHUNK = 64
HGRN_SUB = 8
FNET_WIDTH = 512
FNET_GROUP_DIM = 128
EPS = 1e-6
NEG_INF = -1e30
IN_COLS = 10752
BAND = 64
PERM = 256

QKV_COLS = 3 * ATT_COLS
Z_COLS = 2 * HGRN_WIDTH
GATE_COLS = 3 * D_MODEL
BD_COLS = GATE_COLS + 3 * HGRN_WIDTH
BD_Q, BD_I, BD_G = (GATE_COLS // HGRN_WIDTH + k for k in range(3))

V7X_VMEM_LIMIT_BYTES = 56 * 1024 * 1024
FF_CHUNKS = ((0, 1536), (1536, 2816))
TOKEN_TILE = 512
ATT_CLASS_ROWS = {1: 1024, 4: 256, 16: 128}
HGRN_TILE = 512
ATT_ROWS_PER_TILE = 64


def _cparams(sem):
    return pltpu.CompilerParams(dimension_semantics=sem, vmem_limit_bytes=V7X_VMEM_LIMIT_BYTES)


def _rms_rows(x, g):
    ms = jnp.mean(x * x, axis=-1, keepdims=True)
    return x * lax.rsqrt(ms + EPS) * g


def _ffn_body(x_ref, g_ref, wg_ref, wu_ref, wd_ref, o_ref):
    x = x_ref[...]
    h = _rms_rows(x, g_ref[...]).astype(BF16)
    acc = None
    for lo, hi in FF_CHUNKS:
        gate = jnp.dot(h, wg_ref[:, lo:hi], preferred_element_type=F32)
        up = jnp.dot(h, wu_ref[:, lo:hi], preferred_element_type=F32)
        a = (gate * jax.nn.sigmoid(gate) * up).astype(BF16)
        part = jnp.dot(a, wd_ref[lo:hi, :], preferred_element_type=F32)
        acc = part if acc is None else acc + part
    o_ref[...] = x + 0.5 * acc


def _ffn(x2d, g, wg, wu, wd):
    t, tm = x2d.shape[0], TOKEN_TILE
    const = lambda i: (0, 0)
    one = pl.Buffered(1)
    return pl.pallas_call(
        _ffn_body,
        grid=(t // tm,),
        in_specs=[
            pl.BlockSpec((tm, D_MODEL), lambda i: (i, 0)),
            pl.BlockSpec((1, D_MODEL), const),
            pl.BlockSpec((D_MODEL, D_FF), const, pipeline_mode=one),
            pl.BlockSpec((D_MODEL, D_FF), const, pipeline_mode=one),
            pl.BlockSpec((D_FF, D_MODEL), const, pipeline_mode=one),
        ],
        out_specs=pl.BlockSpec((tm, D_MODEL), lambda i: (i, 0)),
        out_shape=jax.ShapeDtypeStruct((t, D_MODEL), F32),
        compiler_params=_cparams(("parallel",)),
        name="ffn",
    )(x2d, g, wg, wu, wd)


def _head_norm_rope(acc, gain, cos, sin, bd_ref):
    tm = acc.shape[0]
    lane = lax.broadcasted_iota(jnp.int32, (tm, 128), 1)
    first_half = (lane % HEAD_DIM) < (HEAD_DIM // 2)
    slabs = []
    for s in range(ATT_WIDTH // 256):
        a = acc[:, s * 256:(s + 1) * 256]
        ms = jnp.dot((a * a).astype(BF16), bd_ref[...], preferred_element_type=F32)
        y = a * lax.rsqrt(ms + EPS) * gain[:, s * 256:(s + 1) * 256]
        for t in range(2):
            y1 = y[:, t * 128:(t + 1) * 128]
            yr = jnp.where(first_half, pltpu.roll(y1, 128 - HEAD_DIM // 2, 1), pltpu.roll(y1, HEAD_DIM // 2, 1))
            slabs.append(y1 * cos + yr * sin)
    return jnp.concatenate(slabs, axis=1)


def _proj_qkv_body(x_ref, g_ref, w_ref, gq_ref, gk_ref, c1_ref, s1_ref, c4_ref, s4_ref, c16_ref, s16_ref,
                   bd_ref, p4_ref, p16_ref, *out_refs):
    tm = x_ref.shape[0]
    h = _rms_rows(x_ref[...], g_ref[...]).astype(BF16)

    def permuted(p_ref):
        blocks = [jnp.dot(p_ref[...], h[b * PERM:(b + 1) * PERM], preferred_element_type=F32).astype(BF16)
                  for b in range(tm // PERM)]
        return jnp.concatenate(blocks, axis=0)

    lhs = (h, permuted(p4_ref), permuted(p16_ref))
    tables = ((c1_ref, s1_ref), (c4_ref, s4_ref), (c16_ref, s16_ref))
    gains = (gq_ref[...] * HEAD_DIM ** -0.5, gk_ref[...])
    for gi, (_, r) in enumerate(WINDOW_DILATIONS):
        cos, sin = tables[gi][0][...], tables[gi][1][...]
        n = PERM // r
        for ti in range(3):
            col = ti * ATT_COLS + gi * ATT_WIDTH
            y = jnp.dot(lhs[gi], w_ref[:, col:col + ATT_WIDTH], preferred_element_type=F32)
            if ti < 2:
                y = _head_norm_rope(y, gains[ti], cos, sin, bd_ref)
            y = y.astype(BF16)
            o_ref = out_refs[3 * gi + ti]
            if r == 1:
                o_ref[0] = y
            else:
                for b in range(tm // PERM):
                    for c in range(r):
                        o_ref[c, b * n:(b + 1) * n, :] = y[b * PERM + c * n:b * PERM + (c + 1) * n, :]


def _proj_rest_body(x_ref, g_ref, w_ref, z_ref, bd_ref, u_ref):
    h = _rms_rows(x_ref[...], g_ref[...]).astype(BF16)
    tn = 512
    proj = lambda c0: jnp.dot(h, w_ref[:, c0:c0 + tn], preferred_element_type=F32)
    for k in range(Z_COLS // tn):
        z_ref[:, k * tn:(k + 1) * tn] = proj(QKV_COLS + k * tn)
    hgrn0 = QKV_COLS + Z_COLS
    u0 = hgrn0 + 3 * HGRN_WIDTH
    gate0 = u0 + FNET_WIDTH
    for k in range(GATE_COLS // tn):
        bd_ref[:, k * tn:(k + 1) * tn] = proj(gate0 + k * tn).astype(BF16)
    for k in range(3 * HGRN_WIDTH // tn):
        bd_ref[:, GATE_COLS + k * tn:GATE_COLS + (k + 1) * tn] = proj(hgrn0 + k * tn).astype(BF16)
    u_ref[...] = proj(u0).astype(BF16)


def _proj_qkv(x2d, g, w_in, gq, gk, consts, batch, seq):
    t, tm = x2d.shape[0], TOKEN_TILE
    nsb = seq // tm
    const = lambda i: (0, 0)
    tab = pl.BlockSpec((tm, 128), lambda i: (i % nsb, 0))
    in_specs = [
        pl.BlockSpec((tm, D_MODEL), lambda i: (i, 0)),
        pl.BlockSpec((1, D_MODEL), const),
        pl.BlockSpec((D_MODEL, IN_COLS), const, pipeline_mode=pl.Buffered(1)),
        pl.BlockSpec((1, ATT_WIDTH), const), pl.BlockSpec((1, ATT_WIDTH), const),
        tab, tab, tab, tab, tab, tab,
        pl.BlockSpec((256, 256), const), pl.BlockSpec((PERM, PERM), const), pl.BlockSpec((PERM, PERM), const),
    ]
    out_specs, out_shape = [], []
    for _, r in WINDOW_DILATIONS:
        for _ in range(3):
            out_specs.append(pl.BlockSpec((None, r, tm // r, ATT_WIDTH), lambda i: (i // nsb, 0, i % nsb, 0)))
            out_shape.append(jax.ShapeDtypeStruct((batch, r, seq // r, ATT_WIDTH), BF16))
    rope = consts["rope"]
    return pl.pallas_call(
        _proj_qkv_body,
        grid=(t // tm,),
        in_specs=in_specs,
        out_specs=out_specs,
        out_shape=out_shape,
        compiler_params=_cparams(("parallel",)),
        name="proj_qkv",
    )(x2d, g, w_in, gq, gk, rope[1][0], rope[1][1], rope[4][0], rope[4][1], rope[16][0], rope[16][1],
      consts["head_bd"], consts["perm"][4], consts["perm"][16])


def _proj_rest(x2d, g, w_in):
    t, tm = x2d.shape[0], TOKEN_TILE
    const = lambda i: (0, 0)
    row = lambda w: pl.BlockSpec((tm, w), lambda i: (i, 0))
    return pl.pallas_call(
        _proj_rest_body,
        grid=(t // tm,),
        in_specs=[row(D_MODEL), pl.BlockSpec((1, D_MODEL), const),
                  pl.BlockSpec((D_MODEL, IN_COLS), const, pipeline_mode=pl.Buffered(1))],
        out_specs=[row(Z_COLS), row(BD_COLS), row(FNET_WIDTH)],
        out_shape=[jax.ShapeDtypeStruct((t, Z_COLS), F32), jax.ShapeDtypeStruct((t, BD_COLS), BF16),
                   jax.ShapeDtypeStruct((t, FNET_WIDTH), BF16)],
        compiler_params=_cparams(("parallel",)),
        name="proj_rest",
    )(x2d, g, w_in)


def _lse_lane(h):
    return HEAD_DIM * (h % 2) + 16 * (h // 2)


def _attn_body(q_ref, kp_ref, kc_ref, kn_ref, vp_ref, vc_ref, vn_ref, pt_ref, o_ref, lse_ref,
               ks, vs, os_, ls, *, r, tl, seq_l):
    blk = pl.program_id(1)
    sq, kw = 128, 128 + 2 * BAND
    ks[:, 0:BAND, :] = kp_ref[...]
    ks[:, BAND:BAND + tl, :] = kc_ref[...]
    ks[:, BAND + tl:2 * BAND + tl, :] = kn_ref[...]
    vs[:, 0:BAND, :] = vp_ref[...]
    vs[:, BAND:BAND + tl, :] = vc_ref[...]
    vs[:, BAND + tl:2 * BAND + tl, :] = vn_ref[...]

    nsub = tl // sq
    hq = ATT_ROWS_PER_TILE
    qi = lax.broadcasted_iota(jnp.int32, (sq, kw), 0)
    ki = lax.broadcasted_iota(jnp.int32, (sq, kw), 1)
    in_band = jnp.abs(ki - BAND - qi) <= BAND
    lane = lax.broadcasted_iota(jnp.int32, (hq, 128), 1)
    lane_lo = lane < HEAD_DIM
    lane_pair = (lane % HEAD_DIM) // 16
    lane_k = lax.broadcasted_iota(jnp.int32, (kw, 128), 1) < HEAD_DIM
    nt = (((1,), (1,)), ((), ()))

    def unit(u, carry):
        c = u // nsub
        q0 = pl.multiple_of((u % nsub) * sq, sq)
        kpos = blk * tl + q0 - BAND + ki
        valid = in_band & (kpos >= 0) & (kpos < seq_l)
        q2 = q_ref[c, pl.ds(q0, sq), :]
        k2 = ks[c, pl.ds(q0, kw), :]
        v2 = vs[c, pl.ds(q0, kw), :]
        for half in range(sq // hq):
            r0, r1 = half * hq, (half + 1) * hq
            vmask = jnp.concatenate([valid[r0:r1], valid[r0:r1]], axis=0)
            lse_tile = jnp.zeros((hq, 128), F32)
            for hp in range(ATT_HEADS // 2):
                sl = slice(hp * 128, (hp + 1) * 128)
                q128, k128, v128 = q2[r0:r1, sl], k2[:, sl], v2[:, sl]
                zero = jnp.zeros_like(q128)
                qstack = jnp.concatenate([jnp.where(lane_lo, q128, zero), jnp.where(lane_lo, zero, q128)], axis=0)
                s = lax.dot_general(qstack, k128, nt, preferred_element_type=F32)
                s = jnp.where(vmask, s, NEG_INF)
                m = jnp.max(s, axis=-1, keepdims=True)
                p = jnp.exp(s - m).astype(BF16)
                one = jnp.ones_like(v128)
                pv0 = jnp.dot(p[0:hq], jnp.where(lane_k, v128, one), preferred_element_type=F32)
                pv1 = jnp.dot(p[hq:2 * hq], jnp.where(lane_k, one, v128), preferred_element_type=F32)
                num = jnp.where(lane_lo, pv0, pv1)
                den = pltpu.roll(jnp.where(lane_lo, pv1, pv0), HEAD_DIM, 1)
                os_[c, pl.ds(q0 + r0, hq), sl] = (num / den).astype(BF16)
                lse_pair = jnp.where(lane_lo, m[0:hq], m[hq:2 * hq]) + jnp.log(den)
                lse_tile = jnp.where(lane_pair == hp, lse_pair, lse_tile)
            ls[c, pl.ds(q0 + r0, hq), :] = lse_tile
        return carry

    lax.fori_loop(0, r * nsub, unit, 0)

    if r == 1:
        o_ref[...] = os_[0]
        lse_ref[...] = ls[0]
    else:
        n = PERM // r
        for b in range(r * tl // PERM):
            cat = jnp.concatenate([os_[c, b * n:(b + 1) * n, :] for c in range(r)], axis=0)
            o_ref[b * PERM:(b + 1) * PERM, :] = jnp.dot(pt_ref[...], cat, preferred_element_type=F32).astype(BF16)
        for c in range(r):
            lse_ref[pl.ds(c, tl, stride=r), :] = ls[c]


def _attention_group(q, k, v, perm_t, r):
    b, _, l, _ = q.shape
    tl = min(ATT_CLASS_ROWS[r], l)
    ts = r * tl
    nh = tl // BAND
    last = l // BAND - 1
    cur = pl.BlockSpec((None, r, tl, ATT_WIDTH), lambda bi, i: (bi, 0, i, 0))
    prv = pl.BlockSpec((None, r, BAND, ATT_WIDTH), lambda bi, i: (bi, 0, jnp.maximum(i * nh - 1, 0), 0))
    nxt = pl.BlockSpec((None, r, BAND, ATT_WIDTH), lambda bi, i: (bi, 0, jnp.minimum((i + 1) * nh, last), 0))
    body = functools.partial(_attn_body, r=r, tl=tl, seq_l=l)
    return pl.pallas_call(
        body,
        grid=(b, l // tl),
        in_specs=[cur, prv, cur, nxt, prv, cur, nxt, pl.BlockSpec((PERM, PERM), lambda bi, i: (0, 0))],
        out_specs=[pl.BlockSpec((None, ts, ATT_WIDTH), lambda bi, i: (bi, i, 0)),
                   pl.BlockSpec((None, ts, 128), lambda bi, i: (bi, i, 0))],
        out_shape=[jax.ShapeDtypeStruct((b, r * l, ATT_WIDTH), BF16), jax.ShapeDtypeStruct((b, r * l, 128), F32)],
        scratch_shapes=[
            pltpu.VMEM((r, tl + 2 * BAND, ATT_WIDTH), BF16),
            pltpu.VMEM((r, tl + 2 * BAND, ATT_WIDTH), BF16),
            pltpu.VMEM((r, tl, ATT_WIDTH), BF16),
            pltpu.VMEM((r, tl, 128), F32),
        ],
        compiler_params=_cparams(("parallel", "parallel")),
        name=f"attn_r{r}",
    )(q, k, k, k, v, v, v, perm_t)


def _hgrn_body(z_ref, q_ref, i_ref, lb_ref, wsel_ref, o_ref, st_ref, *, rev, nch):
    @pl.when(pl.program_id(1) == 0)
    def _():
        st_ref[...] = jnp.zeros_like(st_ref)

    ch, sb = HGRN_CHUNK, HGRN_SUB
    row = lax.broadcasted_iota(jnp.int32, (ch, ch), 0)
    col = lax.broadcasted_iota(jnp.int32, (ch, ch), 1)
    dist = (col - row) if rev else (row - col)
    same = lambda n: (row // n) == (col // n)
    diag_mask = same(sb) & (dist >= 0)
    level_sizes = (8, 16, 32)
    level_masks = [same(2 * n) & jnp.logical_not(same(n)) & (dist > 0) for n in level_sizes]
    sub = lax.broadcasted_iota(jnp.int32, (ch, HGRN_DK), 0) % sb
    tau = (sb - 1 - sub) if rev else sub
    nt = (((1,), (1,)), ((), ()))
    tn = (((0,), (0,)), ((), ()))
    prev = lambda x, k: pltpu.roll(x, (ch - k) if rev else k, 0)
    nxt = lambda x, k: pltpu.roll(x, k if rev else (ch - k), 0)
    last_row = lambda start, n: start if rev else start + n - 1

    def widen(pf, sf, n):
        pf_rows, sf_rows = [], []
        for v in range(ch // 8):
            blk = (8 * v) // n
            first = blk % 2 == 0
            later = first if rev else not first
            sib = (blk + 1) * n if first else (blk - 1) * n
            r = last_row(sib, n)
            total = jnp.broadcast_to(pf[r:r + 1, :], (8, HGRN_DK))
            x, y = pf[8 * v:8 * v + 8], sf[8 * v:8 * v + 8]
            pf_rows.append(x * total if later else x)
            sf_rows.append(y if later else y * total)
        return jnp.concatenate(pf_rows, axis=0), jnp.concatenate(sf_rows, axis=0)

    def chunk(ci, carry):
        c = (nch - 1 - ci) if rev else ci
        r0 = pl.multiple_of(c * ch, ch)
        heads = range(HGRN_HEADS)
        sls = [slice(h * HGRN_DK, (h + 1) * HGRN_DK) for h in heads]
        qs, ivs, kfs, pfs, sfs, bands = [], [], [], [], [], []
        for sl in sls:
            z = z_ref[pl.ds(r0, ch), sl]
            q = q_ref[pl.ds(r0, ch), sl].astype(F32)
            lb = lb_ref[0:1, sl]
            one_m_lb = 1.0 - lb
            e = jnp.exp(-jnp.abs(z))
            inv = 1.0 / (1.0 + e)
            pos = z >= 0
            f = lb + one_m_lb * (jnp.where(pos, 1.0, e) * inv)
            kf = one_m_lb * (jnp.where(pos, e, 1.0) * inv)
            pf = f
            sf = jnp.where(tau == sb - 1, 1.0, nxt(f, 1))
            for k in (1, 2, 4):
                pf = pf * jnp.where(tau >= k, prev(pf, k), 1.0)
                sf = sf * jnp.where(tau <= sb - 1 - k, nxt(sf, k), 1.0)
            g = kf
            terms = [q * g]
            for d in range(1, sb):
                g = prev(g, 1) * f
                terms.append(q * g)
            qs.append(q); ivs.append(i_ref[pl.ds(r0, ch), sl]); kfs.append(kf); pfs.append(pf); sfs.append(sf)
            bands.append(jnp.concatenate(terms, axis=1).astype(BF16))

        bands = [jnp.dot(x, wsel_ref[...], preferred_element_type=F32) for x in bands]

        levels = []
        for n in level_sizes:
            levels.append([lax.dot_general((qs[h] * pfs[h]).astype(BF16), (kfs[h] * sfs[h]).astype(BF16), nt,
                                           preferred_element_type=F32) for h in heads])
            for h in heads:
                pfs[h], sfs[h] = widen(pfs[h], sfs[h], n)
        sts = [st_ref[h] for h in heads]
        inter = [lax.dot_general((qs[h] * pfs[h]).astype(BF16), sts[h].astype(BF16), nt, preferred_element_type=F32)
                 for h in heads]
        upd = [lax.dot_general(ivs[h], (kfs[h] * sfs[h]).astype(BF16), tn, preferred_element_type=F32)
               for h in heads]

        r_end = last_row(0, ch)
        for h in heads:
            skew = pltpu.roll(bands[h], 0, 1, stride=1, stride_axis=0)
            a = jnp.where(diag_mask, skew[:, 0:ch], 0.0)
            for lvl, mask in zip(levels, level_masks):
                a = jnp.where(mask, lvl[h], a)
            o = jnp.dot(a.astype(BF16), ivs[h], preferred_element_type=F32)
            o_ref[pl.ds(r0, ch), sls[h]] = o + inter[h]
            st_ref[h] = sts[h] * pfs[h][r_end:r_end + 1, :] + upd[h]
        return carry

    lax.fori_loop(0, nch, chunk, 0, unroll=2)


def _band_selector(rev):
    w = np.zeros((HGRN_SUB * HGRN_DK, 128), np.float32)
    for d in range(HGRN_SUB):
        w[d * HGRN_DK:(d + 1) * HGRN_DK, d if rev else (128 - d) % 128] = 1.0
    return jnp.asarray(w, BF16)


def _hgrn_scan(z, bd, lb, rev):
    b, s, _ = z.shape
    ts = HGRN_TILE
    nb = s // ts
    order = (lambda j: nb - 1 - j) if rev else (lambda j: j)
    spec = lambda col: pl.BlockSpec((None, ts, HGRN_WIDTH), lambda bi, j: (bi, order(j), col))
    const = lambda shape: pl.BlockSpec(shape, lambda bi, j: (0, 0))
    body = functools.partial(_hgrn_body, rev=rev, nch=ts // HGRN_CHUNK)
    return pl.pallas_call(
        body,
        grid=(b, nb),
        in_specs=[spec(1 if rev else 0), spec(BD_Q), spec(BD_I), const((1, HGRN_WIDTH)),
                  const((HGRN_SUB * HGRN_DK, 128))],
        out_specs=spec(0),
        out_shape=jax.ShapeDtypeStruct((b, s, HGRN_WIDTH), F32),
        scratch_shapes=[pltpu.VMEM((HGRN_HEADS, HGRN_DK, HGRN_DK), F32)],
        compiler_params=_cparams(("parallel", "arbitrary")),
        name="hgrn_bwd" if rev else "hgrn_fwd",
    )(z, bd, bd, lb, _band_selector(rev))


def _fnet_a_body(u_ref, cs_ref, m_ref, o_ref, *, n1, nt):
    ngrp = FNET_WIDTH // FNET_GROUP_DIM
    for s in range(nt):
        zr, zi = [], []
        for g in range(ngrp):
            c0 = s * FNET_WIDTH + g * FNET_GROUP_DIM
            zz = jnp.dot(u_ref[:, c0:c0 + FNET_GROUP_DIM], cs_ref[...], preferred_element_type=F32)
            zr.append(zz[:, :FNET_GROUP_DIM])
            zi.append(zz[:, FNET_GROUP_DIM:])
        zcat = jnp.concatenate([jnp.concatenate(zr, axis=1), jnp.concatenate(zi, axis=1)], axis=0).astype(BF16)
        res = jnp.dot(m_ref[s], zcat, preferred_element_type=F32)
        o_ref[:, s * FNET_WIDTH:(s + 1) * FNET_WIDTH] = res.astype(o_ref.dtype)


def _fnet_c_body(b_ref, cs_ref, o_ref, *, kt):
    for k in range(kt):
        bcat = jnp.concatenate([b_ref[0, k], b_ref[1, k]], axis=0)
        x = jnp.dot(cs_ref[...], bcat, preferred_element_type=F32)
        o_ref[:, k * FNET_WIDTH:(k + 1) * FNET_WIDTH] = x.astype(o_ref.dtype)


def _fnet_tables(n1, n2):
    n = n1 * n2
    k = np.arange(FNET_GROUP_DIM)
    ang = 2.0 * np.pi * ((k[:, None] * k[None, :]) % FNET_GROUP_DIM) / FNET_GROUP_DIM
    cs_ch = np.concatenate([np.cos(ang), -np.sin(ang)], axis=1) / np.sqrt(FNET_GROUP_DIM)
    k1 = np.arange(n1)[None, :, None]
    m1 = np.arange(n1)[None, None, :]
    s2 = np.arange(n2)[:, None, None]
    idx = (k1 * m1 * n2 + s2 * k1) % n
    th = 2.0 * np.pi * idx / n
    gc, gs = np.cos(th) / np.sqrt(n1), np.sin(th) / np.sqrt(n1)
    m = np.concatenate([np.concatenate([gc, gs], axis=2), np.concatenate([-gs, gc], axis=2)], axis=1)
    k2 = np.arange(n2)
    ang2 = 2.0 * np.pi * ((k2[:, None] * k2[None, :]) % n2) / n2
    cs_seq = np.concatenate([np.cos(ang2), np.sin(ang2)], axis=1) / np.sqrt(n2)
    return (jnp.asarray(cs_ch, BF16), jnp.asarray(m, BF16), jnp.asarray(cs_seq, BF16))


def _fourier_branch(u, n1, n2):
    b, s, _ = u.shape
    cs_ch, m, cs_seq = _fnet_tables(n1, n2)
    nt = 4
    kt = 8
    u2 = u.reshape(b, n1, n2 * FNET_WIDTH)
    a = pl.pallas_call(
        functools.partial(_fnet_a_body, n1=n1, nt=nt),
        grid=(b, n2 // nt),
        in_specs=[
            pl.BlockSpec((None, n1, nt * FNET_WIDTH), lambda bi, j: (bi, 0, j)),
            pl.BlockSpec((FNET_GROUP_DIM, 2 * FNET_GROUP_DIM), lambda bi, j: (0, 0)),
            pl.BlockSpec((nt, 2 * n1, 2 * n1), lambda bi, j: (j, 0, 0)),
        ],
        out_specs=pl.BlockSpec((None, 2 * n1, nt * FNET_WIDTH), lambda bi, j: (bi, 0, j)),
        out_shape=jax.ShapeDtypeStruct((b, 2 * n1, n2 * FNET_WIDTH), BF16),
        compiler_params=_cparams(("parallel", "parallel")),
        name="fnet_a",
    )(u2, cs_ch, m)
    a5 = a.reshape(b, 2, n1, n2, FNET_WIDTH)
    y = pl.pallas_call(
        functools.partial(_fnet_c_body, kt=kt),
        grid=(b, n1 // kt),
        in_specs=[
            pl.BlockSpec((None, 2, kt, n2, FNET_WIDTH), lambda bi, j: (bi, 0, j, 0, 0)),
            pl.BlockSpec((n2, 2 * n2), lambda bi, j: (0, 0)),
        ],
        out_specs=pl.BlockSpec((None, n2, kt * FNET_WIDTH), lambda bi, j: (bi, 0, j)),
        out_shape=jax.ShapeDtypeStruct((b, n2, n1 * FNET_WIDTH), BF16),
        compiler_params=_cparams(("parallel", "parallel")),
        name="fnet_c",
    )(a5, cs_seq)
    return y.reshape(b, s, FNET_WIDTH)


def _merge_body(x_ref, o0_ref, o1_ref, o2_ref, l0_ref, l1_ref, l2_ref, hf_ref, hb_ref, gh_ref, y_ref,
                ga_ref, gb_ref, gc_ref, wa_ref, wh_ref, wf_ref, wo_ref, hg_ref, ex_ref, out_ref):
    lses = [l0_ref[...], l1_ref[...], l2_ref[...]]
    m = jnp.maximum(jnp.maximum(lses[0], lses[1]), lses[2])
    ws = [jnp.exp(l - m) for l in lses]
    inv = 1.0 / (ws[0] + ws[1] + ws[2])
    att = None
    for w, o_ref in zip(ws, (o0_ref, o1_ref, o2_ref)):
        alpha = jnp.dot((w * inv).astype(BF16), ex_ref[...], preferred_element_type=F32)
        term = alpha * o_ref[...].astype(F32)
        att = term if att is None else att + term
    a = jnp.dot(att.astype(BF16), wa_ref[...], preferred_element_type=F32)
    oh = hf_ref[...] + hb_ref[...]
    gh = gh_ref[...].astype(F32)
    heads = []
    for h in range(HGRN_HEADS):
        sl = slice(h * HGRN_DK, (h + 1) * HGRN_DK)
        gate = gh[:, sl]
        heads.append(_rms_rows(oh[:, sl], hg_ref[...]) * (gate * jax.nn.sigmoid(gate)))
    bh = jnp.dot(jnp.concatenate(heads, axis=1).astype(BF16), wh_ref[...], preferred_element_type=F32)
    c = jnp.dot(y_ref[...].astype(BF16), wf_ref[...], preferred_element_type=F32)
    sig = lambda ref: jax.nn.sigmoid(ref[...].astype(F32))
    merged = sig(ga_ref) * a + sig(gb_ref) * bh + sig(gc_ref) * c
    out_ref[...] = x_ref[...] + jnp.dot(merged.astype(BF16), wo_ref[...], preferred_element_type=F32)


def _merge(x2d, att_outs, att_lses, hf, hb, bd, y, wa, wh, wf, wo, hgain, expand):
    t, tm = x2d.shape[0], TOKEN_TILE
    row = lambda w: pl.BlockSpec((tm, w), lambda i: (i, 0))
    const = lambda shape: pl.BlockSpec(shape, lambda i: (0, 0))
    gate = lambda k: pl.BlockSpec((tm, D_MODEL), lambda i: (i, k))
    in_specs = ([row(D_MODEL)] + [row(ATT_WIDTH)] * 3 + [row(128)] * 3 + [row(HGRN_WIDTH)] * 2
                + [pl.BlockSpec((tm, HGRN_WIDTH), lambda i: (i, BD_G)),
                   row(FNET_WIDTH), gate(0), gate(1), gate(2),
                   const((ATT_WIDTH, D_MODEL)), const((HGRN_WIDTH, D_MODEL)), const((FNET_WIDTH, D_MODEL)),
                   const((D_MODEL, D_MODEL)), const((1, HGRN_DK)), const((128, ATT_WIDTH))])
    return pl.pallas_call(
        _merge_body,
        grid=(t // tm,),
        in_specs=in_specs,
        out_specs=row(D_MODEL),
        out_shape=jax.ShapeDtypeStruct((t, D_MODEL), F32),
        compiler_params=_cparams(("parallel",)),
        name="merge",
    )(x2d, *att_outs, *att_lses, hf, hb, bd, y, bd, bd, bd, wa, wh, wf, wo, hgain, expand)


def _class_order(tab, r):
    s, w = tab.shape
    return tab.reshape(s // PERM, PERM // r, r, w).transpose(0, 2, 1, 3).reshape(s, w)


def _trunk_consts(seq):
    half = HEAD_DIM // 2
    inv_freq = ROPE_THETA ** (-jnp.arange(half, dtype=F32) / half)
    ang = jnp.arange(seq).astype(F32)[:, None] * inv_freq[None, :]
    cos, sin = jnp.cos(ang), jnp.sin(ang)
    cos128 = jnp.concatenate([cos, cos, cos, cos], axis=1)
    sin128 = jnp.concatenate([-sin, sin, -sin, sin], axis=1)
    rope = {r: (_class_order(cos128, r), _class_order(sin128, r)) for _, r in WINDOW_DILATIONS}
    hd = np.arange(256) // HEAD_DIM
    head_bd = jnp.asarray((hd[:, None] == hd[None, :]) / HEAD_DIM, BF16)
    perm, perm_t = {}, {}
    for _, r in WINDOW_DILATIONS:
        idx = np.arange(PERM)
        n = PERM // r
        p = np.zeros((PERM, PERM), np.float32)
        p[idx, (idx % n) * r + idx // n] = 1.0
        perm[r], perm_t[r] = jnp.asarray(p, BF16), jnp.asarray(p.T, BF16)
    expand = np.zeros((128, ATT_WIDTH), np.float32)
    for h in range(ATT_HEADS):
        expand[_lse_lane(h), h * HEAD_DIM:(h + 1) * HEAD_DIM] = 1.0
    return dict(rope=rope, head_bd=head_bd, perm=perm, perm_t=perm_t, expand=jnp.asarray(expand, BF16), n1=128)


def _layer_weights(l, lb, ffn1_norm, ffn1_w_gate, ffn1_w_up, ffn1_w_down, mix_norm, w_in, q_norm, k_norm,
                   w_att_out, hgrn_out_norm, w_hgrn_out, w_fnet_out, w_out, ffn2_norm, ffn2_w_gate, ffn2_w_up,
                   ffn2_w_down):
    bf = lambda w: w[l].astype(BF16)
    vec = lambda v: v[l].astype(F32).reshape(1, -1)
    heads_per_tile = ATT_WIDTH // HEAD_DIM
    return dict(
        ffn1=(vec(ffn1_norm), bf(ffn1_w_gate), bf(ffn1_w_up), bf(ffn1_w_down)),
        ffn2=(vec(ffn2_norm), bf(ffn2_w_gate), bf(ffn2_w_up), bf(ffn2_w_down)),
        mix_norm=vec(mix_norm), w_in=bf(w_in),
        gq=jnp.tile(q_norm[l].astype(F32), heads_per_tile).reshape(1, -1),
        gk=jnp.tile(k_norm[l].astype(F32), heads_per_tile).reshape(1, -1),
        lb_fwd=lb[l, 0].reshape(1, -1), lb_bwd=lb[l, 1].reshape(1, -1),
        wa=bf(w_att_out), wh=bf(w_hgrn_out), wf=bf(w_fnet_out), wo=bf(w_out), hgain=vec(hgrn_out_norm),
    )


def _encoder_layer(x, wts, consts):
    b, s, _ = x.shape
    t = b * s
    x2d = x.reshape(t, D_MODEL)
    x2d = _ffn(x2d, *wts["ffn1"])
    qkv = _proj_qkv(x2d, wts["mix_norm"], wts["w_in"], wts["gq"], wts["gk"], consts, b, s)
    z, bd, u = _proj_rest(x2d, wts["mix_norm"], wts["w_in"])
    z3, bd3 = z.reshape(b, s, Z_COLS), bd.reshape(b, s, BD_COLS)
    outs, lses = [], []
    for gi, (w, r) in enumerate(WINDOW_DILATIONS):
        assert w // (2 * r) == BAND
        o, lse = _attention_group(qkv[3 * gi], qkv[3 * gi + 1], qkv[3 * gi + 2], consts["perm_t"][r], r)
        outs.append(o.reshape(t, ATT_WIDTH))
        lses.append(lse.reshape(t, 128))
    hf = _hgrn_scan(z3, bd3, wts["lb_fwd"], False).reshape(t, HGRN_WIDTH)
    hb = _hgrn_scan(z3, bd3, wts["lb_bwd"], True).reshape(t, HGRN_WIDTH)
    y = _fourier_branch(u.reshape(b, s, FNET_WIDTH), consts["n1"], s // consts["n1"]).reshape(t, FNET_WIDTH)
    x2d = _merge(x2d, outs, lses, hf, hb, bd, y, wts["wa"], wts["wh"], wts["wf"], wts["wo"], wts["hgain"],
                 consts["expand"])
    x2d = _ffn(x2d, *wts["ffn2"])
    return x2d.reshape(b, s, D_MODEL)


def kernel(x_prompt, x_sample, hgrn_lb_logits, ffn1_norm, ffn1_w_gate, ffn1_w_up, ffn1_w_down, mix_norm, w_in,
           q_norm, k_norm, w_att_out, hgrn_out_norm, w_hgrn_out, w_fnet_out, w_out, ffn2_norm, ffn2_w_gate,
           ffn2_w_up, ffn2_w_down):
    depth = w_in.shape[0]
    p = jax.nn.softmax(hgrn_lb_logits.astype(F32), axis=0)
    cum = jnp.cumsum(p, axis=0)
    lb = cum - cum[0:1]
    layers = [_layer_weights(l, lb, ffn1_norm, ffn1_w_gate, ffn1_w_up, ffn1_w_down, mix_norm, w_in, q_norm,
                             k_norm, w_att_out, hgrn_out_norm, w_hgrn_out, w_fnet_out, w_out, ffn2_norm,
                             ffn2_w_gate, ffn2_w_up, ffn2_w_down) for l in range(depth)]

    def trunk(x):
        consts = _trunk_consts(x.shape[1])
        for wts in layers:
            x = _encoder_layer(x, wts, consts)
        return x

    return (trunk(x_prompt), trunk(x_sample))
```

```python
import functools

import numpy as np
import jax
import jax.numpy as jnp
from jax import lax
from jax.experimental import pallas as pl
from jax.experimental.pallas import tpu as pltpu

F32 = jnp.float32
BF16 = jnp.bfloat16

D_MODEL = 1024
D_FF = 2816
HEAD_DIM = 64
ATT_HEADS = 8
WINDOW_DILATIONS = ((128, 1), (512, 4), (2048, 16))
N_GROUPS = len(WINDOW_DILATIONS)
ATT_WIDTH = ATT_HEADS * HEAD_DIM
ATT_COLS = N_GROUPS * ATT_WIDTH
ROPE_THETA = 10000.0
HGRN_HEADS = 4
HGRN_DK = 128
HGRN_WIDTH = 512
HGRN_CHUNK = 64
HGRN_SUB = 8
FNET_WIDTH = 512
FNET_GROUP_DIM = 128
EPS = 1e-6
NEG_INF = -1e30
IN_COLS = 10752
BAND = 64
PERM = 256

QKV_COLS = 3 * ATT_COLS
Z_COLS = 2 * HGRN_WIDTH
GATE_COLS = 3 * D_MODEL
BD_COLS = GATE_COLS + 3 * HGRN_WIDTH
BD_Q, BD_I, BD_G = (GATE_COLS // HGRN_WIDTH + k for k in range(3))

V7X_VMEM_LIMIT_BYTES = 56 * 1024 * 1024
FF_CHUNKS = ((0, 1536), (1536, 2816))
TOKEN_TILE = 512
ATT_CLASS_ROWS = {1: 1024, 4: 256, 16: 128}
HGRN_TILE = 512
ATT_ROWS_PER_TILE = 64


def _cparams(sem):
    return pltpu.CompilerParams(dimension_semantics=sem, vmem_limit_bytes=V7X_VMEM_LIMIT_BYTES)


def _rms_rows(x, g):
    ms = jnp.mean(x * x, axis=-1, keepdims=True)
    return x * lax.rsqrt(ms + EPS) * g


def _ffn_body(x_ref, g_ref, wg_ref, wu_ref, wd_ref, o_ref):
    x = x_ref[...]
    h = _rms_rows(x, g_ref[...]).astype(BF16)
    acc = None
    for lo, hi in FF_CHUNKS:
        gate = jnp.dot(h, wg_ref[:, lo:hi], preferred_element_type=F32)
        up = jnp.dot(h, wu_ref[:, lo:hi], preferred_element_type=F32)
        a = (gate * jax.nn.sigmoid(gate) * up).astype(BF16)
        part = jnp.dot(a, wd_ref[lo:hi, :], preferred_element_type=F32)
        acc = part if acc is None else acc + part
    o_ref[...] = x + 0.5 * acc


def _of_layer(l, rows, cols, **kw):
    return pl.BlockSpec((None, rows, cols), lambda *_: (l, 0, 0), **kw)


def _ffn(x2d, g, wg, wu, wd, l):
    t, tm = x2d.shape[0], TOKEN_TILE
    one = pl.Buffered(1)
    return pl.pallas_call(
        _ffn_body,
        grid=(t // tm,),
        in_specs=[
            pl.BlockSpec((tm, D_MODEL), lambda i: (i, 0)),
            _of_layer(l, 1, D_MODEL),
            _of_layer(l, D_MODEL, D_FF, pipeline_mode=one),
            _of_layer(l, D_MODEL, D_FF, pipeline_mode=one),
            _of_layer(l, D_FF, D_MODEL, pipeline_mode=one),
        ],
        out_specs=pl.BlockSpec((tm, D_MODEL), lambda i: (i, 0)),
        out_shape=jax.ShapeDtypeStruct((t, D_MODEL), F32),
        compiler_params=_cparams(("parallel",)),
        name="ffn",
    )(x2d, g, wg, wu, wd)


def _head_norm_rope(acc, gain, cos, sin, bd_ref):
    tm = acc.shape[0]
    lane = lax.broadcasted_iota(jnp.int32, (tm, 128), 1)
    first_half = (lane % HEAD_DIM) < (HEAD_DIM // 2)
    slabs = []
    for s in range(ATT_WIDTH // 256):
        a = acc[:, s * 256:(s + 1) * 256]
        ms = jnp.dot((a * a).astype(BF16), bd_ref[...], preferred_element_type=F32)
        y = a * lax.rsqrt(ms + EPS) * gain[:, s * 256:(s + 1) * 256]
        for t in range(2):
            y1 = y[:, t * 128:(t + 1) * 128]
            yr = jnp.where(first_half, pltpu.roll(y1, 128 - HEAD_DIM // 2, 1), pltpu.roll(y1, HEAD_DIM // 2, 1))
            slabs.append(y1 * cos + yr * sin)
    return jnp.concatenate(slabs, axis=1)


def _proj_qkv_body(x_ref, g_ref, w_ref, gq_ref, gk_ref, c1_ref, s1_ref, c4_ref, s4_ref, c16_ref, s16_ref,
                   bd_ref, p4_ref, p16_ref, *out_refs):
    tm = x_ref.shape[0]
    h = _rms_rows(x_ref[...], g_ref[...]).astype(BF16)

    def permuted(p_ref):
        blocks = [jnp.dot(p_ref[...], h[b * PERM:(b + 1) * PERM], preferred_element_type=F32).astype(BF16)
                  for b in range(tm // PERM)]
        return jnp.concatenate(blocks, axis=0)

    lhs = (h, permuted(p4_ref), permuted(p16_ref))
    tables = ((c1_ref, s1_ref), (c4_ref, s4_ref), (c16_ref, s16_ref))
    gains = (gq_ref[...] * HEAD_DIM ** -0.5, gk_ref[...])
    for gi, (_, r) in enumerate(WINDOW_DILATIONS):
        cos, sin = tables[gi][0][...], tables[gi][1][...]
        n = PERM // r
        for ti in range(3):
            col = ti * ATT_COLS + gi * ATT_WIDTH
            y = jnp.dot(lhs[gi], w_ref[:, col:col + ATT_WIDTH], preferred_element_type=F32)
            if ti < 2:
                y = _head_norm_rope(y, gains[ti], cos, sin, bd_ref)
            y = y.astype(BF16)
            o_ref = out_refs[3 * gi + ti]
            if r == 1:
                o_ref[0] = y
            else:
                for b in range(tm // PERM):
                    for c in range(r):
                        o_ref[c, b * n:(b + 1) * n, :] = y[b * PERM + c * n:b * PERM + (c + 1) * n, :]


def _proj_rest_body(x_ref, g_ref, w_ref, z_ref, bd_ref, u_ref):
    h = _rms_rows(x_ref[...], g_ref[...]).astype(BF16)
    tn = 512
    proj = lambda c0: jnp.dot(h, w_ref[:, c0:c0 + tn], preferred_element_type=F32)
    for k in range(Z_COLS // tn):
        z_ref[:, k * tn:(k + 1) * tn] = proj(QKV_COLS + k * tn)
    hgrn0 = QKV_COLS + Z_COLS
    u0 = hgrn0 + 3 * HGRN_WIDTH
    gate0 = u0 + FNET_WIDTH
    for k in range(GATE_COLS // tn):
        bd_ref[:, k * tn:(k + 1) * tn] = proj(gate0 + k * tn).astype(BF16)
    for k in range(3 * HGRN_WIDTH // tn):
        bd_ref[:, GATE_COLS + k * tn:GATE_COLS + (k + 1) * tn] = proj(hgrn0 + k * tn).astype(BF16)
    u_ref[...] = proj(u0).astype(BF16)


def _proj_qkv(x2d, g, w_in, gq, gk, consts, batch, seq, l):
    t, tm = x2d.shape[0], TOKEN_TILE
    nsb = seq // tm
    const = lambda i: (0, 0)
    tab = pl.BlockSpec((tm, 128), lambda i: (i % nsb, 0))
    in_specs = [
        pl.BlockSpec((tm, D_MODEL), lambda i: (i, 0)),
        _of_layer(l, 1, D_MODEL),
        _of_layer(l, D_MODEL, IN_COLS, pipeline_mode=pl.Buffered(1)),
        _of_layer(l, 1, ATT_WIDTH), _of_layer(l, 1, ATT_WIDTH),
        tab, tab, tab, tab, tab, tab,
        pl.BlockSpec((256, 256), const), pl.BlockSpec((PERM, PERM), const), pl.BlockSpec((PERM, PERM), const),
    ]
    out_specs, out_shape = [], []
    for _, r in WINDOW_DILATIONS:
        for _ in range(3):
            out_specs.append(pl.BlockSpec((None, r, tm // r, ATT_WIDTH), lambda i: (i // nsb, 0, i % nsb, 0)))
            out_shape.append(jax.ShapeDtypeStruct((batch, r, seq // r, ATT_WIDTH), BF16))
    rope = consts["rope"]
    return pl.pallas_call(
        _proj_qkv_body,
        grid=(t // tm,),
        in_specs=in_specs,
        out_specs=out_specs,
        out_shape=out_shape,
        compiler_params=_cparams(("parallel",)),
        name="proj_qkv",
    )(x2d, g, w_in, gq, gk, rope[1][0], rope[1][1], rope[4][0], rope[4][1], rope[16][0], rope[16][1],
      consts["head_bd"], consts["perm"][4], consts["perm"][16])


def _proj_rest(x2d, g, w_in, l):
    t, tm = x2d.shape[0], TOKEN_TILE
    row = lambda w: pl.BlockSpec((tm, w), lambda i: (i, 0))
    return pl.pallas_call(
        _proj_rest_body,
        grid=(t // tm,),
        in_specs=[row(D_MODEL), _of_layer(l, 1, D_MODEL),
                  _of_layer(l, D_MODEL, IN_COLS, pipeline_mode=pl.Buffered(1))],
        out_specs=[row(Z_COLS), row(BD_COLS), row(FNET_WIDTH)],
        out_shape=[jax.ShapeDtypeStruct((t, Z_COLS), F32), jax.ShapeDtypeStruct((t, BD_COLS), BF16),
                   jax.ShapeDtypeStruct((t, FNET_WIDTH), BF16)],
        compiler_params=_cparams(("parallel",)),
        name="proj_rest",
    )(x2d, g, w_in)


def _lse_lane(h):
    return HEAD_DIM * (h % 2) + 16 * (h // 2)


def _attn_body(q_ref, kp_ref, kc_ref, kn_ref, vp_ref, vc_ref, vn_ref, pt_ref, o_ref, lse_ref,
               ks, vs, os_, ls, *, r, tl, seq_l):
    blk = pl.program_id(1)
    sq, kw = 128, 128 + 2 * BAND
    ks[:, 0:BAND, :] = kp_ref[...]
    ks[:, BAND:BAND + tl, :] = kc_ref[...]
    ks[:, BAND + tl:2 * BAND + tl, :] = kn_ref[...]
    vs[:, 0:BAND, :] = vp_ref[...]
    vs[:, BAND:BAND + tl, :] = vc_ref[...]
    vs[:, BAND + tl:2 * BAND + tl, :] = vn_ref[...]

    nsub = tl // sq
    hq = ATT_ROWS_PER_TILE
    qi = lax.broadcasted_iota(jnp.int32, (sq, kw), 0)
    ki = lax.broadcasted_iota(jnp.int32, (sq, kw), 1)
    in_band = jnp.abs(ki - BAND - qi) <= BAND
    lane = lax.broadcasted_iota(jnp.int32, (hq, 128), 1)
    lane_lo = lane < HEAD_DIM
    lane_pair = (lane % HEAD_DIM) // 16
    lane_k = lax.broadcasted_iota(jnp.int32, (kw, 128), 1) < HEAD_DIM
    nt = (((1,), (1,)), ((), ()))

    def unit(u, carry):
        c = u // nsub
        q0 = pl.multiple_of((u % nsub) * sq, sq)
        kpos = blk * tl + q0 - BAND + ki
        valid = in_band & (kpos >= 0) & (kpos < seq_l)
        q2 = q_ref[c, pl.ds(q0, sq), :]
        k2 = ks[c, pl.ds(q0, kw), :]
        v2 = vs[c, pl.ds(q0, kw), :]
        for half in range(sq // hq):
            r0, r1 = half * hq, (half + 1) * hq
            vmask = jnp.concatenate([valid[r0:r1], valid[r0:r1]], axis=0)
            lse_tile = jnp.zeros((hq, 128), F32)
            for hp in range(ATT_HEADS // 2):
                sl = slice(hp * 128, (hp + 1) * 128)
                q128, k128, v128 = q2[r0:r1, sl], k2[:, sl], v2[:, sl]
                zero = jnp.zeros_like(q128)
                qstack = jnp.concatenate([jnp.where(lane_lo, q128, zero), jnp.where(lane_lo, zero, q128)], axis=0)
                s = lax.dot_general(qstack, k128, nt, preferred_element_type=F32)
                s = jnp.where(vmask, s, NEG_INF)
                m = jnp.max(s, axis=-1, keepdims=True)
                p = jnp.exp(s - m).astype(BF16)
                one = jnp.ones_like(v128)
                pv0 = jnp.dot(p[0:hq], jnp.where(lane_k, v128, one), preferred_element_type=F32)
                pv1 = jnp.dot(p[hq:2 * hq], jnp.where(lane_k, one, v128), preferred_element_type=F32)
                num = jnp.where(lane_lo, pv0, pv1)
                den = pltpu.roll(jnp.where(lane_lo, pv1, pv0), HEAD_DIM, 1)
                os_[c, pl.ds(q0 + r0, hq), sl] = (num / den).astype(BF16)
                lse_pair = jnp.where(lane_lo, m[0:hq], m[hq:2 * hq]) + jnp.log(den)
                lse_tile = jnp.where(lane_pair == hp, lse_pair, lse_tile)
            ls[c, pl.ds(q0 + r0, hq), :] = lse_tile
        return carry

    lax.fori_loop(0, r * nsub, unit, 0)

    if r == 1:
        o_ref[...] = os_[0]
        lse_ref[...] = ls[0]
    else:
        n = PERM // r
        for b in range(r * tl // PERM):
            cat = jnp.concatenate([os_[c, b * n:(b + 1) * n, :] for c in range(r)], axis=0)
            o_ref[b * PERM:(b + 1) * PERM, :] = jnp.dot(pt_ref[...], cat, preferred_element_type=F32).astype(BF16)
        for c in range(r):
            lse_ref[pl.ds(c, tl, stride=r), :] = ls[c]


def _attention_group(q, k, v, perm_t, r):
    b, _, l, _ = q.shape
    tl = min(ATT_CLASS_ROWS[r], l)
    ts = r * tl
    nh = tl // BAND
    last = l // BAND - 1
    cur = pl.BlockSpec((None, r, tl, ATT_WIDTH), lambda bi, i: (bi, 0, i, 0))
    prv = pl.BlockSpec((None, r, BAND, ATT_WIDTH), lambda bi, i: (bi, 0, jnp.maximum(i * nh - 1, 0), 0))
    nxt = pl.BlockSpec((None, r, BAND, ATT_WIDTH), lambda bi, i: (bi, 0, jnp.minimum((i + 1) * nh, last), 0))
    body = functools.partial(_attn_body, r=r, tl=tl, seq_l=l)
    return pl.pallas_call(
        body,
        grid=(b, l // tl),
        in_specs=[cur, prv, cur, nxt, prv, cur, nxt, pl.BlockSpec((PERM, PERM), lambda bi, i: (0, 0))],
        out_specs=[pl.BlockSpec((None, ts, ATT_WIDTH), lambda bi, i: (bi, i, 0)),
                   pl.BlockSpec((None, ts, 128), lambda bi, i: (bi, i, 0))],
        out_shape=[jax.ShapeDtypeStruct((b, r * l, ATT_WIDTH), BF16), jax.ShapeDtypeStruct((b, r * l, 128), F32)],
        scratch_shapes=[
            pltpu.VMEM((r, tl + 2 * BAND, ATT_WIDTH), BF16),
            pltpu.VMEM((r, tl + 2 * BAND, ATT_WIDTH), BF16),
            pltpu.VMEM((r, tl, ATT_WIDTH), BF16),
            pltpu.VMEM((r, tl, 128), F32),
        ],
        compiler_params=_cparams(("parallel", "parallel")),
        name=f"attn_r{r}",
    )(q, k, k, k, v, v, v, perm_t)


def _hgrn_body(z_ref, q_ref, i_ref, lb_ref, wsel_ref, o_ref, st_ref, *, rev, nch):
    @pl.when(pl.program_id(1) == 0)
    def _():
        st_ref[...] = jnp.zeros_like(st_ref)

    ch, sb = HGRN_CHUNK, HGRN_SUB
    row = lax.broadcasted_iota(jnp.int32, (ch, ch), 0)
    col = lax.broadcasted_iota(jnp.int32, (ch, ch), 1)
    dist = (col - row) if rev else (row - col)
    same = lambda n: (row // n) == (col // n)
    diag_mask = same(sb) & (dist >= 0)
    level_sizes = (8, 16, 32)
    level_masks = [same(2 * n) & jnp.logical_not(same(n)) & (dist > 0) for n in level_sizes]
    sub = lax.broadcasted_iota(jnp.int32, (ch, HGRN_DK), 0) % sb
    tau = (sb - 1 - sub) if rev else sub
    nt = (((1,), (1,)), ((), ()))
    tn = (((0,), (0,)), ((), ()))
    prev = lambda x, k: pltpu.roll(x, (ch - k) if rev else k, 0)
    nxt = lambda x, k: pltpu.roll(x, k if rev else (ch - k), 0)
    last_row = lambda start, n: start if rev else start + n - 1

    def widen(pf, sf, n):
        pf_rows, sf_rows = [], []
        for v in range(ch // 8):
            blk = (8 * v) // n
            first = blk % 2 == 0
            later = first if rev else not first
            sib = (blk + 1) * n if first else (blk - 1) * n
            r = last_row(sib, n)
            total = jnp.broadcast_to(pf[r:r + 1, :], (8, HGRN_DK))
            x, y = pf[8 * v:8 * v + 8], sf[8 * v:8 * v + 8]
            pf_rows.append(x * total if later else x)
            sf_rows.append(y if later else y * total)
        return jnp.concatenate(pf_rows, axis=0), jnp.concatenate(sf_rows, axis=0)

    def chunk(ci, carry):
        c = (nch - 1 - ci) if rev else ci
        r0 = pl.multiple_of(c * ch, ch)
        heads = range(HGRN_HEADS)
        sls = [slice(h * HGRN_DK, (h + 1) * HGRN_DK) for h in heads]
        qs, ivs, kfs, pfs, sfs, bands = [], [], [], [], [], []
        for sl in sls:
            z = z_ref[pl.ds(r0, ch), sl]
            q = q_ref[pl.ds(r0, ch), sl].astype(F32)
            lb = lb_ref[0:1, sl]
            one_m_lb = 1.0 - lb
            e = jnp.exp(-jnp.abs(z))
            inv = 1.0 / (1.0 + e)
            pos = z >= 0
            f = lb + one_m_lb * (jnp.where(pos, 1.0, e) * inv)
            kf = one_m_lb * (jnp.where(pos, e, 1.0) * inv)
            pf = f
            sf = jnp.where(tau == sb - 1, 1.0, nxt(f, 1))
            for k in (1, 2, 4):
                pf = pf * jnp.where(tau >= k, prev(pf, k), 1.0)
                sf = sf * jnp.where(tau <= sb - 1 - k, nxt(sf, k), 1.0)
            g = kf
            terms = [q * g]
            for d in range(1, sb):
                g = prev(g, 1) * f
                terms.append(q * g)
            qs.append(q); ivs.append(i_ref[pl.ds(r0, ch), sl]); kfs.append(kf); pfs.append(pf); sfs.append(sf)
            bands.append(jnp.concatenate(terms, axis=1).astype(BF16))

        bands = [jnp.dot(x, wsel_ref[...], preferred_element_type=F32) for x in bands]

        levels = []
        for n in level_sizes:
            levels.append([lax.dot_general((qs[h] * pfs[h]).astype(BF16), (kfs[h] * sfs[h]).astype(BF16), nt,
                                           preferred_element_type=F32) for h in heads])
            for h in heads:
                pfs[h], sfs[h] = widen(pfs[h], sfs[h], n)
        sts = [st_ref[h] for h in heads]
        inter = [lax.dot_general((qs[h] * pfs[h]).astype(BF16), sts[h].astype(BF16), nt, preferred_element_type=F32)
                 for h in heads]
        upd = [lax.dot_general(ivs[h], (kfs[h] * sfs[h]).astype(BF16), tn, preferred_element_type=F32)
               for h in heads]

        r_end = last_row(0, ch)
        for h in heads:
            skew = pltpu.roll(bands[h], 0, 1, stride=1, stride_axis=0)
            a = jnp.where(diag_mask, skew[:, 0:ch], 0.0)
            for lvl, mask in zip(levels, level_masks):
                a = jnp.where(mask, lvl[h], a)
            o = jnp.dot(a.astype(BF16), ivs[h], preferred_element_type=F32)
            o_ref[pl.ds(r0, ch), sls[h]] = o + inter[h]
            st_ref[h] = sts[h] * pfs[h][r_end:r_end + 1, :] + upd[h]
        return carry

    lax.fori_loop(0, nch, chunk, 0, unroll=2)


def _band_selector(rev):
    w = np.zeros((HGRN_SUB * HGRN_DK, 128), np.float32)
    for d in range(HGRN_SUB):
        w[d * HGRN_DK:(d + 1) * HGRN_DK, d if rev else (128 - d) % 128] = 1.0
    return jnp.asarray(w, BF16)


def _hgrn_scan(z, bd, lb, rev):
    b, s, _ = z.shape
    ts = HGRN_TILE
    nb = s // ts
    order = (lambda j: nb - 1 - j) if rev else (lambda j: j)
    spec = lambda col: pl.BlockSpec((None, ts, HGRN_WIDTH), lambda bi, j: (bi, order(j), col))
    const = lambda shape: pl.BlockSpec(shape, lambda bi, j: (0, 0))
    body = functools.partial(_hgrn_body, rev=rev, nch=ts // HGRN_CHUNK)
    return pl.pallas_call(
        body,
        grid=(b, nb),
        in_specs=[spec(1 if rev else 0), spec(BD_Q), spec(BD_I), const((1, HGRN_WIDTH)),
                  const((HGRN_SUB * HGRN_DK, 128))],
        out_specs=spec(0),
        out_shape=jax.ShapeDtypeStruct((b, s, HGRN_WIDTH), F32),
        scratch_shapes=[pltpu.VMEM((HGRN_HEADS, HGRN_DK, HGRN_DK), F32)],
        compiler_params=_cparams(("parallel", "arbitrary")),
        name="hgrn_bwd" if rev else "hgrn_fwd",
    )(z, bd, bd, lb, _band_selector(rev))


def _fnet_a_body(u_ref, cs_ref, m_ref, o_ref, *, n1, nt):
    ngrp = FNET_WIDTH // FNET_GROUP_DIM
    for s in range(nt):
        zr, zi = [], []
        for g in range(ngrp):
            c0 = s * FNET_WIDTH + g * FNET_GROUP_DIM
            zz = jnp.dot(u_ref[:, c0:c0 + FNET_GROUP_DIM], cs_ref[...], preferred_element_type=F32)
            zr.append(zz[:, :FNET_GROUP_DIM])
            zi.append(zz[:, FNET_GROUP_DIM:])
        zcat = jnp.concatenate([jnp.concatenate(zr, axis=1), jnp.concatenate(zi, axis=1)], axis=0).astype(BF16)
        res = jnp.dot(m_ref[s], zcat, preferred_element_type=F32)
        o_ref[:, s * FNET_WIDTH:(s + 1) * FNET_WIDTH] = res.astype(o_ref.dtype)


def _fnet_c_body(b_ref, cs_ref, o_ref, *, kt):
    for k in range(kt):
        bcat = jnp.concatenate([b_ref[0, k], b_ref[1, k]], axis=0)
        x = jnp.dot(cs_ref[...], bcat, preferred_element_type=F32)
        o_ref[:, k * FNET_WIDTH:(k + 1) * FNET_WIDTH] = x.astype(o_ref.dtype)


def _fnet_tables(n1, n2):
    n = n1 * n2
    k = np.arange(FNET_GROUP_DIM)
    ang = 2.0 * np.pi * ((k[:, None] * k[None, :]) % FNET_GROUP_DIM) / FNET_GROUP_DIM
    cs_ch = np.concatenate([np.cos(ang), -np.sin(ang)], axis=1) / np.sqrt(FNET_GROUP_DIM)
    k1 = np.arange(n1)[None, :, None]
    m1 = np.arange(n1)[None, None, :]
    s2 = np.arange(n2)[:, None, None]
    idx = (k1 * m1 * n2 + s2 * k1) % n
    th = 2.0 * np.pi * idx / n
    gc, gs = np.cos(th) / np.sqrt(n1), np.sin(th) / np.sqrt(n1)
    m = np.concatenate([np.concatenate([gc, gs], axis=2), np.concatenate([-gs, gc], axis=2)], axis=1)
    k2 = np.arange(n2)
    ang2 = 2.0 * np.pi * ((k2[:, None] * k2[None, :]) % n2) / n2
    cs_seq = np.concatenate([np.cos(ang2), np.sin(ang2)], axis=1) / np.sqrt(n2)
    return (jnp.asarray(cs_ch, BF16), jnp.asarray(m, BF16), jnp.asarray(cs_seq, BF16))


def _fourier_branch(u, n1, n2):
    b, s, _ = u.shape
    cs_ch, m, cs_seq = _fnet_tables(n1, n2)
    nt = 4
    kt = 8
    u2 = u.reshape(b, n1, n2 * FNET_WIDTH)
    a = pl.pallas_call(
        functools.partial(_fnet_a_body, n1=n1, nt=nt),
        grid=(b, n2 // nt),
        in_specs=[
            pl.BlockSpec((None, n1, nt * FNET_WIDTH), lambda bi, j: (bi, 0, j)),
            pl.BlockSpec((FNET_GROUP_DIM, 2 * FNET_GROUP_DIM), lambda bi, j: (0, 0)),
            pl.BlockSpec((nt, 2 * n1, 2 * n1), lambda bi, j: (j, 0, 0)),
        ],
        out_specs=pl.BlockSpec((None, 2 * n1, nt * FNET_WIDTH), lambda bi, j: (bi, 0, j)),
        out_shape=jax.ShapeDtypeStruct((b, 2 * n1, n2 * FNET_WIDTH), BF16),
        compiler_params=_cparams(("parallel", "parallel")),
        name="fnet_a",
    )(u2, cs_ch, m)
    a5 = a.reshape(b, 2, n1, n2, FNET_WIDTH)
    y = pl.pallas_call(
        functools.partial(_fnet_c_body, kt=kt),
        grid=(b, n1 // kt),
        in_specs=[
            pl.BlockSpec((None, 2, kt, n2, FNET_WIDTH), lambda bi, j: (bi, 0, j, 0, 0)),
            pl.BlockSpec((n2, 2 * n2), lambda bi, j: (0, 0)),
        ],
        out_specs=pl.BlockSpec((None, n2, kt * FNET_WIDTH), lambda bi, j: (bi, 0, j)),
        out_shape=jax.ShapeDtypeStruct((b, n2, n1 * FNET_WIDTH), BF16),
        compiler_params=_cparams(("parallel", "parallel")),
        name="fnet_c",
    )(a5, cs_seq)
    return y.reshape(b, s, FNET_WIDTH)


def _merge_body(x_ref, o0_ref, o1_ref, o2_ref, l0_ref, l1_ref, l2_ref, hf_ref, hb_ref, gh_ref, y_ref,
                ga_ref, gb_ref, gc_ref, wa_ref, wh_ref, wf_ref, wo_ref, hg_ref, ex_ref, out_ref):
    lses = [l0_ref[...], l1_ref[...], l2_ref[...]]
    m = jnp.maximum(jnp.maximum(lses[0], lses[1]), lses[2])
    ws = [jnp.exp(l - m) for l in lses]
    inv = 1.0 / (ws[0] + ws[1] + ws[2])
    att = None
    for w, o_ref in zip(ws, (o0_ref, o1_ref, o2_ref)):
        alpha = jnp.dot((w * inv).astype(BF16), ex_ref[...], preferred_element_type=F32)
        term = alpha * o_ref[...].astype(F32)
        att = term if att is None else att + term
    a = jnp.dot(att.astype(BF16), wa_ref[...], preferred_element_type=F32)
    oh = hf_ref[...] + hb_ref[...]
    gh = gh_ref[...].astype(F32)
    heads = []
    for h in range(HGRN_HEADS):
        sl = slice(h * HGRN_DK, (h + 1) * HGRN_DK)
        gate = gh[:, sl]
        heads.append(_rms_rows(oh[:, sl], hg_ref[...]) * (gate * jax.nn.sigmoid(gate)))
    bh = jnp.dot(jnp.concatenate(heads, axis=1).astype(BF16), wh_ref[...], preferred_element_type=F32)
    c = jnp.dot(y_ref[...].astype(BF16), wf_ref[...], preferred_element_type=F32)
    sig = lambda ref: jax.nn.sigmoid(ref[...].astype(F32))
    merged = sig(ga_ref) * a + sig(gb_ref) * bh + sig(gc_ref) * c
    out_ref[...] = x_ref[...] + jnp.dot(merged.astype(BF16), wo_ref[...], preferred_element_type=F32)


def _merge(x2d, att_outs, att_lses, hf, hb, bd, y, wa, wh, wf, wo, hgain, expand, l):
    t, tm = x2d.shape[0], TOKEN_TILE
    row = lambda w: pl.BlockSpec((tm, w), lambda i: (i, 0))
    gate = lambda k: pl.BlockSpec((tm, D_MODEL), lambda i: (i, k))
    in_specs = ([row(D_MODEL)] + [row(ATT_WIDTH)] * 3 + [row(128)] * 3 + [row(HGRN_WIDTH)] * 2
                + [pl.BlockSpec((tm, HGRN_WIDTH), lambda i: (i, BD_G)),
                   row(FNET_WIDTH), gate(0), gate(1), gate(2),
                   _of_layer(l, ATT_WIDTH, D_MODEL), _of_layer(l, HGRN_WIDTH, D_MODEL),
                   _of_layer(l, FNET_WIDTH, D_MODEL), _of_layer(l, D_MODEL, D_MODEL), _of_layer(l, 1, HGRN_DK),
                   pl.BlockSpec((128, ATT_WIDTH), lambda i: (0, 0))])
    return pl.pallas_call(
        _merge_body,
        grid=(t // tm,),
        in_specs=in_specs,
        out_specs=row(D_MODEL),
        out_shape=jax.ShapeDtypeStruct((t, D_MODEL), F32),
        compiler_params=_cparams(("parallel",)),
        name="merge",
    )(x2d, *att_outs, *att_lses, hf, hb, bd, y, bd, bd, bd, wa, wh, wf, wo, hgain, expand)


def _trunk_consts(seq):
    half = HEAD_DIM // 2
    inv_freq = ROPE_THETA ** (-jnp.arange(half, dtype=F32) / half)
    rope = {}
    for _, r in WINDOW_DILATIONS:
        pos = np.arange(seq).reshape(seq // PERM, PERM // r, r).transpose(0, 2, 1).reshape(seq)
        ang = jnp.asarray(pos, F32)[:, None] * inv_freq[None, :]
        cos, sin = jnp.cos(ang), jnp.sin(ang)
        rope[r] = (jnp.concatenate([cos, cos, cos, cos], axis=1),
                   jnp.concatenate([-sin, sin, -sin, sin], axis=1))
    hd = np.arange(256) // HEAD_DIM
    head_bd = jnp.asarray((hd[:, None] == hd[None, :]) / HEAD_DIM, BF16)
    perm, perm_t = {}, {}
    for _, r in WINDOW_DILATIONS:
        idx = np.arange(PERM)
        n = PERM // r
        p = np.zeros((PERM, PERM), np.float32)
        p[idx, (idx % n) * r + idx // n] = 1.0
        perm[r], perm_t[r] = jnp.asarray(p, BF16), jnp.asarray(p.T, BF16)
    expand = np.zeros((128, ATT_WIDTH), np.float32)
    for h in range(ATT_HEADS):
        expand[_lse_lane(h), h * HEAD_DIM:(h + 1) * HEAD_DIM] = 1.0
    return dict(rope=rope, head_bd=head_bd, perm=perm, perm_t=perm_t, expand=jnp.asarray(expand, BF16), n1=128)


def _stacked_weights(hgrn_lb_logits, ffn1_norm, ffn1_w_gate, ffn1_w_up, ffn1_w_down, mix_norm, w_in, q_norm,
                     k_norm, w_att_out, hgrn_out_norm, w_hgrn_out, w_fnet_out, w_out, ffn2_norm, ffn2_w_gate,
                     ffn2_w_up, ffn2_w_down):
    bf = lambda w: w.astype(BF16)
    vec = lambda v: v.astype(F32)[:, None, :]
    heads_per_tile = ATT_WIDTH // HEAD_DIM
    p = jax.nn.softmax(hgrn_lb_logits.astype(F32), axis=0)
    cum = jnp.cumsum(p, axis=0)
    return dict(
        ffn1=(vec(ffn1_norm), bf(ffn1_w_gate), bf(ffn1_w_up), bf(ffn1_w_down)),
        ffn2=(vec(ffn2_norm), bf(ffn2_w_gate), bf(ffn2_w_up), bf(ffn2_w_down)),
        mix_norm=vec(mix_norm), w_in=bf(w_in),
        gq=vec(jnp.tile(q_norm, (1, heads_per_tile))), gk=vec(jnp.tile(k_norm, (1, heads_per_tile))),
        lb=cum - cum[0:1],
        wa=bf(w_att_out), wh=bf(w_hgrn_out), wf=bf(w_fnet_out), wo=bf(w_out), hgain=vec(hgrn_out_norm),
    )


def _encoder_layer(x, wts, consts, l):
    b, s, _ = x.shape
    t = b * s
    x2d = x.reshape(t, D_MODEL)
    x2d = _ffn(x2d, *wts["ffn1"], l)
    qkv = _proj_qkv(x2d, wts["mix_norm"], wts["w_in"], wts["gq"], wts["gk"], consts, b, s, l)
    z, bd, u = _proj_rest(x2d, wts["mix_norm"], wts["w_in"], l)
    z3, bd3 = z.reshape(b, s, Z_COLS), bd.reshape(b, s, BD_COLS)
    outs, lses = [], []
    for gi, (w, r) in enumerate(WINDOW_DILATIONS):
        assert w // (2 * r) == BAND
        o, lse = _attention_group(qkv[3 * gi], qkv[3 * gi + 1], qkv[3 * gi + 2], consts["perm_t"][r], r)
        outs.append(o.reshape(t, ATT_WIDTH))
        lses.append(lse.reshape(t, 128))
    hf = _hgrn_scan(z3, bd3, wts["lb"][l, 0].reshape(1, -1), False).reshape(t, HGRN_WIDTH)
    hb = _hgrn_scan(z3, bd3, wts["lb"][l, 1].reshape(1, -1), True).reshape(t, HGRN_WIDTH)
    y = _fourier_branch(u.reshape(b, s, FNET_WIDTH), consts["n1"], s // consts["n1"]).reshape(t, FNET_WIDTH)
    x2d = _merge(x2d, outs, lses, hf, hb, bd, y, wts["wa"], wts["wh"], wts["wf"], wts["wo"], wts["hgain"],
                 consts["expand"], l)
    x2d = _ffn(x2d, *wts["ffn2"], l)
    return x2d.reshape(b, s, D_MODEL)


def kernel(x_prompt, x_sample, hgrn_lb_logits, ffn1_norm, ffn1_w_gate, ffn1_w_up, ffn1_w_down, mix_norm, w_in,
           q_norm, k_norm, w_att_out, hgrn_out_norm, w_hgrn_out, w_fnet_out, w_out, ffn2_norm, ffn2_w_gate,
           ffn2_w_up, ffn2_w_down):
    wts = _stacked_weights(hgrn_lb_logits, ffn1_norm, ffn1_w_gate, ffn1_w_up, ffn1_w_down, mix_norm, w_in, q_norm,
                           k_norm, w_att_out, hgrn_out_norm, w_hgrn_out, w_fnet_out, w_out, ffn2_norm,
                           ffn2_w_gate, ffn2_w_up, ffn2_w_down)
    consts = _trunk_consts(max(x_prompt.shape[1], x_sample.shape[1]))

    def trunk(x):
        for l in range(w_in.shape[0]):
            x = _encoder_layer(x, wts, consts, l)
        return x

    return (trunk(x_prompt), trunk(x_sample))
```

```python
import functools

import numpy as np
import jax
import jax.numpy as jnp
from jax import lax
from jax.experimental import pallas as pl
from jax.experimental.pallas import tpu as pltpu

F32 = jnp.float32
BF16 = jnp.bfloat16

D_MODEL = 1024
D_FF = 2816
HEAD_DIM = 64
ATT_HEADS = 8
WINDOW_DILATIONS = ((128, 1), (512, 4), (2048, 16))
N_GROUPS = len(WINDOW_DILATIONS)
ATT_WIDTH = ATT_HEADS * HEAD_DIM
ATT_COLS = N_GROUPS * ATT_WIDTH
ROPE_THETA = 10000.0
HGRN_HEADS = 4
HGRN_DK = 128
HGRN_WIDTH = 512
HGRN_CHUNK = 64
HGRN_SUB = 8
FNET_WIDTH = 512
FNET_GROUP_DIM = 128
EPS = 1e-6
NEG_INF = -1e30
IN_COLS = 10752
BAND = 64
PERM = 256

QKV_COLS = 3 * ATT_COLS
Z_COLS = 2 * HGRN_WIDTH
GATE_COLS = 3 * D_MODEL
BD_COLS = GATE_COLS + 3 * HGRN_WIDTH
BD_Q, BD_I, BD_G = (GATE_COLS // HGRN_WIDTH + k for k in range(3))

V7X_VMEM_LIMIT_BYTES = 56 * 1024 * 1024
FF_CHUNKS = ((0, 1536), (1536, 2816))
TOKEN_TILE = 512
ATT_CLASS_ROWS = {1: 1024, 4: 256, 16: 128}
HGRN_TILE = 512
ATT_ROWS_PER_TILE = 64


def _cparams(sem):
    return pltpu.CompilerParams(dimension_semantics=sem, vmem_limit_bytes=V7X_VMEM_LIMIT_BYTES)


def _rms_rows(x, g):
    ms = jnp.mean(x * x, axis=-1, keepdims=True)
    return x * lax.rsqrt(ms + EPS) * g


def _ffn_body(x_ref, g_ref, wg_ref, wu_ref, wd_ref, o_ref):
    x = x_ref[...]
    h = _rms_rows(x, g_ref[...]).astype(BF16)
    acc = None
    for lo, hi in FF_CHUNKS:
        gate = jnp.dot(h, wg_ref[:, lo:hi], preferred_element_type=F32)
        up = jnp.dot(h, wu_ref[:, lo:hi], preferred_element_type=F32)
        a = (gate * jax.nn.sigmoid(gate) * up).astype(BF16)
        part = jnp.dot(a, wd_ref[lo:hi, :], preferred_element_type=F32)
        acc = part if acc is None else acc + part
    o_ref[...] = x + 0.5 * acc


def _of_layer(l, rows, cols, **kw):
    return pl.BlockSpec((None, rows, cols), lambda *_: (l, 0, 0), **kw)


def _ffn(x2d, g, wg, wu, wd, l):
    t, tm = x2d.shape[0], TOKEN_TILE
    one = pl.Buffered(1)
    return pl.pallas_call(
        _ffn_body,
        grid=(t // tm,),
        in_specs=[
            pl.BlockSpec((tm, D_MODEL), lambda i: (i, 0)),
            _of_layer(l, 1, D_MODEL),
            _of_layer(l, D_MODEL, D_FF, pipeline_mode=one),
            _of_layer(l, D_MODEL, D_FF, pipeline_mode=one),
            _of_layer(l, D_FF, D_MODEL, pipeline_mode=one),
        ],
        out_specs=pl.BlockSpec((tm, D_MODEL), lambda i: (i, 0)),
        out_shape=jax.ShapeDtypeStruct((t, D_MODEL), F32),
        compiler_params=_cparams(("parallel",)),
        name="ffn",
    )(x2d, g, wg, wu, wd)


def _head_norm_rope(acc, gain, cos, sin, bd_ref):
    tm = acc.shape[0]
    lane = lax.broadcasted_iota(jnp.int32, (tm, 128), 1)
    first_half = (lane % HEAD_DIM) < (HEAD_DIM // 2)
    slabs = []
    for s in range(ATT_WIDTH // 256):
        a = acc[:, s * 256:(s + 1) * 256]
        ms = jnp.dot((a * a).astype(BF16), bd_ref[...], preferred_element_type=F32)
        y = a * lax.rsqrt(ms + EPS) * gain[:, s * 256:(s + 1) * 256]
        for t in range(2):
            y1 = y[:, t * 128:(t + 1) * 128]
            yr = jnp.where(first_half, pltpu.roll(y1, 128 - HEAD_DIM // 2, 1), pltpu.roll(y1, HEAD_DIM // 2, 1))
            slabs.append(y1 * cos + yr * sin)
    return jnp.concatenate(slabs, axis=1)


def _proj_qkv_body(x_ref, g_ref, w_ref, gq_ref, gk_ref, c1_ref, s1_ref, c4_ref, s4_ref, c16_ref, s16_ref,
                   bd_ref, p4_ref, p16_ref, *out_refs):
    tm = x_ref.shape[0]
    h = _rms_rows(x_ref[...], g_ref[...]).astype(BF16)

    def permuted(p_ref):
        blocks = [jnp.dot(p_ref[...], h[b * PERM:(b + 1) * PERM], preferred_element_type=F32).astype(BF16)
                  for b in range(tm // PERM)]
        return jnp.concatenate(blocks, axis=0)

    lhs = (h, permuted(p4_ref), permuted(p16_ref))
    tables = ((c1_ref, s1_ref), (c4_ref, s4_ref), (c16_ref, s16_ref))
    gains = (gq_ref[...] * HEAD_DIM ** -0.5, gk_ref[...])
    for gi, (_, r) in enumerate(WINDOW_DILATIONS):
        cos, sin = tables[gi][0][...], tables[gi][1][...]
        n = PERM // r
        for ti in range(3):
            col = ti * ATT_COLS + gi * ATT_WIDTH
            y = jnp.dot(lhs[gi], w_ref[:, col:col + ATT_WIDTH], preferred_element_type=F32)
            if ti < 2:
                y = _head_norm_rope(y, gains[ti], cos, sin, bd_ref)
            y = y.astype(BF16)
            o_ref = out_refs[3 * gi + ti]
            if r == 1:
                o_ref[0] = y
            else:
                for b in range(tm // PERM):
                    for c in range(r):
                        o_ref[c, b * n:(b + 1) * n, :] = y[b * PERM + c * n:b * PERM + (c + 1) * n, :]


def _proj_rest_body(x_ref, g_ref, w_ref, z_ref, bd_ref, u_ref):
    h = _rms_rows(x_ref[...], g_ref[...]).astype(BF16)
    tn = 512
    proj = lambda c0: jnp.dot(h, w_ref[:, c0:c0 + tn], preferred_element_type=F32)
    for k in range(Z_COLS // tn):
        z_ref[:, k * tn:(k + 1) * tn] = proj(QKV_COLS + k * tn)
    hgrn0 = QKV_COLS + Z_COLS
    u0 = hgrn0 + 3 * HGRN_WIDTH
    gate0 = u0 + FNET_WIDTH
    for k in range(GATE_COLS // tn):
        bd_ref[:, k * tn:(k + 1) * tn] = proj(gate0 + k * tn).astype(BF16)
    for k in range(3 * HGRN_WIDTH // tn):
        bd_ref[:, GATE_COLS + k * tn:GATE_COLS + (k + 1) * tn] = proj(hgrn0 + k * tn).astype(BF16)
    u_ref[...] = proj(u0).astype(BF16)


def _proj_qkv(x2d, g, w_in, gq, gk, consts, batch, seq, l):
    t, tm = x2d.shape[0], TOKEN_TILE
    nsb = seq // tm
    const = lambda i: (0, 0)
    tab = pl.BlockSpec((tm, 128), lambda i: (i % nsb, 0))
    in_specs = [
        pl.BlockSpec((tm, D_MODEL), lambda i: (i, 0)),
        _of_layer(l, 1, D_MODEL),
        _of_layer(l, D_MODEL, IN_COLS, pipeline_mode=pl.Buffered(1)),
        _of_layer(l, 1, ATT_WIDTH), _of_layer(l, 1, ATT_WIDTH),
        tab, tab, tab, tab, tab, tab,
        pl.BlockSpec((256, 256), const), pl.BlockSpec((PERM, PERM), const), pl.BlockSpec((PERM, PERM), const),
    ]
    out_specs, out_shape = [], []
    for _, r in WINDOW_DILATIONS:
        for _ in range(3):
            out_specs.append(pl.BlockSpec((None, r, tm // r, ATT_WIDTH), lambda i: (i // nsb, 0, i % nsb, 0)))
            out_shape.append(jax.ShapeDtypeStruct((batch, r, seq // r, ATT_WIDTH), BF16))
    rope = consts["rope"]
    return pl.pallas_call(
        _proj_qkv_body,
        grid=(t // tm,),
        in_specs=in_specs,
        out_specs=out_specs,
        out_shape=out_shape,
        compiler_params=_cparams(("parallel",)),
        name="proj_qkv",
    )(x2d, g, w_in, gq, gk, rope[1][0], rope[1][1], rope[4][0], rope[4][1], rope[16][0], rope[16][1],
      consts["head_bd"], consts["perm"][4], consts["perm"][16])


def _proj_rest(x2d, g, w_in, l):
    t, tm = x2d.shape[0], TOKEN_TILE
    row = lambda w: pl.BlockSpec((tm, w), lambda i: (i, 0))
    return pl.pallas_call(
        _proj_rest_body,
        grid=(t // tm,),
        in_specs=[row(D_MODEL), _of_layer(l, 1, D_MODEL),
                  _of_layer(l, D_MODEL, IN_COLS, pipeline_mode=pl.Buffered(1))],
        out_specs=[row(Z_COLS), row(BD_COLS), row(FNET_WIDTH)],
        out_shape=[jax.ShapeDtypeStruct((t, Z_COLS), F32), jax.ShapeDtypeStruct((t, BD_COLS), BF16),
                   jax.ShapeDtypeStruct((t, FNET_WIDTH), BF16)],
        compiler_params=_cparams(("parallel",)),
        name="proj_rest",
    )(x2d, g, w_in)


def _lse_lane(h):
    return HEAD_DIM * (h % 2) + 16 * (h // 2)


def _attn_body(q_ref, kp_ref, kc_ref, kn_ref, vp_ref, vc_ref, vn_ref, pt_ref, o_ref, lse_ref,
               ks, vs, os_, ls, *, r, tl, seq_l):
    blk = pl.program_id(1)
    sq, kw = 128, 128 + 2 * BAND
    ks[:, 0:BAND, :] = kp_ref[...]
    ks[:, BAND:BAND + tl, :] = kc_ref[...]
    ks[:, BAND + tl:2 * BAND + tl, :] = kn_ref[...]
    vs[:, 0:BAND, :] = vp_ref[...]
    vs[:, BAND:BAND + tl, :] = vc_ref[...]
    vs[:, BAND + tl:2 * BAND + tl, :] = vn_ref[...]

    nsub = tl // sq
    hq = ATT_ROWS_PER_TILE
    qi = lax.broadcasted_iota(jnp.int32, (sq, kw), 0)
    ki = lax.broadcasted_iota(jnp.int32, (sq, kw), 1)
    in_band = jnp.abs(ki - BAND - qi) <= BAND
    lane = lax.broadcasted_iota(jnp.int32, (hq, 128), 1)
    lane_lo = lane < HEAD_DIM
    lane_pair = (lane % HEAD_DIM) // 16
    lane_k = lax.broadcasted_iota(jnp.int32, (kw, 128), 1) < HEAD_DIM
    nt = (((1,), (1,)), ((), ()))

    def unit(u, carry):
        c = u // nsub
        q0 = pl.multiple_of((u % nsub) * sq, sq)
        kpos = blk * tl + q0 - BAND + ki
        valid = in_band & (kpos >= 0) & (kpos < seq_l)
        q2 = q_ref[c, pl.ds(q0, sq), :]
        k2 = ks[c, pl.ds(q0, kw), :]
        v2 = vs[c, pl.ds(q0, kw), :]
        for half in range(sq // hq):
            r0, r1 = half * hq, (half + 1) * hq
            vmask = jnp.concatenate([valid[r0:r1], valid[r0:r1]], axis=0)
            lse_tile = jnp.zeros((hq, 128), F32)
            for hp in range(ATT_HEADS // 2):
                sl = slice(hp * 128, (hp + 1) * 128)
                q128, k128, v128 = q2[r0:r1, sl], k2[:, sl], v2[:, sl]
                zero = jnp.zeros_like(q128)
                qstack = jnp.concatenate([jnp.where(lane_lo, q128, zero), jnp.where(lane_lo, zero, q128)], axis=0)
                s = lax.dot_general(qstack, k128, nt, preferred_element_type=F32)
                s = jnp.where(vmask, s, NEG_INF)
                m = jnp.max(s, axis=-1, keepdims=True)
                p = jnp.exp(s - m).astype(BF16)
                one = jnp.ones_like(v128)
                pv0 = jnp.dot(p[0:hq], jnp.where(lane_k, v128, one), preferred_element_type=F32)
                pv1 = jnp.dot(p[hq:2 * hq], jnp.where(lane_k, one, v128), preferred_element_type=F32)
                num = jnp.where(lane_lo, pv0, pv1)
                den = pltpu.roll(jnp.where(lane_lo, pv1, pv0), HEAD_DIM, 1)
                os_[c, pl.ds(q0 + r0, hq), sl] = (num / den).astype(BF16)
                lse_pair = jnp.where(lane_lo, m[0:hq], m[hq:2 * hq]) + jnp.log(den)
                lse_tile = jnp.where(lane_pair == hp, lse_pair, lse_tile)
            ls[c, pl.ds(q0 + r0, hq), :] = lse_tile
        return carry

    lax.fori_loop(0, r * nsub, unit, 0, unroll=2)

    if r == 1:
        o_ref[...] = os_[0]
        lse_ref[...] = ls[0]
    else:
        n = PERM // r
        for b in range(r * tl // PERM):
            cat = jnp.concatenate([os_[c, b * n:(b + 1) * n, :] for c in range(r)], axis=0)
            o_ref[b * PERM:(b + 1) * PERM, :] = jnp.dot(pt_ref[...], cat, preferred_element_type=F32).astype(BF16)
        for c in range(r):
            lse_ref[pl.ds(c, tl, stride=r), :] = ls[c]


def _attention_group(q, k, v, perm_t, r):
    b, _, l, _ = q.shape
    tl = min(ATT_CLASS_ROWS[r], l)
    ts = r * tl
    nh = tl // BAND
    last = l // BAND - 1
    cur = pl.BlockSpec((None, r, tl, ATT_WIDTH), lambda bi, i: (bi, 0, i, 0))
    prv = pl.BlockSpec((None, r, BAND, ATT_WIDTH), lambda bi, i: (bi, 0, jnp.maximum(i * nh - 1, 0), 0))
    nxt = pl.BlockSpec((None, r, BAND, ATT_WIDTH), lambda bi, i: (bi, 0, jnp.minimum((i + 1) * nh, last), 0))
    body = functools.partial(_attn_body, r=r, tl=tl, seq_l=l)
    return pl.pallas_call(
        body,
        grid=(b, l // tl),
        in_specs=[cur, prv, cur, nxt, prv, cur, nxt, pl.BlockSpec((PERM, PERM), lambda bi, i: (0, 0))],
        out_specs=[pl.BlockSpec((None, ts, ATT_WIDTH), lambda bi, i: (bi, i, 0)),
                   pl.BlockSpec((None, ts, 128), lambda bi, i: (bi, i, 0))],
        out_shape=[jax.ShapeDtypeStruct((b, r * l, ATT_WIDTH), BF16), jax.ShapeDtypeStruct((b, r * l, 128), F32)],
        scratch_shapes=[
            pltpu.VMEM((r, tl + 2 * BAND, ATT_WIDTH), BF16),
            pltpu.VMEM((r, tl + 2 * BAND, ATT_WIDTH), BF16),
            pltpu.VMEM((r, tl, ATT_WIDTH), BF16),
            pltpu.VMEM((r, tl, 128), F32),
        ],
        compiler_params=_cparams(("parallel", "parallel")),
        name=f"attn_r{r}",
    )(q, k, k, k, v, v, v, perm_t)


def _hgrn_body(z_ref, q_ref, i_ref, lb_ref, wsel_ref, o_ref, st_ref, *, rev, nch):
    @pl.when(pl.program_id(1) == 0)
    def _():
        st_ref[...] = jnp.zeros_like(st_ref)

    ch, sb = HGRN_CHUNK, HGRN_SUB
    row = lax.broadcasted_iota(jnp.int32, (ch, ch), 0)
    col = lax.broadcasted_iota(jnp.int32, (ch, ch), 1)
    dist = (col - row) if rev else (row - col)
    same = lambda n: (row // n) == (col // n)
    diag_mask = same(sb) & (dist >= 0)
    level_sizes = (8, 16, 32)
    level_masks = [same(2 * n) & jnp.logical_not(same(n)) & (dist > 0) for n in level_sizes]
    sub = lax.broadcasted_iota(jnp.int32, (ch, HGRN_DK), 0) % sb
    tau = (sb - 1 - sub) if rev else sub
    nt = (((1,), (1,)), ((), ()))
    tn = (((0,), (0,)), ((), ()))
    def roll8(x, k):
        return pltpu.roll(x.reshape(ch // sb, sb, HGRN_DK), k % sb, 1).reshape(ch, HGRN_DK)

    prev = lambda x, k: roll8(x, -k if rev else k)
    nxt = lambda x, k: roll8(x, k if rev else -k)
    last_row = lambda start, n: start if rev else start + n - 1

    def widen(pf, sf, n):
        pf_rows, sf_rows = [], []
        for v in range(ch // 8):
            blk = (8 * v) // n
            first = blk % 2 == 0
            later = first if rev else not first
            sib = (blk + 1) * n if first else (blk - 1) * n
            r = last_row(sib, n)
            total = jnp.broadcast_to(pf[r:r + 1, :], (8, HGRN_DK))
            x, y = pf[8 * v:8 * v + 8], sf[8 * v:8 * v + 8]
            pf_rows.append(x * total if later else x)
            sf_rows.append(y if later else y * total)
        return jnp.concatenate(pf_rows, axis=0), jnp.concatenate(sf_rows, axis=0)

    def chunk(ci, carry):
        c = (nch - 1 - ci) if rev else ci
        r0 = pl.multiple_of(c * ch, ch)
        heads = range(HGRN_HEADS)
        sls = [slice(h * HGRN_DK, (h + 1) * HGRN_DK) for h in heads]
        qs, ivs, kfs, pfs, sfs, bands = [], [], [], [], [], []
        for sl in sls:
            z = z_ref[pl.ds(r0, ch), sl]
            q = q_ref[pl.ds(r0, ch), sl].astype(F32)
            lb = lb_ref[0:1, sl]
            one_m_lb = 1.0 - lb
            e = jnp.exp(-jnp.abs(z))
            inv = 1.0 / (1.0 + e)
            pos = z >= 0
            f = lb + one_m_lb * (jnp.where(pos, 1.0, e) * inv)
            kf = one_m_lb * (jnp.where(pos, e, 1.0) * inv)
            pf = f
            sf = jnp.where(tau == sb - 1, 1.0, nxt(f, 1))
            for k in (1, 2, 4):
                pf = pf * jnp.where(tau >= k, prev(pf, k), 1.0)
                sf = sf * jnp.where(tau <= sb - 1 - k, nxt(sf, k), 1.0)
            g = kf
            terms = [q * g]
            for d in range(1, sb):
                g = prev(g, 1) * f
                terms.append(q * g)
            qs.append(q); ivs.append(i_ref[pl.ds(r0, ch), sl]); kfs.append(kf); pfs.append(pf); sfs.append(sf)
            bands.append(jnp.concatenate(terms, axis=1).astype(BF16))

        bands = [jnp.dot(x, wsel_ref[...], preferred_element_type=F32) for x in bands]

        levels = []
        for n in level_sizes:
            levels.append([lax.dot_general((qs[h] * pfs[h]).astype(BF16), (kfs[h] * sfs[h]).astype(BF16), nt,
                                           preferred_element_type=F32) for h in heads])
            for h in heads:
                pfs[h], sfs[h] = widen(pfs[h], sfs[h], n)
        sts = [st_ref[h] for h in heads]
        inter = [lax.dot_general((qs[h] * pfs[h]).astype(BF16), sts[h].astype(BF16), nt, preferred_element_type=F32)
                 for h in heads]
        upd = [lax.dot_general(ivs[h], (kfs[h] * sfs[h]).astype(BF16), tn, preferred_element_type=F32)
               for h in heads]

        r_end = last_row(0, ch)
        for h in heads:
            skew = pltpu.roll(bands[h], 0, 1, stride=1, stride_axis=0)
            a = jnp.where(diag_mask, skew[:, 0:ch], 0.0)
            for lvl, mask in zip(levels, level_masks):
                a = jnp.where(mask, lvl[h], a)
            o = jnp.dot(a.astype(BF16), ivs[h], preferred_element_type=F32)
            o_ref[pl.ds(r0, ch), sls[h]] = (o + inter[h]).astype(o_ref.dtype)
            st_ref[h] = sts[h] * pfs[h][r_end:r_end + 1, :] + upd[h]
        return carry

    lax.fori_loop(0, nch, chunk, 0, unroll=4)


def _band_selector(rev):
    w = np.zeros((HGRN_SUB * HGRN_DK, 128), np.float32)
    for d in range(HGRN_SUB):
        w[d * HGRN_DK:(d + 1) * HGRN_DK, d if rev else (128 - d) % 128] = 1.0
    return jnp.asarray(w, BF16)


def _hgrn_scan(z, bd, lb, rev):
    b, s, _ = z.shape
    ts = HGRN_TILE
    nb = s // ts
    order = (lambda j: nb - 1 - j) if rev else (lambda j: j)
    spec = lambda col: pl.BlockSpec((None, ts, HGRN_WIDTH), lambda bi, j: (bi, order(j), col))
    const = lambda shape: pl.BlockSpec(shape, lambda bi, j: (0, 0))
    body = functools.partial(_hgrn_body, rev=rev, nch=ts // HGRN_CHUNK)
    return pl.pallas_call(
        body,
        grid=(b, nb),
        in_specs=[spec(1 if rev else 0), spec(BD_Q), spec(BD_I), const((1, HGRN_WIDTH)),
                  const((HGRN_SUB * HGRN_DK, 128))],
        out_specs=spec(0),
        out_shape=jax.ShapeDtypeStruct((b, s, HGRN_WIDTH), BF16),
        scratch_shapes=[pltpu.VMEM((HGRN_HEADS, HGRN_DK, HGRN_DK), F32)],
        compiler_params=_cparams(("parallel", "arbitrary")),
        name="hgrn_bwd" if rev else "hgrn_fwd",
    )(z, bd, bd, lb, _band_selector(rev))


def _fnet_a_body(u_ref, cs_ref, m_ref, o_ref, *, n1, nt):
    ngrp = FNET_WIDTH // FNET_GROUP_DIM
    for s in range(nt):
        zr, zi = [], []
        for g in range(ngrp):
            c0 = s * FNET_WIDTH + g * FNET_GROUP_DIM
            zz = jnp.dot(u_ref[:, c0:c0 + FNET_GROUP_DIM], cs_ref[...], preferred_element_type=F32)
            zr.append(zz[:, :FNET_GROUP_DIM])
            zi.append(zz[:, FNET_GROUP_DIM:])
        zcat = jnp.concatenate([jnp.concatenate(zr, axis=1), jnp.concatenate(zi, axis=1)], axis=0).astype(BF16)
        res = jnp.dot(m_ref[s], zcat, preferred_element_type=F32)
        o_ref[:, s * FNET_WIDTH:(s + 1) * FNET_WIDTH] = res.astype(o_ref.dtype)


def _fnet_c_body(b_ref, cs_ref, o_ref, *, kt):
    for k in range(kt):
        bcat = jnp.concatenate([b_ref[0, k], b_ref[1, k]], axis=0)
        x = jnp.dot(cs_ref[...], bcat, preferred_element_type=F32)
        o_ref[:, k * FNET_WIDTH:(k + 1) * FNET_WIDTH] = x.astype(o_ref.dtype)


def _fnet_tables(n1, n2):
    n = n1 * n2
    k = np.arange(FNET_GROUP_DIM)
    ang = 2.0 * np.pi * ((k[:, None] * k[None, :]) % FNET_GROUP_DIM) / FNET_GROUP_DIM
    cs_ch = np.concatenate([np.cos(ang), -np.sin(ang)], axis=1) / np.sqrt(FNET_GROUP_DIM)
    k1 = np.arange(n1)[None, :, None]
    m1 = np.arange(n1)[None, None, :]
    s2 = np.arange(n2)[:, None, None]
    idx = (k1 * m1 * n2 + s2 * k1) % n
    th = 2.0 * np.pi * idx / n
    gc, gs = np.cos(th) / np.sqrt(n1), np.sin(th) / np.sqrt(n1)
    m = np.concatenate([np.concatenate([gc, gs], axis=2), np.concatenate([-gs, gc], axis=2)], axis=1)
    k2 = np.arange(n2)
    ang2 = 2.0 * np.pi * ((k2[:, None] * k2[None, :]) % n2) / n2
    cs_seq = np.concatenate([np.cos(ang2), np.sin(ang2)], axis=1) / np.sqrt(n2)
    return (jnp.asarray(cs_ch, BF16), jnp.asarray(m, BF16), jnp.asarray(cs_seq, BF16))


def _fourier_branch(u, n1, n2):
    b, s, _ = u.shape
    cs_ch, m, cs_seq = _fnet_tables(n1, n2)
    nt = 4
    kt = 8
    u2 = u.reshape(b, n1, n2 * FNET_WIDTH)
    a = pl.pallas_call(
        functools.partial(_fnet_a_body, n1=n1, nt=nt),
        grid=(b, n2 // nt),
        in_specs=[
            pl.BlockSpec((None, n1, nt * FNET_WIDTH), lambda bi, j: (bi, 0, j)),
            pl.BlockSpec((FNET_GROUP_DIM, 2 * FNET_GROUP_DIM), lambda bi, j: (0, 0)),
            pl.BlockSpec((nt, 2 * n1, 2 * n1), lambda bi, j: (j, 0, 0)),
        ],
        out_specs=pl.BlockSpec((None, 2 * n1, nt * FNET_WIDTH), lambda bi, j: (bi, 0, j)),
        out_shape=jax.ShapeDtypeStruct((b, 2 * n1, n2 * FNET_WIDTH), BF16),
        compiler_params=_cparams(("parallel", "parallel")),
        name="fnet_a",
    )(u2, cs_ch, m)
    a5 = a.reshape(b, 2, n1, n2, FNET_WIDTH)
    y = pl.pallas_call(
        functools.partial(_fnet_c_body, kt=kt),
        grid=(b, n1 // kt),
        in_specs=[
            pl.BlockSpec((None, 2, kt, n2, FNET_WIDTH), lambda bi, j: (bi, 0, j, 0, 0)),
            pl.BlockSpec((n2, 2 * n2), lambda bi, j: (0, 0)),
        ],
        out_specs=pl.BlockSpec((None, n2, kt * FNET_WIDTH), lambda bi, j: (bi, 0, j)),
        out_shape=jax.ShapeDtypeStruct((b, n2, n1 * FNET_WIDTH), BF16),
        compiler_params=_cparams(("parallel", "parallel")),
        name="fnet_c",
    )(a5, cs_seq)
    return y.reshape(b, s, FNET_WIDTH)


def _merge_body(x_ref, o0_ref, o1_ref, o2_ref, l0_ref, l1_ref, l2_ref, hf_ref, hb_ref, gh_ref, y_ref,
                ga_ref, gb_ref, gc_ref, wa_ref, wh_ref, wf_ref, wo_ref, hg_ref, ex_ref, out_ref):
    lses = [l0_ref[...], l1_ref[...], l2_ref[...]]
    m = jnp.maximum(jnp.maximum(lses[0], lses[1]), lses[2])
    ws = [jnp.exp(l - m) for l in lses]
    inv = 1.0 / (ws[0] + ws[1] + ws[2])
    att = None
    for w, o_ref in zip(ws, (o0_ref, o1_ref, o2_ref)):
        alpha = jnp.dot((w * inv).astype(BF16), ex_ref[...], preferred_element_type=F32)
        term = alpha * o_ref[...].astype(F32)
        att = term if att is None else att + term
    a = jnp.dot(att.astype(BF16), wa_ref[...], preferred_element_type=F32)
    oh = hf_ref[...].astype(F32) + hb_ref[...].astype(F32)
    gh = gh_ref[...].astype(F32)
    heads = []
    for h in range(HGRN_HEADS):
        sl = slice(h * HGRN_DK, (h + 1) * HGRN_DK)
        gate = gh[:, sl]
        heads.append(_rms_rows(oh[:, sl], hg_ref[...]) * (gate * jax.nn.sigmoid(gate)))
    bh = jnp.dot(jnp.concatenate(heads, axis=1).astype(BF16), wh_ref[...], preferred_element_type=F32)
    c = jnp.dot(y_ref[...].astype(BF16), wf_ref[...], preferred_element_type=F32)
    sig = lambda ref: jax.nn.sigmoid(ref[...].astype(F32))
    merged = sig(ga_ref) * a + sig(gb_ref) * bh + sig(gc_ref) * c
    out_ref[...] = x_ref[...] + jnp.dot(merged.astype(BF16), wo_ref[...], preferred_element_type=F32)


def _merge(x2d, att_outs, att_lses, hf, hb, bd, y, wa, wh, wf, wo, hgain, expand, l):
    t, tm = x2d.shape[0], TOKEN_TILE
    row = lambda w: pl.BlockSpec((tm, w), lambda i: (i, 0))
    gate = lambda k: pl.BlockSpec((tm, D_MODEL), lambda i: (i, k))
    in_specs = ([row(D_MODEL)] + [row(ATT_WIDTH)] * 3 + [row(128)] * 3 + [row(HGRN_WIDTH)] * 2
                + [pl.BlockSpec((tm, HGRN_WIDTH), lambda i: (i, BD_G)),
                   row(FNET_WIDTH), gate(0), gate(1), gate(2),
                   _of_layer(l, ATT_WIDTH, D_MODEL), _of_layer(l, HGRN_WIDTH, D_MODEL),
                   _of_layer(l, FNET_WIDTH, D_MODEL), _of_layer(l, D_MODEL, D_MODEL), _of_layer(l, 1, HGRN_DK),
                   pl.BlockSpec((128, ATT_WIDTH), lambda i: (0, 0))])
    return pl.pallas_call(
        _merge_body,
        grid=(t // tm,),
        in_specs=in_specs,
        out_specs=row(D_MODEL),
        out_shape=jax.ShapeDtypeStruct((t, D_MODEL), F32),
        compiler_params=_cparams(("parallel",)),
        name="merge",
    )(x2d, *att_outs, *att_lses, hf, hb, bd, y, bd, bd, bd, wa, wh, wf, wo, hgain, expand)


def _trunk_consts(seq):
    half = HEAD_DIM // 2
    inv_freq = ROPE_THETA ** (-jnp.arange(half, dtype=F32) / half)
    ang = jnp.arange(seq).astype(F32)[:, None] * inv_freq[None, :]
    cos_nat, sin_nat = jnp.cos(ang), jnp.sin(ang)
    rope = {}
    for _, r in WINDOW_DILATIONS:
        order = lambda tab: tab.reshape(seq // PERM, PERM // r, r, half).transpose(0, 2, 1, 3).reshape(seq, half)
        cos, sin = order(cos_nat), order(sin_nat)
        rope[r] = (jnp.concatenate([cos, cos, cos, cos], axis=1),
                   jnp.concatenate([-sin, sin, -sin, sin], axis=1))
    hd = np.arange(256) // HEAD_DIM
    head_bd = jnp.asarray((hd[:, None] == hd[None, :]) / HEAD_DIM, BF16)
    perm, perm_t = {}, {}
    for _, r in WINDOW_DILATIONS:
        idx = np.arange(PERM)
        n = PERM // r
        p = np.zeros((PERM, PERM), np.float32)
        p[idx, (idx % n) * r + idx // n] = 1.0
        perm[r], perm_t[r] = jnp.asarray(p, BF16), jnp.asarray(p.T, BF16)
    expand = np.zeros((128, ATT_WIDTH), np.float32)
    for h in range(ATT_HEADS):
        expand[_lse_lane(h), h * HEAD_DIM:(h + 1) * HEAD_DIM] = 1.0
    return dict(rope=rope, head_bd=head_bd, perm=perm, perm_t=perm_t, expand=jnp.asarray(expand, BF16), n1=128)


def _stacked_weights(hgrn_lb_logits, ffn1_norm, ffn1_w_gate, ffn1_w_up, ffn1_w_down, mix_norm, w_in, q_norm,
                     k_norm, w_att_out, hgrn_out_norm, w_hgrn_out, w_fnet_out, w_out, ffn2_norm, ffn2_w_gate,
                     ffn2_w_up, ffn2_w_down):
    bf = lambda w: w.astype(BF16)
    vec = lambda v: v.astype(F32)[:, None, :]
    heads_per_tile = ATT_WIDTH // HEAD_DIM
    p = jax.nn.softmax(hgrn_lb_logits.astype(F32), axis=0)
    cum = jnp.cumsum(p, axis=0)
    return dict(
        ffn1=(vec(ffn1_norm), bf(ffn1_w_gate), bf(ffn1_w_up), bf(ffn1_w_down)),
        ffn2=(vec(ffn2_norm), bf(ffn2_w_gate), bf(ffn2_w_up), bf(ffn2_w_down)),
        mix_norm=vec(mix_norm), w_in=bf(w_in),
        gq=vec(jnp.tile(q_norm, (1, heads_per_tile))), gk=vec(jnp.tile(k_norm, (1, heads_per_tile))),
        lb=cum - cum[0:1],
        wa=bf(w_att_out), wh=bf(w_hgrn_out), wf=bf(w_fnet_out), wo=bf(w_out), hgain=vec(hgrn_out_norm),
    )


def _encoder_layer(x, wts, consts, l):
    b, s, _ = x.shape
    t = b * s
    x2d = x.reshape(t, D_MODEL)
    x2d = _ffn(x2d, *wts["ffn1"], l)
    qkv = _proj_qkv(x2d, wts["mix_norm"], wts["w_in"], wts["gq"], wts["gk"], consts, b, s, l)
    z, bd, u = _proj_rest(x2d, wts["mix_norm"], wts["w_in"], l)
    z3, bd3 = z.reshape(b, s, Z_COLS), bd.reshape(b, s, BD_COLS)
    outs, lses = [], []
    for gi, (w, r) in enumerate(WINDOW_DILATIONS):
        assert w // (2 * r) == BAND
        o, lse = _attention_group(qkv[3 * gi], qkv[3 * gi + 1], qkv[3 * gi + 2], consts["perm_t"][r], r)
        outs.append(o.reshape(t, ATT_WIDTH))
        lses.append(lse.reshape(t, 128))
    hf = _hgrn_scan(z3, bd3, wts["lb"][l, 0].reshape(1, -1), False).reshape(t, HGRN_WIDTH)
    hb = _hgrn_scan(z3, bd3, wts["lb"][l, 1].reshape(1, -1), True).reshape(t, HGRN_WIDTH)
    y = _fourier_branch(u.reshape(b, s, FNET_WIDTH), consts["n1"], s // consts["n1"]).reshape(t, FNET_WIDTH)
    x2d = _merge(x2d, outs, lses, hf, hb, bd, y, wts["wa"], wts["wh"], wts["wf"], wts["wo"], wts["hgain"],
                 consts["expand"], l)
    x2d = _ffn(x2d, *wts["ffn2"], l)
    return x2d.reshape(b, s, D_MODEL)


def kernel(x_prompt, x_sample, hgrn_lb_logits, ffn1_norm, ffn1_w_gate, ffn1_w_up, ffn1_w_down, mix_norm, w_in,
           q_norm, k_norm, w_att_out, hgrn_out_norm, w_hgrn_out, w_fnet_out, w_out, ffn2_norm, ffn2_w_gate,
           ffn2_w_up, ffn2_w_down):
    wts = _stacked_weights(hgrn_lb_logits, ffn1_norm, ffn1_w_gate, ffn1_w_up, ffn1_w_down, mix_norm, w_in, q_norm,
                           k_norm, w_att_out, hgrn_out_norm, w_hgrn_out, w_fnet_out, w_out, ffn2_norm,
                           ffn2_w_gate, ffn2_w_up, ffn2_w_down)
    consts = _trunk_consts(max(x_prompt.shape[1], x_sample.shape[1]))

    def trunk(x):
        for l in range(w_in.shape[0]):
            x = _encoder_layer(x, wts, consts, l)
        return x

    return (trunk(x_prompt), trunk(x_sample))
```

```python
import functools

import numpy as np
import jax
import jax.numpy as jnp
from jax import lax
from jax.experimental import pallas as pl
from jax.experimental.pallas import tpu as pltpu

F32 = jnp.float32
BF16 = jnp.bfloat16

D_MODEL = 1024
D_FF = 2816
HEAD_DIM = 64
ATT_HEADS = 8
WINDOW_DILATIONS = ((128, 1), (512, 4), (2048, 16))
N_GROUPS = len(WINDOW_DILATIONS)
ATT_WIDTH = ATT_HEADS * HEAD_DIM
ATT_COLS = N_GROUPS * ATT_WIDTH
ROPE_THETA = 10000.0
HGRN_HEADS = 4
HGRN_DK = 128
HGRN_WIDTH = 512
HGRN_CHUNK = 64
HGRN_SUB = 4
V7X_SUBLANES = 8
FNET_WIDTH = 512
FNET_GROUP_DIM = 128
EPS = 1e-6
NEG_INF = -1e30
IN_COLS = 10752
BAND = 64
PERM = 256

QKV_COLS = 3 * ATT_COLS
Z_COLS = 2 * HGRN_WIDTH
GATE_COLS = 3 * D_MODEL
BD_COLS = GATE_COLS + 3 * HGRN_WIDTH
BD_Q, BD_I, BD_G = (GATE_COLS // HGRN_WIDTH + k for k in range(3))

V7X_VMEM_LIMIT_BYTES = 56 * 1024 * 1024
FF_CHUNKS = ((0, 768), (768, 1536), (1536, 2304), (2304, 2816))
FFN_TOKEN_TILE = 1024
TOKEN_TILE = 512
ATT_CLASS_ROWS = {1: 1024, 4: 256, 16: 128}
HGRN_TILE = 512
FNET_SLABS_PER_STEP = 8
FNET_ROWS_PER_STEP = 32
ATT_ROWS_PER_TILE = 64


def _cparams(sem):
    return pltpu.CompilerParams(dimension_semantics=sem, vmem_limit_bytes=V7X_VMEM_LIMIT_BYTES)


def _rms_rows(x, g):
    ms = jnp.mean(x * x, axis=-1, keepdims=True)
    return x * lax.rsqrt(ms + EPS) * g


def _ffn_body(x_ref, g_ref, wg_ref, wu_ref, wd_ref, o_ref):
    x = x_ref[...]
    h = _rms_rows(x, g_ref[...]).astype(BF16)
    acc = None
    for lo, hi in FF_CHUNKS:
        gate = jnp.dot(h, wg_ref[:, lo:hi], preferred_element_type=F32)
        up = jnp.dot(h, wu_ref[:, lo:hi], preferred_element_type=F32)
        a = (gate * jax.nn.sigmoid(gate) * up).astype(BF16)
        part = jnp.dot(a, wd_ref[lo:hi, :], preferred_element_type=F32)
        acc = part if acc is None else acc + part
    o_ref[...] = x + 0.5 * acc


def _of_layer(l, rows, cols, **kw):
    return pl.BlockSpec((None, rows, cols), lambda *_: (l, 0, 0), **kw)


def _ffn(x2d, g, wg, wu, wd, l):
    t, tm = x2d.shape[0], FFN_TOKEN_TILE
    one = pl.Buffered(1)
    return pl.pallas_call(
        _ffn_body,
        grid=(t // tm,),
        in_specs=[
            pl.BlockSpec((tm, D_MODEL), lambda i: (i, 0)),
            _of_layer(l, 1, D_MODEL),
            _of_layer(l, D_MODEL, D_FF, pipeline_mode=one),
            _of_layer(l, D_MODEL, D_FF, pipeline_mode=one),
            _of_layer(l, D_FF, D_MODEL, pipeline_mode=one),
        ],
        out_specs=pl.BlockSpec((tm, D_MODEL), lambda i: (i, 0)),
        out_shape=jax.ShapeDtypeStruct((t, D_MODEL), F32),
        compiler_params=_cparams(("parallel",)),
        name="ffn",
    )(x2d, g, wg, wu, wd)


def _head_norm_rope(acc, gain, cos, sin, bd_ref):
    tm = acc.shape[0]
    lane = lax.broadcasted_iota(jnp.int32, (tm, 128), 1)
    first_half = (lane % HEAD_DIM) < (HEAD_DIM // 2)
    slabs = []
    for s in range(ATT_WIDTH // 256):
        a = acc[:, s * 256:(s + 1) * 256]
        ms = jnp.dot((a * a).astype(BF16), bd_ref[...], preferred_element_type=F32)
        y = a * lax.rsqrt(ms + EPS) * gain[:, s * 256:(s + 1) * 256]
        for t in range(2):
            y1 = y[:, t * 128:(t + 1) * 128]
            yr = jnp.where(first_half, pltpu.roll(y1, 128 - HEAD_DIM // 2, 1), pltpu.roll(y1, HEAD_DIM // 2, 1))
            slabs.append(y1 * cos + yr * sin)
    return jnp.concatenate(slabs, axis=1)


def _qkv_pieces(x_ref, g_ref, w_ref, gq_ref, gk_ref, table_refs, bd_ref, p4_ref, p16_ref, out_refs):
    tm = x_ref.shape[0]
    h = _rms_rows(x_ref[...], g_ref[...]).astype(BF16)

    def permuted(p_ref):
        blocks = [jnp.dot(p_ref[...], h[b * PERM:(b + 1) * PERM], preferred_element_type=F32).astype(BF16)
                  for b in range(tm // PERM)]
        return jnp.concatenate(blocks, axis=0)

    lhs = (h, permuted(p4_ref), permuted(p16_ref))
    gains = (gq_ref[...] * HEAD_DIM ** -0.5, gk_ref[...])

    def piece(gi, ti):
        def run():
            r = WINDOW_DILATIONS[gi][1]
            n = PERM // r
            col = ti * ATT_COLS + gi * ATT_WIDTH
            y = jnp.dot(lhs[gi], w_ref[:, col:col + ATT_WIDTH], preferred_element_type=F32)
            if ti < 2:
                y = _head_norm_rope(y, gains[ti], table_refs[2 * gi][...], table_refs[2 * gi + 1][...], bd_ref)
            y = y.astype(BF16)
            o_ref = out_refs[3 * gi + ti]
            if r == 1:
                o_ref[0] = y
            else:
                for b in range(tm // PERM):
                    for c in range(r):
                        o_ref[c, b * n:(b + 1) * n, :] = y[b * PERM + c * n:b * PERM + (c + 1) * n, :]
        return run

    return [piece(gi, ti) for gi in range(N_GROUPS) for ti in range(3)]


def _proj_qkv_body(x_ref, g_ref, w_ref, gq_ref, gk_ref, c1_ref, s1_ref, c4_ref, s4_ref, c16_ref, s16_ref,
                   bd_ref, p4_ref, p16_ref, *out_refs):
    tables = (c1_ref, s1_ref, c4_ref, s4_ref, c16_ref, s16_ref)
    for run in _qkv_pieces(x_ref, g_ref, w_ref, gq_ref, gk_ref, tables, bd_ref, p4_ref, p16_ref, out_refs):
        run()


def _proj_rest_body(x_ref, g_ref, w_ref, z_ref, bd_ref, u_ref):
    h = _rms_rows(x_ref[...], g_ref[...]).astype(BF16)
    tn = 512
    proj = lambda c0: jnp.dot(h, w_ref[:, c0:c0 + tn], preferred_element_type=F32)
    for k in range(Z_COLS // tn):
        z_ref[:, k * tn:(k + 1) * tn] = proj(QKV_COLS + k * tn)
    hgrn0 = QKV_COLS + Z_COLS
    u0 = hgrn0 + 3 * HGRN_WIDTH
    gate0 = u0 + FNET_WIDTH
    for k in range(GATE_COLS // tn):
        bd_ref[:, k * tn:(k + 1) * tn] = proj(gate0 + k * tn).astype(BF16)
    for k in range(3 * HGRN_WIDTH // tn):
        bd_ref[:, GATE_COLS + k * tn:GATE_COLS + (k + 1) * tn] = proj(hgrn0 + k * tn).astype(BF16)
    u_ref[...] = proj(u0).astype(BF16)


def _proj_qkv(x2d, g, w_in, gq, gk, consts, batch, seq, l):
    t, tm = x2d.shape[0], TOKEN_TILE
    nsb = seq // tm
    const = lambda i: (0, 0)
    tab = pl.BlockSpec((tm, 128), lambda i: (i % nsb, 0))
    in_specs = [
        pl.BlockSpec((tm, D_MODEL), lambda i: (i, 0)),
        _of_layer(l, 1, D_MODEL),
        _of_layer(l, D_MODEL, IN_COLS, pipeline_mode=pl.Buffered(1)),
        _of_layer(l, 1, ATT_WIDTH), _of_layer(l, 1, ATT_WIDTH),
        tab, tab, tab, tab, tab, tab,
        pl.BlockSpec((256, 256), const), pl.BlockSpec((PERM, PERM), const), pl.BlockSpec((PERM, PERM), const),
    ]
    out_specs, out_shape = [], []
    for _, r in WINDOW_DILATIONS:
        for _ in range(3):
            out_specs.append(pl.BlockSpec((None, r, tm // r, ATT_WIDTH), lambda i: (i // nsb, 0, i % nsb, 0)))
            out_shape.append(jax.ShapeDtypeStruct((batch, r, seq // r, ATT_WIDTH), BF16))
    rope = consts["rope"]
    return pl.pallas_call(
        _proj_qkv_body,
        grid=(t // tm,),
        in_specs=in_specs,
        out_specs=out_specs,
        out_shape=out_shape,
        compiler_params=_cparams(("parallel",)),
        name="proj_qkv",
    )(x2d, g, w_in, gq, gk, rope[1][0], rope[1][1], rope[4][0], rope[4][1], rope[16][0], rope[16][1],
      consts["head_bd"], consts["perm"][4], consts["perm"][16])


def _proj_rest(x2d, g, w_in, l):
    t, tm = x2d.shape[0], TOKEN_TILE
    row = lambda w: pl.BlockSpec((tm, w), lambda i: (i, 0))
    return pl.pallas_call(
        _proj_rest_body,
        grid=(t // tm,),
        in_specs=[row(D_MODEL), _of_layer(l, 1, D_MODEL),
                  _of_layer(l, D_MODEL, IN_COLS, pipeline_mode=pl.Buffered(1))],
        out_specs=[row(Z_COLS), row(BD_COLS), row(FNET_WIDTH)],
        out_shape=[jax.ShapeDtypeStruct((t, Z_COLS), F32), jax.ShapeDtypeStruct((t, BD_COLS), BF16),
                   jax.ShapeDtypeStruct((t, FNET_WIDTH), BF16)],
        compiler_params=_cparams(("parallel",)),
        name="proj_rest",
    )(x2d, g, w_in)


def _lse_lane(h):
    return HEAD_DIM * (h % 2) + 16 * (h // 2)


def _attn_body(q_ref, kp_ref, kc_ref, kn_ref, vp_ref, vc_ref, vn_ref, pt_ref, o_ref, lse_ref,
               ks, vs, os_, ls, *, r, tl, seq_l):
    blk = pl.program_id(1)
    sq, kw = 128, 128 + 2 * BAND
    ks[:, 0:BAND, :] = kp_ref[...]
    ks[:, BAND:BAND + tl, :] = kc_ref[...]
    ks[:, BAND + tl:2 * BAND + tl, :] = kn_ref[...]
    vs[:, 0:BAND, :] = vp_ref[...]
    vs[:, BAND:BAND + tl, :] = vc_ref[...]
    vs[:, BAND + tl:2 * BAND + tl, :] = vn_ref[...]

    nsub = tl // sq
    hq = ATT_ROWS_PER_TILE
    qi = lax.broadcasted_iota(jnp.int32, (sq, kw), 0)
    ki = lax.broadcasted_iota(jnp.int32, (sq, kw), 1)
    in_band = jnp.abs(ki - BAND - qi) <= BAND
    lane = lax.broadcasted_iota(jnp.int32, (hq, 128), 1)
    lane_lo = lane < HEAD_DIM
    lane_pair = (lane % HEAD_DIM) // 16
    lane_k = lax.broadcasted_iota(jnp.int32, (kw, 128), 1) < HEAD_DIM
    nt = (((1,), (1,)), ((), ()))

    def unit(u, carry):
        c = u // nsub
        q0 = pl.multiple_of((u % nsub) * sq, sq)
        kpos = blk * tl + q0 - BAND + ki
        valid = in_band & (kpos >= 0) & (kpos < seq_l)
        q2 = q_ref[c, pl.ds(q0, sq), :]
        k2 = ks[c, pl.ds(q0, kw), :]
        v2 = vs[c, pl.ds(q0, kw), :]
        for half in range(sq // hq):
            r0, r1 = half * hq, (half + 1) * hq
            vmask = jnp.concatenate([valid[r0:r1], valid[r0:r1]], axis=0)
            lse_tile = jnp.zeros((hq, 128), F32)
            for hp in range(ATT_HEADS // 2):
                sl = slice(hp * 128, (hp + 1) * 128)
                q128, k128, v128 = q2[r0:r1, sl], k2[:, sl], v2[:, sl]
                zero = jnp.zeros_like(q128)
                qstack = jnp.concatenate([jnp.where(lane_lo, q128, zero), jnp.where(lane_lo, zero, q128)], axis=0)
                s = lax.dot_general(qstack, k128, nt, preferred_element_type=F32)
                s = jnp.where(vmask, s, NEG_INF)
                m = jnp.max(s, axis=-1, keepdims=True)
                p = jnp.exp(s - m).astype(BF16)
                one = jnp.ones_like(v128)
                pv0 = jnp.dot(p[0:hq], jnp.where(lane_k, v128, one), preferred_element_type=F32)
                pv1 = jnp.dot(p[hq:2 * hq], jnp.where(lane_k, one, v128), preferred_element_type=F32)
                num = jnp.where(lane_lo, pv0, pv1)
                den = pltpu.roll(jnp.where(lane_lo, pv1, pv0), HEAD_DIM, 1)
                os_[c, pl.ds(q0 + r0, hq), sl] = (num / den).astype(BF16)
                lse_pair = jnp.where(lane_lo, m[0:hq], m[hq:2 * hq]) + jnp.log(den)
                lse_tile = jnp.where(lane_pair == hp, lse_pair, lse_tile)
            ls[c, pl.ds(q0 + r0, hq), :] = lse_tile
        return carry

    lax.fori_loop(0, r * nsub, unit, 0, unroll=2)

    if r == 1:
        o_ref[...] = os_[0]
        lse_ref[...] = ls[0]
    else:
        n = PERM // r
        for b in range(r * tl // PERM):
            cat = jnp.concatenate([os_[c, b * n:(b + 1) * n, :] for c in range(r)], axis=0)
            o_ref[b * PERM:(b + 1) * PERM, :] = jnp.dot(pt_ref[...], cat, preferred_element_type=F32).astype(BF16)
        for c in range(r):
            lse_ref[pl.ds(c, tl, stride=r), :] = ls[c]


def _attention_group(q, k, v, perm_t, r):
    b, _, l, _ = q.shape
    tl = min(ATT_CLASS_ROWS[r], l)
    ts = r * tl
    nh = tl // BAND
    last = l // BAND - 1
    cur = pl.BlockSpec((None, r, tl, ATT_WIDTH), lambda bi, i: (bi, 0, i, 0))
    prv = pl.BlockSpec((None, r, BAND, ATT_WIDTH), lambda bi, i: (bi, 0, jnp.maximum(i * nh - 1, 0), 0))
    nxt = pl.BlockSpec((None, r, BAND, ATT_WIDTH), lambda bi, i: (bi, 0, jnp.minimum((i + 1) * nh, last), 0))
    body = functools.partial(_attn_body, r=r, tl=tl, seq_l=l)
    return pl.pallas_call(
        body,
        grid=(b, l // tl),
        in_specs=[cur, prv, cur, nxt, prv, cur, nxt, pl.BlockSpec((PERM, PERM), lambda bi, i: (0, 0))],
        out_specs=[pl.BlockSpec((None, ts, ATT_WIDTH), lambda bi, i: (bi, i, 0)),
                   pl.BlockSpec((None, ts, 128), lambda bi, i: (bi, i, 0))],
        out_shape=[jax.ShapeDtypeStruct((b, r * l, ATT_WIDTH), BF16), jax.ShapeDtypeStruct((b, r * l, 128), F32)],
        scratch_shapes=[
            pltpu.VMEM((r, tl + 2 * BAND, ATT_WIDTH), BF16),
            pltpu.VMEM((r, tl + 2 * BAND, ATT_WIDTH), BF16),
            pltpu.VMEM((r, tl, ATT_WIDTH), BF16),
            pltpu.VMEM((r, tl, 128), F32),
        ],
        compiler_params=_cparams(("parallel", "parallel")),
        name=f"attn_r{r}",
    )(q, k, k, k, v, v, v, perm_t)


def _hgrn_chunk_fn(z_ref, q_ref, i_ref, lb_ref, lb_row, wsel_ref, o_ref, st_ref, rev):
    ch, sb = HGRN_CHUNK, HGRN_SUB
    row = lax.broadcasted_iota(jnp.int32, (ch, ch), 0)
    col = lax.broadcasted_iota(jnp.int32, (ch, ch), 1)
    dist = (col - row) if rev else (row - col)
    same = lambda n: (row // n) == (col // n)
    diag_mask = same(sb) & (dist >= 0)
    level_sizes = tuple(n for n in (1, 2, 4, 8, 16, 32) if n >= sb)
    level_masks = [same(2 * n) & jnp.logical_not(same(n)) & (dist > 0) for n in level_sizes]
    sl8 = V7X_SUBLANES
    row8 = lax.broadcasted_iota(jnp.int32, (sl8, HGRN_DK), 0)
    sub = lax.broadcasted_iota(jnp.int32, (ch, HGRN_DK), 0) % sb
    tau = (sb - 1 - sub) if rev else sub
    nt = (((1,), (1,)), ((), ()))
    tn = (((0,), (0,)), ((), ()))

    def roll8(x, k):
        return pltpu.roll(x.reshape(ch // sl8, sl8, HGRN_DK), k % sl8, 1).reshape(ch, HGRN_DK)

    prev = lambda x, k: roll8(x, -k if rev else k)
    nxt = lambda x, k: roll8(x, k if rev else -k)
    last_row = lambda start, n: start if rev else start + n - 1

    def widen(pf, sf, n):
        later_first = rev
        pf_rows, sf_rows = [], []
        for v in range(ch // sl8):
            x, y = pf[sl8 * v:sl8 * (v + 1)], sf[sl8 * v:sl8 * (v + 1)]
            if n >= sl8:
                blk = (sl8 * v) // n
                first = blk % 2 == 0
                sib = (blk + 1) * n if first else (blk - 1) * n
                r = last_row(sib, n)
                total = jnp.broadcast_to(pf[r:r + 1, :], (sl8, HGRN_DK))
                if first == later_first:
                    x = x * total
                else:
                    y = y * total
            else:
                pmul = jnp.ones((sl8, HGRN_DK), F32)
                smul = pmul
                for b0 in range(0, sl8, 2 * n):
                    lo_tot = jnp.broadcast_to(x[last_row(b0, n):last_row(b0, n) + 1, :], (sl8, HGRN_DK))
                    hi_tot = jnp.broadcast_to(x[last_row(b0 + n, n):last_row(b0 + n, n) + 1, :], (sl8, HGRN_DK))
                    in_lo = (row8 >= b0) & (row8 < b0 + n)
                    in_hi = (row8 >= b0 + n) & (row8 < b0 + 2 * n)
                    if later_first:
                        pmul, smul = jnp.where(in_lo, hi_tot, pmul), jnp.where(in_hi, lo_tot, smul)
                    else:
                        pmul, smul = jnp.where(in_hi, lo_tot, pmul), jnp.where(in_lo, hi_tot, smul)
                x, y = x * pmul, y * smul
            pf_rows.append(x)
            sf_rows.append(y)
        return jnp.concatenate(pf_rows, axis=0), jnp.concatenate(sf_rows, axis=0)

    def chunk(c):
        r0 = c * ch if isinstance(c, int) else pl.multiple_of(c * ch, ch)
        heads = range(HGRN_HEADS)
        sls = [slice(h * HGRN_DK, (h + 1) * HGRN_DK) for h in heads]
        qs, ivs, kfs, pfs, sfs, bands = [], [], [], [], [], []
        for sl in sls:
            z = z_ref[pl.ds(r0, ch), sl]
            q = q_ref[pl.ds(r0, ch), sl].astype(F32)
            lb = lb_ref[lb_row:lb_row + 1, sl]
            one_m_lb = 1.0 - lb
            e = jnp.exp(-jnp.abs(z))
            inv = 1.0 / (1.0 + e)
            pos = z >= 0
            f = lb + one_m_lb * (jnp.where(pos, 1.0, e) * inv)
            kf = one_m_lb * (jnp.where(pos, e, 1.0) * inv)
            pf = f
            sf = jnp.where(tau == sb - 1, 1.0, nxt(f, 1))
            for k in (k for k in (1, 2, 4) if k < sb):
                pf = pf * jnp.where(tau >= k, prev(pf, k), 1.0)
                sf = sf * jnp.where(tau <= sb - 1 - k, nxt(sf, k), 1.0)
            g = kf
            terms = [q * g]
            for d in range(1, sb):
                g = prev(g, 1) * f
                terms.append(q * g)
            qs.append(q); ivs.append(i_ref[pl.ds(r0, ch), sl]); kfs.append(kf); pfs.append(pf); sfs.append(sf)
            bands.append(jnp.concatenate(terms, axis=1).astype(BF16))

        bands = [jnp.dot(x, wsel_ref[...], preferred_element_type=F32) for x in bands]

        levels = []
        for n in level_sizes:
            levels.append([lax.dot_general((qs[h] * pfs[h]).astype(BF16), (kfs[h] * sfs[h]).astype(BF16), nt,
                                           preferred_element_type=F32) for h in heads])
            for h in heads:
                pfs[h], sfs[h] = widen(pfs[h], sfs[h], n)
        sts = [st_ref[h] for h in heads]
        inter = [lax.dot_general((qs[h] * pfs[h]).astype(BF16), sts[h].astype(BF16), nt, preferred_element_type=F32)
                 for h in heads]
        upd = [lax.dot_general(ivs[h], (kfs[h] * sfs[h]).astype(BF16), tn, preferred_element_type=F32)
               for h in heads]

        r_end = last_row(0, ch)
        for h in heads:
            skew = pltpu.roll(bands[h], 0, 1, stride=1, stride_axis=0)
            a = jnp.where(diag_mask, skew[:, 0:ch], 0.0)
            for lvl, mask in zip(levels, level_masks):
                a = jnp.where(mask, lvl[h], a)
            o = jnp.dot(a.astype(BF16), ivs[h], preferred_element_type=F32)
            o_ref[pl.ds(r0, ch), sls[h]] = (o + inter[h]).astype(o_ref.dtype)
            st_ref[h] = sts[h] * pfs[h][r_end:r_end + 1, :] + upd[h]

    return chunk


def _hgrn_body(z_ref, q_ref, i_ref, lb_ref, wsel_ref, o_ref, st_ref, *, rev, nch):
    @pl.when(pl.program_id(1) == 0)
    def _():
        st_ref[...] = jnp.zeros_like(st_ref)

    chunk = _hgrn_chunk_fn(z_ref, q_ref, i_ref, lb_ref, 0, wsel_ref, o_ref, st_ref, rev)

    def step(ci, carry):
        chunk((nch - 1 - ci) if rev else ci)
        return carry

    lax.fori_loop(0, nch, step, 0, unroll=4)


def _band_selector(rev):
    w = np.zeros((HGRN_SUB * HGRN_DK, 128), np.float32)
    for d in range(HGRN_SUB):
        w[d * HGRN_DK:(d + 1) * HGRN_DK, d if rev else (128 - d) % 128] = 1.0
    return jnp.asarray(w, BF16)


def _hgrn_scan(z, bd, lb, rev):
    b, s, _ = z.shape
    ts = HGRN_TILE
    nb = s // ts
    order = (lambda j: nb - 1 - j) if rev else (lambda j: j)
    spec = lambda col: pl.BlockSpec((None, ts, HGRN_WIDTH), lambda bi, j: (bi, order(j), col))
    const = lambda shape: pl.BlockSpec(shape, lambda bi, j: (0, 0))
    body = functools.partial(_hgrn_body, rev=rev, nch=ts // HGRN_CHUNK)
    return pl.pallas_call(
        body,
        grid=(b, nb),
        in_specs=[spec(1 if rev else 0), spec(BD_Q), spec(BD_I), const((1, HGRN_WIDTH)),
                  const((HGRN_SUB * HGRN_DK, 128))],
        out_specs=spec(0),
        out_shape=jax.ShapeDtypeStruct((b, s, HGRN_WIDTH), BF16),
        scratch_shapes=[pltpu.VMEM((HGRN_HEADS, HGRN_DK, HGRN_DK), F32)],
        compiler_params=_cparams(("parallel", "arbitrary")),
        name="hgrn_bwd" if rev else "hgrn_fwd",
    )(z, bd, bd, lb, _band_selector(rev))


def _fnet_a_body(u_ref, cs_ref, m_ref, o_ref, *, n1, nt):
    ngrp = FNET_WIDTH // FNET_GROUP_DIM
    for s in range(nt):
        zr, zi = [], []
        for g in range(ngrp):
            c0 = s * FNET_WIDTH + g * FNET_GROUP_DIM
            zz = jnp.dot(u_ref[:, c0:c0 + FNET_GROUP_DIM], cs_ref[...], preferred_element_type=F32)
            zr.append(zz[:, :FNET_GROUP_DIM])
            zi.append(zz[:, FNET_GROUP_DIM:])
        zcat = jnp.concatenate([jnp.concatenate(zr, axis=1), jnp.concatenate(zi, axis=1)], axis=0).astype(BF16)
        res = jnp.dot(m_ref[s], zcat, preferred_element_type=F32)
        o_ref[:, s * FNET_WIDTH:(s + 1) * FNET_WIDTH] = res.astype(o_ref.dtype)


def _fnet_c_body(b_ref, cs_ref, o_ref, *, kt):
    for k in range(kt):
        bcat = jnp.concatenate([b_ref[0, k], b_ref[1, k]], axis=0)
        x = jnp.dot(cs_ref[...], bcat, preferred_element_type=F32)
        o_ref[:, k * FNET_WIDTH:(k + 1) * FNET_WIDTH] = x.astype(o_ref.dtype)


def _fnet_tables(n1, n2):
    n = n1 * n2
    k = np.arange(FNET_GROUP_DIM)
    ang = 2.0 * np.pi * ((k[:, None] * k[None, :]) % FNET_GROUP_DIM) / FNET_GROUP_DIM
    cs_ch = np.concatenate([np.cos(ang), -np.sin(ang)], axis=1) / np.sqrt(FNET_GROUP_DIM)
    k1 = np.arange(n1)[None, :, None]
    m1 = np.arange(n1)[None, None, :]
    s2 = np.arange(n2)[:, None, None]
    idx = (k1 * m1 * n2 + s2 * k1) % n
    th = 2.0 * np.pi * idx / n
    gc, gs = np.cos(th) / np.sqrt(n1), np.sin(th) / np.sqrt(n1)
    m = np.concatenate([np.concatenate([gc, gs], axis=2), np.concatenate([-gs, gc], axis=2)], axis=1)
    k2 = np.arange(n2)
    ang2 = 2.0 * np.pi * ((k2[:, None] * k2[None, :]) % n2) / n2
    cs_seq = np.concatenate([np.cos(ang2), np.sin(ang2)], axis=1) / np.sqrt(n2)
    return (jnp.asarray(cs_ch, BF16), jnp.asarray(m, BF16), jnp.asarray(cs_seq, BF16))


def _fourier_branch(u, n1, n2):
    b, s, _ = u.shape
    cs_ch, m, cs_seq = _fnet_tables(n1, n2)
    nt = FNET_SLABS_PER_STEP
    kt = FNET_ROWS_PER_STEP
    u2 = u.reshape(b, n1, n2 * FNET_WIDTH)
    a = pl.pallas_call(
        functools.partial(_fnet_a_body, n1=n1, nt=nt),
        grid=(b, n2 // nt),
        in_specs=[
            pl.BlockSpec((None, n1, nt * FNET_WIDTH), lambda bi, j: (bi, 0, j)),
            pl.BlockSpec((FNET_GROUP_DIM, 2 * FNET_GROUP_DIM), lambda bi, j: (0, 0)),
            pl.BlockSpec((nt, 2 * n1, 2 * n1), lambda bi, j: (j, 0, 0)),
        ],
        out_specs=pl.BlockSpec((None, 2 * n1, nt * FNET_WIDTH), lambda bi, j: (bi, 0, j)),
        out_shape=jax.ShapeDtypeStruct((b, 2 * n1, n2 * FNET_WIDTH), BF16),
        compiler_params=_cparams(("parallel", "parallel")),
        name="fnet_a",
    )(u2, cs_ch, m)
    a5 = a.reshape(b, 2, n1, n2, FNET_WIDTH)
    y = pl.pallas_call(
        functools.partial(_fnet_c_body, kt=kt),
        grid=(b, n1 // kt),
        in_specs=[
            pl.BlockSpec((None, 2, kt, n2, FNET_WIDTH), lambda bi, j: (bi, 0, j, 0, 0)),
            pl.BlockSpec((n2, 2 * n2), lambda bi, j: (0, 0)),
        ],
        out_specs=pl.BlockSpec((None, n2, kt * FNET_WIDTH), lambda bi, j: (bi, 0, j)),
        out_shape=jax.ShapeDtypeStruct((b, n2, n1 * FNET_WIDTH), BF16),
        compiler_params=_cparams(("parallel", "parallel")),
        name="fnet_c",
    )(a5, cs_seq)
    return y.reshape(b, s, FNET_WIDTH)


def _merge_body(x_ref, o0_ref, o1_ref, o2_ref, l0_ref, l1_ref, l2_ref, hf_ref, hb_ref, gh_ref, y_ref,
                ga_ref, gb_ref, gc_ref, wa_ref, wh_ref, wf_ref, wo_ref, hg_ref, ex_ref, out_ref):
    lses = [l0_ref[...], l1_ref[...], l2_ref[...]]
    m = jnp.maximum(jnp.maximum(lses[0], lses[1]), lses[2])
    ws = [jnp.exp(l - m) for l in lses]
    inv = 1.0 / (ws[0] + ws[1] + ws[2])
    att = None
    for w, o_ref in zip(ws, (o0_ref, o1_ref, o2_ref)):
        alpha = jnp.dot((w * inv).astype(BF16), ex_ref[...], preferred_element_type=F32)
        term = alpha * o_ref[...].astype(F32)
        att = term if att is None else att + term
    a = jnp.dot(att.astype(BF16), wa_ref[...], preferred_element_type=F32)
    oh = hf_ref[...].astype(F32) + hb_ref[...].astype(F32)
    gh = gh_ref[...].astype(F32)
    heads = []
    for h in range(HGRN_HEADS):
        sl = slice(h * HGRN_DK, (h + 1) * HGRN_DK)
        gate = gh[:, sl]
        heads.append(_rms_rows(oh[:, sl], hg_ref[...]) * (gate * jax.nn.sigmoid(gate)))
    bh = jnp.dot(jnp.concatenate(heads, axis=1).astype(BF16), wh_ref[...], preferred_element_type=F32)
    c = jnp.dot(y_ref[...], wf_ref[...], preferred_element_type=F32)
    sig = lambda ref: jax.nn.sigmoid(ref[...].astype(F32))
    merged = sig(ga_ref) * a + sig(gb_ref) * bh + sig(gc_ref) * c
    out_ref[...] = x_ref[...] + jnp.dot(merged.astype(BF16), wo_ref[...], preferred_element_type=F32)


def _merge(x2d, att_outs, att_lses, hf, hb, bd, y, wa, wh, wf, wo, hgain, expand, l):
    t, tm = x2d.shape[0], TOKEN_TILE
    row = lambda w: pl.BlockSpec((tm, w), lambda i: (i, 0))
    gate = lambda k: pl.BlockSpec((tm, D_MODEL), lambda i: (i, k))
    in_specs = ([row(D_MODEL)] + [row(ATT_WIDTH)] * 3 + [row(128)] * 3 + [row(HGRN_WIDTH)] * 2
                + [pl.BlockSpec((tm, HGRN_WIDTH), lambda i: (i, BD_G)),
                   row(FNET_WIDTH), gate(0), gate(1), gate(2),
                   _of_layer(l, ATT_WIDTH, D_MODEL), _of_layer(l, HGRN_WIDTH, D_MODEL),
                   _of_layer(l, FNET_WIDTH, D_MODEL), _of_layer(l, D_MODEL, D_MODEL), _of_layer(l, 1, HGRN_DK),
                   pl.BlockSpec((128, ATT_WIDTH), lambda i: (0, 0))])
    return pl.pallas_call(
        _merge_body,
        grid=(t // tm,),
        in_specs=in_specs,
        out_specs=row(D_MODEL),
        out_shape=jax.ShapeDtypeStruct((t, D_MODEL), F32),
        compiler_params=_cparams(("parallel",)),
        name="merge",
    )(x2d, *att_outs, *att_lses, hf, hb, bd, y, bd, bd, bd, wa, wh, wf, wo, hgain, expand)


def _trunk_consts(seq):
    half = HEAD_DIM // 2
    inv_freq = ROPE_THETA ** (-jnp.arange(half, dtype=F32) / half)
    ang = jnp.arange(seq).astype(F32)[:, None] * inv_freq[None, :]
    cos_nat, sin_nat = jnp.cos(ang), jnp.sin(ang)
    rope = {}
    for _, r in WINDOW_DILATIONS:
        order = lambda tab: tab.reshape(seq // PERM, PERM // r, r, half).transpose(0, 2, 1, 3).reshape(seq, half)
        cos, sin = order(cos_nat), order(sin_nat)
        rope[r] = (jnp.concatenate([cos, cos, cos, cos], axis=1),
                   jnp.concatenate([-sin, sin, -sin, sin], axis=1))
    hd = np.arange(256) // HEAD_DIM
    head_bd = jnp.asarray((hd[:, None] == hd[None, :]) / HEAD_DIM, BF16)
    perm, perm_t = {}, {}
    for _, r in WINDOW_DILATIONS:
        idx = np.arange(PERM)
        n = PERM // r
        p = np.zeros((PERM, PERM), np.float32)
        p[idx, (idx % n) * r + idx // n] = 1.0
        perm[r], perm_t[r] = jnp.asarray(p, BF16), jnp.asarray(p.T, BF16)
    expand = np.zeros((128, ATT_WIDTH), np.float32)
    for h in range(ATT_HEADS):
        expand[_lse_lane(h), h * HEAD_DIM:(h + 1) * HEAD_DIM] = 1.0
    return dict(rope=rope, head_bd=head_bd, perm=perm, perm_t=perm_t, expand=jnp.asarray(expand, BF16), n1=128)


def _stacked_weights(hgrn_lb_logits, ffn1_norm, ffn1_w_gate, ffn1_w_up, ffn1_w_down, mix_norm, w_in, q_norm,
                     k_norm, w_att_out, hgrn_out_norm, w_hgrn_out, w_fnet_out, w_out, ffn2_norm, ffn2_w_gate,
                     ffn2_w_up, ffn2_w_down):
    bf = lambda w: w.astype(BF16)
    vec = lambda v: v.astype(F32)[:, None, :]
    heads_per_tile = ATT_WIDTH // HEAD_DIM
    p = jax.nn.softmax(hgrn_lb_logits.astype(F32), axis=0)
    cum = jnp.cumsum(p, axis=0)
    return dict(
        ffn1=(vec(ffn1_norm), bf(ffn1_w_gate), bf(ffn1_w_up), bf(ffn1_w_down)),
        ffn2=(vec(ffn2_norm), bf(ffn2_w_gate), bf(ffn2_w_up), bf(ffn2_w_down)),
        mix_norm=vec(mix_norm), w_in=bf(w_in),
        gq=vec(jnp.tile(q_norm, (1, heads_per_tile))), gk=vec(jnp.tile(k_norm, (1, heads_per_tile))),
        lb=cum - cum[0:1],
        wa=bf(w_att_out), wh=bf(w_hgrn_out), wf=bf(w_fnet_out), wo=bf(w_out), hgain=vec(hgrn_out_norm),
    )


def _encoder_layer(x, wts, consts, l):
    b, s, _ = x.shape
    t = b * s
    x2d = x.reshape(t, D_MODEL)
    x2d = _ffn(x2d, *wts["ffn1"], l)
    z, bd, u = _proj_rest(x2d, wts["mix_norm"], wts["w_in"], l)
    z3, bd3 = z.reshape(b, s, Z_COLS), bd.reshape(b, s, BD_COLS)
    qkv = _proj_qkv(x2d, wts["mix_norm"], wts["w_in"], wts["gq"], wts["gk"], consts, b, s, l)
    hf = _hgrn_scan(z3, bd3, wts["lb"][l, 0].reshape(1, -1), False).reshape(t, HGRN_WIDTH)
    hb = _hgrn_scan(z3, bd3, wts["lb"][l, 1].reshape(1, -1), True).reshape(t, HGRN_WIDTH)
    outs, lses = [], []
    for gi, (w, r) in enumerate(WINDOW_DILATIONS):
        assert w // (2 * r) == BAND
        o, lse = _attention_group(qkv[3 * gi], qkv[3 * gi + 1], qkv[3 * gi + 2], consts["perm_t"][r], r)
        outs.append(o.reshape(t, ATT_WIDTH))
        lses.append(lse.reshape(t, 128))
    y = _fourier_branch(u.reshape(b, s, FNET_WIDTH), consts["n1"], s // consts["n1"]).reshape(t, FNET_WIDTH)
    x2d = _merge(x2d, outs, lses, hf, hb, bd, y, wts["wa"], wts["wh"], wts["wf"], wts["wo"], wts["hgain"],
                 consts["expand"], l)
    x2d = _ffn(x2d, *wts["ffn2"], l)
    return x2d.reshape(b, s, D_MODEL)


def kernel(x_prompt, x_sample, hgrn_lb_logits, ffn1_norm, ffn1_w_gate, ffn1_w_up, ffn1_w_down, mix_norm, w_in,
           q_norm, k_norm, w_att_out, hgrn_out_norm, w_hgrn_out, w_fnet_out, w_out, ffn2_norm, ffn2_w_gate,
           ffn2_w_up, ffn2_w_down):
    wts = _stacked_weights(hgrn_lb_logits, ffn1_norm, ffn1_w_gate, ffn1_w_up, ffn1_w_down, mix_norm, w_in, q_norm,
                           k_norm, w_att_out, hgrn_out_norm, w_hgrn_out, w_fnet_out, w_out, ffn2_norm,
                           ffn2_w_gate, ffn2_w_up, ffn2_w_down)
    consts = _trunk_consts(max(x_prompt.shape[1], x_sample.shape[1]))

    def trunk(x):
        for l in range(w_in.shape[0]):
            x = _encoder_layer(x, wts, consts, l)
        return x

    return (trunk(x_prompt), trunk(x_sample))
```

```python
import functools

import numpy as np
import jax
import jax.numpy as jnp
from jax import lax
from jax.experimental import pallas as pl
from jax.experimental.pallas import tpu as pltpu

F32 = jnp.float32
BF16 = jnp.bfloat16

D_MODEL = 1024
D_FF = 2816
HEAD_DIM = 64
ATT_HEADS = 8
WINDOW_DILATIONS = ((128, 1), (512, 4), (2048, 16))
N_GROUPS = len(WINDOW_DILATIONS)
ATT_WIDTH = ATT_HEADS * HEAD_DIM
ATT_COLS = N_GROUPS * ATT_WIDTH
ROPE_THETA = 10000.0
HGRN_HEADS = 4
HGRN_DK = 128
HGRN_WIDTH = 512
HGRN_CHUNK = 64
HGRN_SUB = 4
V7X_SUBLANES = 8
FNET_WIDTH = 512
FNET_GROUP_DIM = 128
EPS = 1e-6
NEG_INF = -1e30
IN_COLS = 10752
BAND = 64
PERM = 256

QKV_COLS = 3 * ATT_COLS
Z_COLS = 2 * HGRN_WIDTH
GATE_COLS = 3 * D_MODEL
BD_COLS = GATE_COLS + 3 * HGRN_WIDTH
BD_Q, BD_I, BD_G = (GATE_COLS // HGRN_WIDTH + k for k in range(3))

V7X_LANES = 128
V7X_MXU_DIM = 256
V7X_VMEM_LIMIT_BYTES = 56 * 1024 * 1024
PROJ_CHUNK = 512
ATT_UNIT_ROWS = 128
LSE_LANES_PER_HEAD = V7X_LANES // ATT_HEADS
FNET_SLOW_LEN = 128
FF_CHUNKS = ((0, 768), (768, 1536), (1536, 2304), (2304, 2816))
FFN_TOKEN_TILE = 1024
TOKEN_TILE = 512
ATT_CLASS_ROWS = {1: 2048, 4: 512, 16: 128}
HGRN_TILE = 1024
FNET_SLABS_PER_STEP = 8
FNET_ROWS_PER_STEP = 32
ATT_ROWS_PER_TILE = 64


def _cparams(sem):
    return pltpu.CompilerParams(dimension_semantics=sem, vmem_limit_bytes=V7X_VMEM_LIMIT_BYTES)


def _rms_rows(x, g):
    ms = jnp.mean(x * x, axis=-1, keepdims=True)
    return x * lax.rsqrt(ms + EPS) * g


def _ffn_body(x_ref, g_ref, wg_ref, wu_ref, wd_ref, o_ref):
    x = x_ref[...]
    h = _rms_rows(x, g_ref[...]).astype(BF16)
    acc = None
    for lo, hi in FF_CHUNKS:
        gate = jnp.dot(h, wg_ref[:, lo:hi], preferred_element_type=F32)
        up = jnp.dot(h, wu_ref[:, lo:hi], preferred_element_type=F32)
        a = (gate * jax.nn.sigmoid(gate) * up).astype(BF16)
        part = jnp.dot(a, wd_ref[lo:hi, :], preferred_element_type=F32)
        acc = part if acc is None else acc + part
    o_ref[...] = x + 0.5 * acc


def _of_layer(l, rows, cols, **kw):
    return pl.BlockSpec((None, rows, cols), lambda *_: (l, 0, 0), **kw)


def _ffn(x2d, g, wg, wu, wd, l):
    t, tm = x2d.shape[0], FFN_TOKEN_TILE
    one = pl.Buffered(1)
    return pl.pallas_call(
        _ffn_body,
        grid=(t // tm,),
        in_specs=[
            pl.BlockSpec((tm, D_MODEL), lambda i: (i, 0)),
            _of_layer(l, 1, D_MODEL),
            _of_layer(l, D_MODEL, D_FF, pipeline_mode=one),
            _of_layer(l, D_MODEL, D_FF, pipeline_mode=one),
            _of_layer(l, D_FF, D_MODEL, pipeline_mode=one),
        ],
        out_specs=pl.BlockSpec((tm, D_MODEL), lambda i: (i, 0)),
        out_shape=jax.ShapeDtypeStruct((t, D_MODEL), F32),
        compiler_params=_cparams(("parallel",)),
        name="ffn",
    )(x2d, g, wg, wu, wd)


def _head_norm_rope(acc, gain, cos, sin, bd_ref):
    tm = acc.shape[0]
    ln, mx, half = V7X_LANES, V7X_MXU_DIM, HEAD_DIM // 2
    lane = lax.broadcasted_iota(jnp.int32, (tm, ln), 1)
    first_half = (lane % HEAD_DIM) < half
    slabs = []
    for s in range(ATT_WIDTH // mx):
        a = acc[:, s * mx:(s + 1) * mx]
        ms = jnp.dot((a * a).astype(BF16), bd_ref[...], preferred_element_type=F32)
        y = a * lax.rsqrt(ms + EPS) * gain[:, s * mx:(s + 1) * mx]
        for t in range(mx // ln):
            y1 = y[:, t * ln:(t + 1) * ln]
            yr = jnp.where(first_half, pltpu.roll(y1, ln - half, 1), pltpu.roll(y1, half, 1))
            slabs.append(y1 * cos + yr * sin)
    return jnp.concatenate(slabs, axis=1)


def _qkv_pieces(x_ref, g_ref, w_ref, gq_ref, gk_ref, table_refs, bd_ref, p4_ref, p16_ref, out_refs):
    tm = x_ref.shape[0]
    h = _rms_rows(x_ref[...], g_ref[...]).astype(BF16)

    def permuted(p_ref):
        blocks = [jnp.dot(p_ref[...], h[b * PERM:(b + 1) * PERM], preferred_element_type=F32).astype(BF16)
                  for b in range(tm // PERM)]
        return jnp.concatenate(blocks, axis=0)

    lhs = (h, permuted(p4_ref), permuted(p16_ref))
    gains = (gq_ref[...] * HEAD_DIM ** -0.5, gk_ref[...])

    def lanes128(c_ref, s_ref):
        c, s = c_ref[...], s_ref[...]
        return jnp.concatenate([c, c, c, c], axis=1), jnp.concatenate([-s, s, -s, s], axis=1)

    tables = [lanes128(table_refs[2 * gi], table_refs[2 * gi + 1]) for gi in range(N_GROUPS)]

    def piece(gi, ti):
        def run():
            r = WINDOW_DILATIONS[gi][1]
            n = PERM // r
            col = ti * ATT_COLS + gi * ATT_WIDTH
            y = jnp.dot(lhs[gi], w_ref[:, col:col + ATT_WIDTH], preferred_element_type=F32)
            if ti < 2:
                y = _head_norm_rope(y, gains[ti], *tables[gi], bd_ref)
            y = y.astype(BF16)
            o_ref = out_refs[3 * gi + ti]
            if r == 1:
                o_ref[0] = y
            else:
                for b in range(tm // PERM):
                    for c in range(r):
                        o_ref[c, b * n:(b + 1) * n, :] = y[b * PERM + c * n:b * PERM + (c + 1) * n, :]
        return run

    return [piece(gi, ti) for gi in range(N_GROUPS) for ti in range(3)]


def _proj_qkv_body(x_ref, g_ref, w_ref, gq_ref, gk_ref, c1_ref, s1_ref, c4_ref, s4_ref, c16_ref, s16_ref,
                   bd_ref, p4_ref, p16_ref, *out_refs):
    tables = (c1_ref, s1_ref, c4_ref, s4_ref, c16_ref, s16_ref)
    for run in _qkv_pieces(x_ref, g_ref, w_ref, gq_ref, gk_ref, tables, bd_ref, p4_ref, p16_ref, out_refs):
        run()


def _proj_rest_body(x_ref, g_ref, w_ref, z_ref, bd_ref, u_ref):
    h = _rms_rows(x_ref[...], g_ref[...]).astype(BF16)
    tn = PROJ_CHUNK
    proj = lambda c0: jnp.dot(h, w_ref[:, c0:c0 + tn], preferred_element_type=F32)
    for k in range(Z_COLS // tn):
        z_ref[:, k * tn:(k + 1) * tn] = proj(QKV_COLS + k * tn)
    hgrn0 = QKV_COLS + Z_COLS
    u0 = hgrn0 + 3 * HGRN_WIDTH
    gate0 = u0 + FNET_WIDTH
    for k in range(GATE_COLS // tn):
        bd_ref[:, k * tn:(k + 1) * tn] = proj(gate0 + k * tn).astype(BF16)
    for k in range(3 * HGRN_WIDTH // tn):
        bd_ref[:, GATE_COLS + k * tn:GATE_COLS + (k + 1) * tn] = proj(hgrn0 + k * tn).astype(BF16)
    u_ref[...] = proj(u0).astype(BF16)


def _proj_qkv(x2d, g, w_in, gq, gk, consts, batch, seq, l):
    t, tm = x2d.shape[0], TOKEN_TILE
    nsb = seq // tm
    const = lambda i: (0, 0)
    tab = pl.BlockSpec((tm, HEAD_DIM // 2), lambda i: (i % nsb, 0))
    in_specs = [
        pl.BlockSpec((tm, D_MODEL), lambda i: (i, 0)),
        _of_layer(l, 1, D_MODEL),
        _of_layer(l, D_MODEL, IN_COLS, pipeline_mode=pl.Buffered(1)),
        _of_layer(l, 1, ATT_WIDTH), _of_layer(l, 1, ATT_WIDTH),
        tab, tab, tab, tab, tab, tab,
        pl.BlockSpec((V7X_MXU_DIM, V7X_MXU_DIM), const),
        pl.BlockSpec((PERM, PERM), const), pl.BlockSpec((PERM, PERM), const),
    ]
    out_specs, out_shape = [], []
    for _, r in WINDOW_DILATIONS:
        for _ in range(3):
            out_specs.append(pl.BlockSpec((None, r, tm // r, ATT_WIDTH), lambda i: (i // nsb, 0, i % nsb, 0)))
            out_shape.append(jax.ShapeDtypeStruct((batch, r, seq // r, ATT_WIDTH), BF16))
    rope = consts["rope"]
    return pl.pallas_call(
        _proj_qkv_body,
        grid=(t // tm,),
        in_specs=in_specs,
        out_specs=out_specs,
        out_shape=out_shape,
        compiler_params=_cparams(("parallel",)),
        name="proj_qkv",
    )(x2d, g, w_in, gq, gk, rope[1][0], rope[1][1], rope[4][0], rope[4][1], rope[16][0], rope[16][1],
      consts["head_bd"], consts["perm"][4], consts["perm"][16])


def _proj_rest(x2d, g, w_in, l):
    t, tm = x2d.shape[0], TOKEN_TILE
    row = lambda w: pl.BlockSpec((tm, w), lambda i: (i, 0))
    return pl.pallas_call(
        _proj_rest_body,
        grid=(t // tm,),
        in_specs=[row(D_MODEL), _of_layer(l, 1, D_MODEL),
                  _of_layer(l, D_MODEL, IN_COLS, pipeline_mode=pl.Buffered(1))],
        out_specs=[row(Z_COLS), row(BD_COLS), row(FNET_WIDTH)],
        out_shape=[jax.ShapeDtypeStruct((t, Z_COLS), F32), jax.ShapeDtypeStruct((t, BD_COLS), BF16),
                   jax.ShapeDtypeStruct((t, FNET_WIDTH), BF16)],
        compiler_params=_cparams(("parallel",)),
        name="proj_rest",
    )(x2d, g, w_in)


def _lse_lane(h):
    return HEAD_DIM * (h % 2) + LSE_LANES_PER_HEAD * (h // 2)


def _attn_body(q_ref, kp_ref, kc_ref, kn_ref, vp_ref, vc_ref, vn_ref, pt_ref, o_ref, lse_ref,
               ks, vs, os_, ls, *, r, tl, seq_l):
    blk = pl.program_id(1)
    ln = V7X_LANES
    sq, kw = ATT_UNIT_ROWS, ATT_UNIT_ROWS + 2 * BAND
    ks[:, 0:BAND, :] = kp_ref[...]
    ks[:, BAND:BAND + tl, :] = kc_ref[...]
    ks[:, BAND + tl:2 * BAND + tl, :] = kn_ref[...]
    vs[:, 0:BAND, :] = vp_ref[...]
    vs[:, BAND:BAND + tl, :] = vc_ref[...]
    vs[:, BAND + tl:2 * BAND + tl, :] = vn_ref[...]

    nsub = tl // sq
    hq = ATT_ROWS_PER_TILE
    qi = lax.broadcasted_iota(jnp.int32, (sq, kw), 0)
    ki = lax.broadcasted_iota(jnp.int32, (sq, kw), 1)
    in_band = jnp.abs(ki - BAND - qi) <= BAND
    lane = lax.broadcasted_iota(jnp.int32, (hq, ln), 1)
    lane_lo = lane < HEAD_DIM
    lane_pair = (lane % HEAD_DIM) // LSE_LANES_PER_HEAD
    lane_k = lax.broadcasted_iota(jnp.int32, (kw, ln), 1) < HEAD_DIM
    nt = (((1,), (1,)), ((), ()))

    def unit(u, carry):
        c = u // nsub
        q0 = pl.multiple_of((u % nsub) * sq, sq)
        kpos = blk * tl + q0 - BAND + ki
        valid = in_band & (kpos >= 0) & (kpos < seq_l)
        q2 = q_ref[c, pl.ds(q0, sq), :]
        k2 = ks[c, pl.ds(q0, kw), :]
        v2 = vs[c, pl.ds(q0, kw), :]
        for half in range(sq // hq):
            r0, r1 = half * hq, (half + 1) * hq
            vmask = jnp.concatenate([valid[r0:r1], valid[r0:r1]], axis=0)
            lse_tile = jnp.zeros((hq, ln), F32)
            for hp in range(ATT_HEADS // 2):
                sl = slice(hp * ln, (hp + 1) * ln)
                q128, k128, v128 = q2[r0:r1, sl], k2[:, sl], v2[:, sl]
                zero = jnp.zeros_like(q128)
                qstack = jnp.concatenate([jnp.where(lane_lo, q128, zero), jnp.where(lane_lo, zero, q128)], axis=0)
                s = lax.dot_general(qstack, k128, nt, preferred_element_type=F32)
                s = jnp.where(vmask, s, NEG_INF)
                m = jnp.max(s, axis=-1, keepdims=True)
                p = jnp.exp(s - m).astype(BF16)
                one = jnp.ones_like(v128)
                pv0 = jnp.dot(p[0:hq], jnp.where(lane_k, v128, one), preferred_element_type=F32)
                pv1 = jnp.dot(p[hq:2 * hq], jnp.where(lane_k, one, v128), preferred_element_type=F32)
                num = jnp.where(lane_lo, pv0, pv1)
                den = pltpu.roll(jnp.where(lane_lo, pv1, pv0), HEAD_DIM, 1)
                os_[c, pl.ds(q0 + r0, hq), sl] = (num / den).astype(BF16)
                lse_pair = jnp.where(lane_lo, m[0:hq], m[hq:2 * hq]) + jnp.log(den)
                lse_tile = jnp.where(lane_pair == hp, lse_pair, lse_tile)
            ls[c, pl.ds(q0 + r0, hq), :] = lse_tile
        return carry

    lax.fori_loop(0, r * nsub, unit, 0, unroll=2)

    if r == 1:
        o_ref[...] = os_[0]
        lse_ref[...] = ls[0]
    else:
        n = PERM // r
        for b in range(r * tl // PERM):
            cat = jnp.concatenate([os_[c, b * n:(b + 1) * n, :] for c in range(r)], axis=0)
            o_ref[b * PERM:(b + 1) * PERM, :] = jnp.dot(pt_ref[...], cat, preferred_element_type=F32).astype(BF16)
        for c in range(r):
            lse_ref[pl.ds(c, tl, stride=r), :] = ls[c]


def _attention_group(q, k, v, perm_t, r):
    b, _, l, _ = q.shape
    tl = min(ATT_CLASS_ROWS[r], l)
    ts = r * tl
    nh = tl // BAND
    last = l // BAND - 1
    cur = pl.BlockSpec((None, r, tl, ATT_WIDTH), lambda bi, i: (bi, 0, i, 0))
    prv = pl.BlockSpec((None, r, BAND, ATT_WIDTH), lambda bi, i: (bi, 0, jnp.maximum(i * nh - 1, 0), 0))
    nxt = pl.BlockSpec((None, r, BAND, ATT_WIDTH), lambda bi, i: (bi, 0, jnp.minimum((i + 1) * nh, last), 0))
    body = functools.partial(_attn_body, r=r, tl=tl, seq_l=l)
    return pl.pallas_call(
        body,
        grid=(b, l // tl),
        in_specs=[cur, prv, cur, nxt, prv, cur, nxt, pl.BlockSpec((PERM, PERM), lambda bi, i: (0, 0))],
        out_specs=[pl.BlockSpec((None, ts, ATT_WIDTH), lambda bi, i: (bi, i, 0)),
                   pl.BlockSpec((None, ts, V7X_LANES), lambda bi, i: (bi, i, 0))],
        out_shape=[jax.ShapeDtypeStruct((b, r * l, ATT_WIDTH), BF16),
                   jax.ShapeDtypeStruct((b, r * l, V7X_LANES), F32)],
        scratch_shapes=[
            pltpu.VMEM((r, tl + 2 * BAND, ATT_WIDTH), BF16),
            pltpu.VMEM((r, tl + 2 * BAND, ATT_WIDTH), BF16),
            pltpu.VMEM((r, tl, ATT_WIDTH), BF16),
            pltpu.VMEM((r, tl, V7X_LANES), F32),
        ],
        compiler_params=_cparams(("parallel", "parallel")),
        name=f"attn_r{r}",
    )(q, k, k, k, v, v, v, perm_t)


def _hgrn_chunk_fn(z_ref, q_ref, i_ref, lb_ref, lb_row, wsel_ref, o_ref, st_ref, rev):
    ch, sb = HGRN_CHUNK, HGRN_SUB
    row = lax.broadcasted_iota(jnp.int32, (ch, ch), 0)
    col = lax.broadcasted_iota(jnp.int32, (ch, ch), 1)
    dist = (col - row) if rev else (row - col)
    same = lambda n: (row // n) == (col // n)
    diag_mask = same(sb) & (dist >= 0)
    level_sizes = tuple(n for n in (1, 2, 4, 8, 16, 32) if n >= sb)
    level_masks = [same(2 * n) & jnp.logical_not(same(n)) & (dist > 0) for n in level_sizes]
    sl8 = V7X_SUBLANES
    row8 = lax.broadcasted_iota(jnp.int32, (sl8, HGRN_DK), 0)
    sub = lax.broadcasted_iota(jnp.int32, (ch, HGRN_DK), 0) % sb
    tau = (sb - 1 - sub) if rev else sub
    nt = (((1,), (1,)), ((), ()))
    tn = (((0,), (0,)), ((), ()))

    def roll8(x, k):
        return pltpu.roll(x.reshape(ch // sl8, sl8, HGRN_DK), k % sl8, 1).reshape(ch, HGRN_DK)

    prev = lambda x, k: roll8(x, -k if rev else k)
    nxt = lambda x, k: roll8(x, k if rev else -k)
    last_row = lambda start, n: start if rev else start + n - 1

    def widen(pf, sf, n):
        later_first = rev
        pf_rows, sf_rows = [], []
        for v in range(ch // sl8):
            x, y = pf[sl8 * v:sl8 * (v + 1)], sf[sl8 * v:sl8 * (v + 1)]
            if n >= sl8:
                blk = (sl8 * v) // n
                first = blk % 2 == 0
                sib = (blk + 1) * n if first else (blk - 1) * n
                r = last_row(sib, n)
                total = jnp.broadcast_to(pf[r:r + 1, :], (sl8, HGRN_DK))
                if first == later_first:
                    x = x * total
                else:
                    y = y * total
            else:
                pmul = jnp.ones((sl8, HGRN_DK), F32)
                smul = pmul
                for b0 in range(0, sl8, 2 * n):
                    lo_tot = jnp.broadcast_to(x[last_row(b0, n):last_row(b0, n) + 1, :], (sl8, HGRN_DK))
                    hi_tot = jnp.broadcast_to(x[last_row(b0 + n, n):last_row(b0 + n, n) + 1, :], (sl8, HGRN_DK))
                    in_lo = (row8 >= b0) & (row8 < b0 + n)
                    in_hi = (row8 >= b0 + n) & (row8 < b0 + 2 * n)
                    if later_first:
                        pmul, smul = jnp.where(in_lo, hi_tot, pmul), jnp.where(in_hi, lo_tot, smul)
                    else:
                        pmul, smul = jnp.where(in_hi, lo_tot, pmul), jnp.where(in_lo, hi_tot, smul)
                x, y = x * pmul, y * smul
            pf_rows.append(x)
            sf_rows.append(y)
        return jnp.concatenate(pf_rows, axis=0), jnp.concatenate(sf_rows, axis=0)

    def chunk(c):
        r0 = c * ch if isinstance(c, int) else pl.multiple_of(c * ch, ch)
        heads = range(HGRN_HEADS)
        sls = [slice(h * HGRN_DK, (h + 1) * HGRN_DK) for h in heads]
        qs, ivs, kfs, pfs, sfs, bands = [], [], [], [], [], []
        for sl in sls:
            z = z_ref[pl.ds(r0, ch), sl]
            q = q_ref[pl.ds(r0, ch), sl].astype(F32)
            lb = lb_ref[lb_row:lb_row + 1, sl]
            one_m_lb = 1.0 - lb
            e = jnp.exp(-jnp.abs(z))
            inv = 1.0 / (1.0 + e)
            pos = z >= 0
            f = lb + one_m_lb * (jnp.where(pos, 1.0, e) * inv)
            kf = one_m_lb * (jnp.where(pos, e, 1.0) * inv)
            pf = f
            sf = jnp.where(tau == sb - 1, 1.0, nxt(f, 1))
            for k in (k for k in (1, 2, 4) if k < sb):
                pf = pf * jnp.where(tau >= k, prev(pf, k), 1.0)
                sf = sf * jnp.where(tau <= sb - 1 - k, nxt(sf, k), 1.0)
            g = kf
            terms = [q * g]
            for d in range(1, sb):
                g = prev(g, 1) * f
                terms.append(q * g)
            qs.append(q); ivs.append(i_ref[pl.ds(r0, ch), sl]); kfs.append(kf); pfs.append(pf); sfs.append(sf)
            bands.append(jnp.concatenate(terms, axis=1).astype(BF16))

        bands = [jnp.dot(x, wsel_ref[...], preferred_element_type=F32) for x in bands]

        levels = []
        for n in level_sizes:
            levels.append([lax.dot_general((qs[h] * pfs[h]).astype(BF16), (kfs[h] * sfs[h]).astype(BF16), nt,
                                           preferred_element_type=F32) for h in heads])
            for h in heads:
                pfs[h], sfs[h] = widen(pfs[h], sfs[h], n)
        sts = [st_ref[h] for h in heads]
        inter = [lax.dot_general((qs[h] * pfs[h]).astype(BF16), sts[h].astype(BF16), nt, preferred_element_type=F32)
                 for h in heads]
        upd = [lax.dot_general(ivs[h], (kfs[h] * sfs[h]).astype(BF16), tn, preferred_element_type=F32)
               for h in heads]

        r_end = last_row(0, ch)
        for h in heads:
            skew = pltpu.roll(bands[h], 0, 1, stride=1, stride_axis=0)
            a = jnp.where(diag_mask, skew[:, 0:ch], 0.0)
            for lvl, mask in zip(levels, level_masks):
                a = jnp.where(mask, lvl[h], a)
            o = jnp.dot(a.astype(BF16), ivs[h], preferred_element_type=F32)
            o_ref[pl.ds(r0, ch), sls[h]] = (o + inter[h]).astype(o_ref.dtype)
            st_ref[h] = sts[h] * pfs[h][r_end:r_end + 1, :] + upd[h]

    return chunk


def _hgrn_body(z_ref, q_ref, i_ref, lb_ref, wsel_ref, o_ref, st_ref, *, rev, nch):
    @pl.when(pl.program_id(1) == 0)
    def _():
        st_ref[...] = jnp.zeros_like(st_ref)

    chunk = _hgrn_chunk_fn(z_ref, q_ref, i_ref, lb_ref, 0, wsel_ref, o_ref, st_ref, rev)

    def step(ci, carry):
        chunk((nch - 1 - ci) if rev else ci)
        return carry

    lax.fori_loop(0, nch, step, 0, unroll=4)


def _band_selector(rev):
    w = np.zeros((HGRN_SUB * HGRN_DK, V7X_LANES), np.float32)
    for d in range(HGRN_SUB):
        w[d * HGRN_DK:(d + 1) * HGRN_DK, d if rev else (V7X_LANES - d) % V7X_LANES] = 1.0
    return jnp.asarray(w, BF16)


def _hgrn_scan(z, bd, lb, rev):
    b, s, _ = z.shape
    ts = HGRN_TILE
    nb = s // ts
    order = (lambda j: nb - 1 - j) if rev else (lambda j: j)
    spec = lambda col: pl.BlockSpec((None, ts, HGRN_WIDTH), lambda bi, j: (bi, order(j), col))
    const = lambda shape: pl.BlockSpec(shape, lambda bi, j: (0, 0))
    body = functools.partial(_hgrn_body, rev=rev, nch=ts // HGRN_CHUNK)
    return pl.pallas_call(
        body,
        grid=(b, nb),
        in_specs=[spec(1 if rev else 0), spec(BD_Q), spec(BD_I), const((1, HGRN_WIDTH)),
                  const((HGRN_SUB * HGRN_DK, V7X_LANES))],
        out_specs=spec(0),
        out_shape=jax.ShapeDtypeStruct((b, s, HGRN_WIDTH), BF16),
        scratch_shapes=[pltpu.VMEM((HGRN_HEADS, HGRN_DK, HGRN_DK), F32)],
        compiler_params=_cparams(("parallel", "arbitrary")),
        name="hgrn_bwd" if rev else "hgrn_fwd",
    )(z, bd, bd, lb, _band_selector(rev))


def _fnet_a_body(u_ref, cs_ref, m_ref, o_ref, *, n1, nt):
    ngrp = FNET_WIDTH // FNET_GROUP_DIM
    for s in range(nt):
        zr, zi = [], []
        for g in range(ngrp):
            c0 = s * FNET_WIDTH + g * FNET_GROUP_DIM
            zz = jnp.dot(u_ref[:, c0:c0 + FNET_GROUP_DIM], cs_ref[...], preferred_element_type=F32)
            zr.append(zz[:, :FNET_GROUP_DIM])
            zi.append(zz[:, FNET_GROUP_DIM:])
        zcat = jnp.concatenate([jnp.concatenate(zr, axis=1), jnp.concatenate(zi, axis=1)], axis=0).astype(BF16)
        res = jnp.dot(m_ref[s], zcat, preferred_element_type=F32)
        o_ref[:, s * FNET_WIDTH:(s + 1) * FNET_WIDTH] = res.astype(o_ref.dtype)


def _fnet_c_body(b_ref, cs_ref, o_ref, *, kt):
    for k in range(kt):
        bcat = jnp.concatenate([b_ref[0, k], b_ref[1, k]], axis=0)
        x = jnp.dot(cs_ref[...], bcat, preferred_element_type=F32)
        o_ref[:, k * FNET_WIDTH:(k + 1) * FNET_WIDTH] = x.astype(o_ref.dtype)


def _fnet_tables(n1, n2):
    n = n1 * n2
    k = np.arange(FNET_GROUP_DIM)
    ang = 2.0 * np.pi * ((k[:, None] * k[None, :]) % FNET_GROUP_DIM) / FNET_GROUP_DIM
    cs_ch = np.concatenate([np.cos(ang), -np.sin(ang)], axis=1) / np.sqrt(FNET_GROUP_DIM)
    k1 = np.arange(n1)[None, :, None]
    m1 = np.arange(n1)[None, None, :]
    s2 = np.arange(n2)[:, None, None]
    idx = (k1 * m1 * n2 + s2 * k1) % n
    th = 2.0 * np.pi * idx / n
    gc, gs = np.cos(th) / np.sqrt(n1), np.sin(th) / np.sqrt(n1)
    m = np.concatenate([np.concatenate([gc, gs], axis=2), np.concatenate([-gs, gc], axis=2)], axis=1)
    k2 = np.arange(n2)
    ang2 = 2.0 * np.pi * ((k2[:, None] * k2[None, :]) % n2) / n2
    cs_seq = np.concatenate([np.cos(ang2), np.sin(ang2)], axis=1) / np.sqrt(n2)
    return (jnp.asarray(cs_ch, BF16), jnp.asarray(m, BF16), jnp.asarray(cs_seq, BF16))


def _fourier_branch(u, n1, n2):
    b, s, _ = u.shape
    cs_ch, m, cs_seq = _fnet_tables(n1, n2)
    nt = FNET_SLABS_PER_STEP
    kt = FNET_ROWS_PER_STEP
    u2 = u.reshape(b, n1, n2 * FNET_WIDTH)
    a = pl.pallas_call(
        functools.partial(_fnet_a_body, n1=n1, nt=nt),
        grid=(b, n2 // nt),
        in_specs=[
            pl.BlockSpec((None, n1, nt * FNET_WIDTH), lambda bi, j: (bi, 0, j)),
            pl.BlockSpec((FNET_GROUP_DIM, 2 * FNET_GROUP_DIM), lambda bi, j: (0, 0)),
            pl.BlockSpec((nt, 2 * n1, 2 * n1), lambda bi, j: (j, 0, 0)),
        ],
        out_specs=pl.BlockSpec((None, 2 * n1, nt * FNET_WIDTH), lambda bi, j: (bi, 0, j)),
        out_shape=jax.ShapeDtypeStruct((b, 2 * n1, n2 * FNET_WIDTH), BF16),
        compiler_params=_cparams(("parallel", "parallel")),
        name="fnet_a",
    )(u2, cs_ch, m)
    a5 = a.reshape(b, 2, n1, n2, FNET_WIDTH)
    y = pl.pallas_call(
        functools.partial(_fnet_c_body, kt=kt),
        grid=(b, n1 // kt),
        in_specs=[
            pl.BlockSpec((None, 2, kt, n2, FNET_WIDTH), lambda bi, j: (bi, 0, j, 0, 0)),
            pl.BlockSpec((n2, 2 * n2), lambda bi, j: (0, 0)),
        ],
        out_specs=pl.BlockSpec((None, n2, kt * FNET_WIDTH), lambda bi, j: (bi, 0, j)),
        out_shape=jax.ShapeDtypeStruct((b, n2, n1 * FNET_WIDTH), BF16),
        compiler_params=_cparams(("parallel", "parallel")),
        name="fnet_c",
    )(a5, cs_seq)
    return y.reshape(b, s, FNET_WIDTH)


def _merge_body(x_ref, o0_ref, o1_ref, o2_ref, l0_ref, l1_ref, l2_ref, hf_ref, hb_ref, gh_ref, y_ref,
                ga_ref, gb_ref, gc_ref, wa_ref, wh_ref, wf_ref, wo_ref, hg_ref, ex_ref, out_ref):
    lses = [l0_ref[...], l1_ref[...], l2_ref[...]]
    m = jnp.maximum(jnp.maximum(lses[0], lses[1]), lses[2])
    ws = [jnp.exp(l - m) for l in lses]
    inv = 1.0 / (ws[0] + ws[1] + ws[2])
    att = None
    for w, o_ref in zip(ws, (o0_ref, o1_ref, o2_ref)):
        alpha = jnp.dot((w * inv).astype(BF16), ex_ref[...], preferred_element_type=F32)
        term = alpha * o_ref[...].astype(F32)
        att = term if att is None else att + term
    a = jnp.dot(att.astype(BF16), wa_ref[...], preferred_element_type=F32)
    oh = hf_ref[...].astype(F32) + hb_ref[...].astype(F32)
    gh = gh_ref[...].astype(F32)
    heads = []
    for h in range(HGRN_HEADS):
        sl = slice(h * HGRN_DK, (h + 1) * HGRN_DK)
        gate = gh[:, sl]
        heads.append(_rms_rows(oh[:, sl], hg_ref[...]) * (gate * jax.nn.sigmoid(gate)))
    bh = jnp.dot(jnp.concatenate(heads, axis=1).astype(BF16), wh_ref[...], preferred_element_type=F32)
    c = jnp.dot(y_ref[...], wf_ref[...], preferred_element_type=F32)
    sig = lambda ref: jax.nn.sigmoid(ref[...].astype(F32))
    merged = sig(ga_ref) * a + sig(gb_ref) * bh + sig(gc_ref) * c
    out_ref[...] = x_ref[...] + jnp.dot(merged.astype(BF16), wo_ref[...], preferred_element_type=F32)


def _merge(x2d, att_outs, att_lses, hf, hb, bd, y, wa, wh, wf, wo, hgain, expand, l):
    t, tm = x2d.shape[0], TOKEN_TILE
    row = lambda w: pl.BlockSpec((tm, w), lambda i: (i, 0))
    gate = lambda k: pl.BlockSpec((tm, D_MODEL), lambda i: (i, k))
    in_specs = ([row(D_MODEL)] + [row(ATT_WIDTH)] * 3 + [row(V7X_LANES)] * 3 + [row(HGRN_WIDTH)] * 2
                + [pl.BlockSpec((tm, HGRN_WIDTH), lambda i: (i, BD_G)),
                   row(FNET_WIDTH), gate(0), gate(1), gate(2),
                   _of_layer(l, ATT_WIDTH, D_MODEL), _of_layer(l, HGRN_WIDTH, D_MODEL),
                   _of_layer(l, FNET_WIDTH, D_MODEL), _of_layer(l, D_MODEL, D_MODEL), _of_layer(l, 1, HGRN_DK),
                   pl.BlockSpec((V7X_LANES, ATT_WIDTH), lambda i: (0, 0))])
    return pl.pallas_call(
        _merge_body,
        grid=(t // tm,),
        in_specs=in_specs,
        out_specs=row(D_MODEL),
        out_shape=jax.ShapeDtypeStruct((t, D_MODEL), F32),
        compiler_params=_cparams(("parallel",)),
        name="merge",
    )(x2d, *att_outs, *att_lses, hf, hb, bd, y, bd, bd, bd, wa, wh, wf, wo, hgain, expand)


def _trunk_consts(seq):
    half = HEAD_DIM // 2
    inv_freq = np.float32(ROPE_THETA) ** (-np.arange(half, dtype=np.float32) / np.float32(half))
    ang = np.arange(seq, dtype=np.float32)[:, None] * inv_freq[None, :]
    cos_nat, sin_nat = np.cos(ang.astype(np.float64)), np.sin(ang.astype(np.float64))
    rope = {}
    for _, r in WINDOW_DILATIONS:
        order = lambda tab: tab.reshape(seq // PERM, PERM // r, r, half).transpose(0, 2, 1, 3).reshape(seq, half)
        rope[r] = (jnp.asarray(order(cos_nat), F32), jnp.asarray(order(sin_nat), F32))
    hd = np.arange(V7X_MXU_DIM) // HEAD_DIM
    head_bd = jnp.asarray((hd[:, None] == hd[None, :]) / HEAD_DIM, BF16)
    perm, perm_t = {}, {}
    for _, r in WINDOW_DILATIONS:
        idx = np.arange(PERM)
        n = PERM // r
        p = np.zeros((PERM, PERM), np.float32)
        p[idx, (idx % n) * r + idx // n] = 1.0
        perm[r], perm_t[r] = jnp.asarray(p, BF16), jnp.asarray(p.T, BF16)
    expand = np.zeros((V7X_LANES, ATT_WIDTH), np.float32)
    for h in range(ATT_HEADS):
        expand[_lse_lane(h), h * HEAD_DIM:(h + 1) * HEAD_DIM] = 1.0
    return dict(rope=rope, head_bd=head_bd, perm=perm, perm_t=perm_t, expand=jnp.asarray(expand, BF16),
                n1=FNET_SLOW_LEN)


def _stacked_weights(hgrn_lb_logits, ffn1_norm, ffn1_w_gate, ffn1_w_up, ffn1_w_down, mix_norm, w_in, q_norm,
                     k_norm, w_att_out, hgrn_out_norm, w_hgrn_out, w_fnet_out, w_out, ffn2_norm, ffn2_w_gate,
                     ffn2_w_up, ffn2_w_down):
    bf = lambda w: w.astype(BF16)
    vec = lambda v: v.astype(F32)[:, None, :]
    heads_per_tile = ATT_WIDTH // HEAD_DIM
    p = jax.nn.softmax(hgrn_lb_logits.astype(F32), axis=0)
    cum = jnp.cumsum(p, axis=0)
    return dict(
        ffn1=(vec(ffn1_norm), bf(ffn1_w_gate), bf(ffn1_w_up), bf(ffn1_w_down)),
        ffn2=(vec(ffn2_norm), bf(ffn2_w_gate), bf(ffn2_w_up), bf(ffn2_w_down)),
        mix_norm=vec(mix_norm), w_in=bf(w_in),
        gq=vec(jnp.tile(q_norm, (1, heads_per_tile))), gk=vec(jnp.tile(k_norm, (1, heads_per_tile))),
        lb=cum - cum[0:1],
        wa=bf(w_att_out), wh=bf(w_hgrn_out), wf=bf(w_fnet_out), wo=bf(w_out), hgain=vec(hgrn_out_norm),
    )


def _encoder_layer(x, wts, consts, l):
    b, s, _ = x.shape
    t = b * s
    x2d = x.reshape(t, D_MODEL)
    x2d = _ffn(x2d, *wts["ffn1"], l)
    z, bd, u = _proj_rest(x2d, wts["mix_norm"], wts["w_in"], l)
    z3, bd3 = z.reshape(b, s, Z_COLS), bd.reshape(b, s, BD_COLS)
    qkv = _proj_qkv(x2d, wts["mix_norm"], wts["w_in"], wts["gq"], wts["gk"], consts, b, s, l)
    hf = _hgrn_scan(z3, bd3, wts["lb"][l, 0].reshape(1, -1), False).reshape(t, HGRN_WIDTH)
    hb = _hgrn_scan(z3, bd3, wts["lb"][l, 1].reshape(1, -1), True).reshape(t, HGRN_WIDTH)
    outs, lses = [], []
    for gi, (w, r) in enumerate(WINDOW_DILATIONS):
        assert w // (2 * r) == BAND
        o, lse = _attention_group(qkv[3 * gi], qkv[3 * gi + 1], qkv[3 * gi + 2], consts["perm_t"][r], r)
        outs.append(o.reshape(t, ATT_WIDTH))
        lses.append(lse.reshape(t, V7X_LANES))
    y = _fourier_branch(u.reshape(b, s, FNET_WIDTH), consts["n1"], s // consts["n1"]).reshape(t, FNET_WIDTH)
    x2d = _merge(x2d, outs, lses, hf, hb, bd, y, wts["wa"], wts["wh"], wts["wf"], wts["wo"], wts["hgain"],
                 consts["expand"], l)
    x2d = _ffn(x2d, *wts["ffn2"], l)
    return x2d.reshape(b, s, D_MODEL)


def kernel(x_prompt, x_sample, hgrn_lb_logits, ffn1_norm, ffn1_w_gate, ffn1_w_up, ffn1_w_down, mix_norm, w_in,
           q_norm, k_norm, w_att_out, hgrn_out_norm, w_hgrn_out, w_fnet_out, w_out, ffn2_norm, ffn2_w_gate,
           ffn2_w_up, ffn2_w_down):
    wts = _stacked_weights(hgrn_lb_logits, ffn1_norm, ffn1_w_gate, ffn1_w_up, ffn1_w_down, mix_norm, w_in, q_norm,
                           k_norm, w_att_out, hgrn_out_norm, w_hgrn_out, w_fnet_out, w_out, ffn2_norm,
                           ffn2_w_gate, ffn2_w_up, ffn2_w_down)
    consts = _trunk_consts(max(x_prompt.shape[1], x_sample.shape[1]))

    def trunk(x):
        for l in range(w_in.shape[0]):
            x = _encoder_layer(x, wts, consts, l)
        return x

    return (trunk(x_prompt), trunk(x_sample))
```

```python
import functools

import numpy as np
import jax
import jax.numpy as jnp
from jax import lax
from jax.experimental import pallas as pl
from jax.experimental.pallas import tpu as pltpu

F32 = jnp.float32
BF16 = jnp.bfloat16

D_MODEL = 1024
D_FF = 2816
HEAD_DIM = 64
ATT_HEADS = 8
WINDOW_DILATIONS = ((128, 1), (512, 4), (2048, 16))
N_GROUPS = len(WINDOW_DILATIONS)
ATT_WIDTH = ATT_HEADS * HEAD_DIM
ATT_COLS = N_GROUPS * ATT_WIDTH
ROPE_THETA = 10000.0
HGRN_HEADS = 4
HGRN_DK = 128
HGRN_WIDTH = 512
HGRN_CHUNK = 64
HGRN_SUB = 4
V7X_SUBLANES = 8
FNET_WIDTH = 512
FNET_GROUP_DIM = 128
EPS = 1e-6
NEG_INF = -1e30
IN_COLS = 10752
BAND = 64
PERM = 256

QKV_COLS = 3 * ATT_COLS
Z_COLS = 2 * HGRN_WIDTH
GATE_COLS = 3 * D_MODEL
BD_COLS = 3 * HGRN_WIDTH
BD_Q, BD_I, BD_G = 0, 1, 2
GATE_COL0 = IN_COLS - GATE_COLS

V7X_LANES = 128
V7X_MXU_DIM = 256
V7X_VMEM_LIMIT_BYTES = 56 * 1024 * 1024
PROJ_CHUNK = 512
ATT_UNIT_ROWS = 128
LSE_LANES_PER_HEAD = V7X_LANES // ATT_HEADS
FNET_SLOW_LEN = 128
FF_CHUNKS = ((0, 768), (768, 1536), (1536, 2304), (2304, 2816))
FFN_TOKEN_TILE = 1024
TOKEN_TILE = 512
ATT_CLASS_ROWS = {1: 2048, 4: 512, 16: 128}
HGRN_TILE = 1024
FNET_SLABS_PER_STEP = 8
FNET_ROWS_PER_STEP = 32
ATT_ROWS_PER_TILE = 64


def _cparams(sem):
    return pltpu.CompilerParams(dimension_semantics=sem, vmem_limit_bytes=V7X_VMEM_LIMIT_BYTES)


def _rms_rows(x, g):
    ms = jnp.mean(x * x, axis=-1, keepdims=True)
    return x * lax.rsqrt(ms + EPS) * g


def _ffn_body(x_ref, g_ref, wg_ref, wu_ref, wd_ref, o_ref):
    x = x_ref[...]
    h = _rms_rows(x, g_ref[...]).astype(BF16)
    acc = None
    for lo, hi in FF_CHUNKS:
        gate = jnp.dot(h, wg_ref[:, lo:hi], preferred_element_type=F32)
        up = jnp.dot(h, wu_ref[:, lo:hi], preferred_element_type=F32)
        a = (gate * jax.nn.sigmoid(gate) * up).astype(BF16)
        part = jnp.dot(a, wd_ref[lo:hi, :], preferred_element_type=F32)
        acc = part if acc is None else acc + part
    o_ref[...] = x + 0.5 * acc


def _of_layer(l, rows, cols, **kw):
    return pl.BlockSpec((None, rows, cols), lambda *_: (l, 0, 0), **kw)


def _ffn(x2d, g, wg, wu, wd, l):
    t, tm = x2d.shape[0], FFN_TOKEN_TILE
    one = pl.Buffered(1)
    return pl.pallas_call(
        _ffn_body,
        grid=(t // tm,),
        in_specs=[
            pl.BlockSpec((tm, D_MODEL), lambda i: (i, 0)),
            _of_layer(l, 1, D_MODEL),
            _of_layer(l, D_MODEL, D_FF, pipeline_mode=one),
            _of_layer(l, D_MODEL, D_FF, pipeline_mode=one),
            _of_layer(l, D_FF, D_MODEL, pipeline_mode=one),
        ],
        out_specs=pl.BlockSpec((tm, D_MODEL), lambda i: (i, 0)),
        out_shape=jax.ShapeDtypeStruct((t, D_MODEL), F32),
        compiler_params=_cparams(("parallel",)),
        name="ffn",
    )(x2d, g, wg, wu, wd)


def _head_norm_rope(acc, gain, cos, sin, bd_ref):
    tm = acc.shape[0]
    ln, mx, half = V7X_LANES, V7X_MXU_DIM, HEAD_DIM // 2
    lane = lax.broadcasted_iota(jnp.int32, (tm, ln), 1)
    first_half = (lane % HEAD_DIM) < half
    slabs = []
    for s in range(ATT_WIDTH // mx):
        a = acc[:, s * mx:(s + 1) * mx]
        ms = jnp.dot((a * a).astype(BF16), bd_ref[...], preferred_element_type=F32)
        y = a * lax.rsqrt(ms + EPS) * gain[:, s * mx:(s + 1) * mx]
        for t in range(mx // ln):
            y1 = y[:, t * ln:(t + 1) * ln]
            yr = jnp.where(first_half, pltpu.roll(y1, ln - half, 1), pltpu.roll(y1, half, 1))
            slabs.append(y1 * cos + yr * sin)
    return jnp.concatenate(slabs, axis=1)


def _qkv_pieces(x_ref, g_ref, w_ref, gq_ref, gk_ref, table_refs, bd_ref, p4_ref, p16_ref, out_refs):
    tm = x_ref.shape[0]
    h = _rms_rows(x_ref[...], g_ref[...]).astype(BF16)

    def permuted(p_ref):
        blocks = [jnp.dot(p_ref[...], h[b * PERM:(b + 1) * PERM], preferred_element_type=F32).astype(BF16)
                  for b in range(tm // PERM)]
        return jnp.concatenate(blocks, axis=0)

    lhs = (h, permuted(p4_ref), permuted(p16_ref))
    gains = (gq_ref[...] * HEAD_DIM ** -0.5, gk_ref[...])

    def lanes128(c_ref, s_ref):
        c, s = c_ref[...], s_ref[...]
        return jnp.concatenate([c, c, c, c], axis=1), jnp.concatenate([-s, s, -s, s], axis=1)

    tables = [lanes128(table_refs[2 * gi], table_refs[2 * gi + 1]) for gi in range(N_GROUPS)]

    def piece(gi, ti):
        def run():
            r = WINDOW_DILATIONS[gi][1]
            n = PERM // r
            col = ti * ATT_COLS + gi * ATT_WIDTH
            y = jnp.dot(lhs[gi], w_ref[:, col:col + ATT_WIDTH], preferred_element_type=F32)
            if ti < 2:
                y = _head_norm_rope(y, gains[ti], *tables[gi], bd_ref)
            y = y.astype(BF16)
            o_ref = out_refs[3 * gi + ti]
            if r == 1:
                o_ref[0] = y
            else:
                for b in range(tm // PERM):
                    for c in range(r):
                        o_ref[c, b * n:(b + 1) * n, :] = y[b * PERM + c * n:b * PERM + (c + 1) * n, :]
        return run

    return [piece(gi, ti) for gi in range(N_GROUPS) for ti in range(3)]


def _proj_qkv_body(x_ref, g_ref, w_ref, gq_ref, gk_ref, c1_ref, s1_ref, c4_ref, s4_ref, c16_ref, s16_ref,
                   bd_ref, p4_ref, p16_ref, *out_refs):
    tables = (c1_ref, s1_ref, c4_ref, s4_ref, c16_ref, s16_ref)
    for run in _qkv_pieces(x_ref, g_ref, w_ref, gq_ref, gk_ref, tables, bd_ref, p4_ref, p16_ref, out_refs):
        run()


def _proj_rest_body(x_ref, g_ref, w_ref, z_ref, bd_ref, u_ref):
    h = _rms_rows(x_ref[...], g_ref[...]).astype(BF16)
    tn = PROJ_CHUNK
    proj = lambda c0: jnp.dot(h, w_ref[:, c0:c0 + tn], preferred_element_type=F32)
    for k in range(Z_COLS // tn):
        z_ref[:, k * tn:(k + 1) * tn] = proj(QKV_COLS + k * tn)
    hgrn0 = QKV_COLS + Z_COLS
    for k in range(BD_COLS // tn):
        bd_ref[:, k * tn:(k + 1) * tn] = proj(hgrn0 + k * tn).astype(BF16)
    u_ref[...] = proj(hgrn0 + BD_COLS).astype(BF16)


def _proj_qkv(x2d, g, w_in, gq, gk, consts, batch, seq, l):
    t, tm = x2d.shape[0], TOKEN_TILE
    nsb = seq // tm
    const = lambda i: (0, 0)
    tab = pl.BlockSpec((tm, HEAD_DIM // 2), lambda i: (i % nsb, 0))
    in_specs = [
        pl.BlockSpec((tm, D_MODEL), lambda i: (i, 0)),
        _of_layer(l, 1, D_MODEL),
        _of_layer(l, D_MODEL, IN_COLS, pipeline_mode=pl.Buffered(1)),
        _of_layer(l, 1, ATT_WIDTH), _of_layer(l, 1, ATT_WIDTH),
        tab, tab, tab, tab, tab, tab,
        pl.BlockSpec((V7X_MXU_DIM, V7X_MXU_DIM), const),
        pl.BlockSpec((PERM, PERM), const), pl.BlockSpec((PERM, PERM), const),
    ]
    out_specs, out_shape = [], []
    for _, r in WINDOW_DILATIONS:
        for _ in range(3):
            out_specs.append(pl.BlockSpec((None, r, tm // r, ATT_WIDTH), lambda i: (i // nsb, 0, i % nsb, 0)))
            out_shape.append(jax.ShapeDtypeStruct((batch, r, seq // r, ATT_WIDTH), BF16))
    rope = consts["rope"]
    return pl.pallas_call(
        _proj_qkv_body,
        grid=(t // tm,),
        in_specs=in_specs,
        out_specs=out_specs,
        out_shape=out_shape,
        compiler_params=_cparams(("parallel",)),
        name="proj_qkv",
    )(x2d, g, w_in, gq, gk, rope[1][0], rope[1][1], rope[4][0], rope[4][1], rope[16][0], rope[16][1],
      consts["head_bd"], consts["perm"][4], consts["perm"][16])


def _proj_rest(x2d, g, w_in, l):
    t, tm = x2d.shape[0], TOKEN_TILE
    row = lambda w: pl.BlockSpec((tm, w), lambda i: (i, 0))
    return pl.pallas_call(
        _proj_rest_body,
        grid=(t // tm,),
        in_specs=[row(D_MODEL), _of_layer(l, 1, D_MODEL),
                  _of_layer(l, D_MODEL, IN_COLS, pipeline_mode=pl.Buffered(1))],
        out_specs=[row(Z_COLS), row(BD_COLS), row(FNET_WIDTH)],
        out_shape=[jax.ShapeDtypeStruct((t, Z_COLS), F32), jax.ShapeDtypeStruct((t, BD_COLS), BF16),
                   jax.ShapeDtypeStruct((t, FNET_WIDTH), BF16)],
        compiler_params=_cparams(("parallel",)),
        name="proj_rest",
    )(x2d, g, w_in)


def _lse_lane(h):
    return HEAD_DIM * (h % 2) + LSE_LANES_PER_HEAD * (h // 2)


def _attn_body(q_ref, kp_ref, kc_ref, kn_ref, vp_ref, vc_ref, vn_ref, pt_ref, o_ref, lse_ref,
               ks, vs, os_, ls, *, r, tl, seq_l):
    blk = pl.program_id(1)
    ln = V7X_LANES
    sq, kw = ATT_UNIT_ROWS, ATT_UNIT_ROWS + 2 * BAND
    ks[:, 0:BAND, :] = kp_ref[...]
    ks[:, BAND:BAND + tl, :] = kc_ref[...]
    ks[:, BAND + tl:2 * BAND + tl, :] = kn_ref[...]
    vs[:, 0:BAND, :] = vp_ref[...]
    vs[:, BAND:BAND + tl, :] = vc_ref[...]
    vs[:, BAND + tl:2 * BAND + tl, :] = vn_ref[...]

    nsub = tl // sq
    hq = ATT_ROWS_PER_TILE
    qi = lax.broadcasted_iota(jnp.int32, (sq, kw), 0)
    ki = lax.broadcasted_iota(jnp.int32, (sq, kw), 1)
    in_band = jnp.abs(ki - BAND - qi) <= BAND
    lane = lax.broadcasted_iota(jnp.int32, (hq, ln), 1)
    lane_lo = lane < HEAD_DIM
    lane_pair = (lane % HEAD_DIM) // LSE_LANES_PER_HEAD
    lane_k = lax.broadcasted_iota(jnp.int32, (kw, ln), 1) < HEAD_DIM
    nt = (((1,), (1,)), ((), ()))

    def unit(u, carry):
        c = u // nsub
        q0 = pl.multiple_of((u % nsub) * sq, sq)
        kpos = blk * tl + q0 - BAND + ki
        valid = in_band & (kpos >= 0) & (kpos < seq_l)
        q2 = q_ref[c, pl.ds(q0, sq), :]
        k2 = ks[c, pl.ds(q0, kw), :]
        v2 = vs[c, pl.ds(q0, kw), :]
        for half in range(sq // hq):
            r0, r1 = half * hq, (half + 1) * hq
            vmask = jnp.concatenate([valid[r0:r1], valid[r0:r1]], axis=0)
            lse_tile = jnp.zeros((hq, ln), F32)
            for hp in range(ATT_HEADS // 2):
                sl = slice(hp * ln, (hp + 1) * ln)
                q128, k128, v128 = q2[r0:r1, sl], k2[:, sl], v2[:, sl]
                zero = jnp.zeros_like(q128)
                qstack = jnp.concatenate([jnp.where(lane_lo, q128, zero), jnp.where(lane_lo, zero, q128)], axis=0)
                s = lax.dot_general(qstack, k128, nt, preferred_element_type=F32)
                s = jnp.where(vmask, s, NEG_INF)
                m = jnp.max(s, axis=-1, keepdims=True)
                p = jnp.exp(s - m).astype(BF16)
                one = jnp.ones_like(v128)
                pv0 = jnp.dot(p[0:hq], jnp.where(lane_k, v128, one), preferred_element_type=F32)
                pv1 = jnp.dot(p[hq:2 * hq], jnp.where(lane_k, one, v128), preferred_element_type=F32)
                num = jnp.where(lane_lo, pv0, pv1)
                den = pltpu.roll(jnp.where(lane_lo, pv1, pv0), HEAD_DIM, 1)
                os_[c, pl.ds(q0 + r0, hq), sl] = (num / den).astype(BF16)
                lse_pair = jnp.where(lane_lo, m[0:hq], m[hq:2 * hq]) + jnp.log(den)
                lse_tile = jnp.where(lane_pair == hp, lse_pair, lse_tile)
            ls[c, pl.ds(q0 + r0, hq), :] = lse_tile
        return carry

    lax.fori_loop(0, r * nsub, unit, 0, unroll=2)

    if r == 1:
        o_ref[...] = os_[0]
        lse_ref[...] = ls[0]
    else:
        n = PERM // r
        for b in range(r * tl // PERM):
            cat = jnp.concatenate([os_[c, b * n:(b + 1) * n, :] for c in range(r)], axis=0)
            o_ref[b * PERM:(b + 1) * PERM, :] = jnp.dot(pt_ref[...], cat, preferred_element_type=F32).astype(BF16)
        for c in range(r):
            lse_ref[pl.ds(c, tl, stride=r), :] = ls[c]


def _attention_group(q, k, v, perm_t, r):
    b, _, l, _ = q.shape
    tl = min(ATT_CLASS_ROWS[r], l)
    ts = r * tl
    nh = tl // BAND
    last = l // BAND - 1
    cur = pl.BlockSpec((None, r, tl, ATT_WIDTH), lambda bi, i: (bi, 0, i, 0))
    prv = pl.BlockSpec((None, r, BAND, ATT_WIDTH), lambda bi, i: (bi, 0, jnp.maximum(i * nh - 1, 0), 0))
    nxt = pl.BlockSpec((None, r, BAND, ATT_WIDTH), lambda bi, i: (bi, 0, jnp.minimum((i + 1) * nh, last), 0))
    body = functools.partial(_attn_body, r=r, tl=tl, seq_l=l)
    return pl.pallas_call(
        body,
        grid=(b, l // tl),
        in_specs=[cur, prv, cur, nxt, prv, cur, nxt, pl.BlockSpec((PERM, PERM), lambda bi, i: (0, 0))],
        out_specs=[pl.BlockSpec((None, ts, ATT_WIDTH), lambda bi, i: (bi, i, 0)),
                   pl.BlockSpec((None, ts, V7X_LANES), lambda bi, i: (bi, i, 0))],
        out_shape=[jax.ShapeDtypeStruct((b, r * l, ATT_WIDTH), BF16),
                   jax.ShapeDtypeStruct((b, r * l, V7X_LANES), F32)],
        scratch_shapes=[
            pltpu.VMEM((r, tl + 2 * BAND, ATT_WIDTH), BF16),
            pltpu.VMEM((r, tl + 2 * BAND, ATT_WIDTH), BF16),
            pltpu.VMEM((r, tl, ATT_WIDTH), BF16),
            pltpu.VMEM((r, tl, V7X_LANES), F32),
        ],
        compiler_params=_cparams(("parallel", "parallel")),
        name=f"attn_r{r}",
    )(q, k, k, k, v, v, v, perm_t)


def _hgrn_chunk_fn(z_ref, q_ref, i_ref, lb_ref, lb_row, wsel_ref, o_ref, st_ref, rev):
    ch, sb = HGRN_CHUNK, HGRN_SUB
    row = lax.broadcasted_iota(jnp.int32, (ch, ch), 0)
    col = lax.broadcasted_iota(jnp.int32, (ch, ch), 1)
    dist = (col - row) if rev else (row - col)
    same = lambda n: (row // n) == (col // n)
    diag_mask = same(sb) & (dist >= 0)
    level_sizes = tuple(n for n in (1, 2, 4, 8, 16, 32) if n >= sb)
    level_masks = [same(2 * n) & jnp.logical_not(same(n)) & (dist > 0) for n in level_sizes]
    sl8 = V7X_SUBLANES
    row8 = lax.broadcasted_iota(jnp.int32, (sl8, HGRN_DK), 0)
    sub = lax.broadcasted_iota(jnp.int32, (ch, HGRN_DK), 0) % sb
    tau = (sb - 1 - sub) if rev else sub
    nt = (((1,), (1,)), ((), ()))
    tn = (((0,), (0,)), ((), ()))

    def roll8(x, k):
        return pltpu.roll(x.reshape(ch // sl8, sl8, HGRN_DK), k % sl8, 1).reshape(ch, HGRN_DK)

    prev = lambda x, k: roll8(x, -k if rev else k)
    nxt = lambda x, k: roll8(x, k if rev else -k)
    last_row = lambda start, n: start if rev else start + n - 1

    def widen(pf, sf, n):
        later_first = rev
        pf_rows, sf_rows = [], []
        for v in range(ch // sl8):
            x, y = pf[sl8 * v:sl8 * (v + 1)], sf[sl8 * v:sl8 * (v + 1)]
            if n >= sl8:
                blk = (sl8 * v) // n
                first = blk % 2 == 0
                sib = (blk + 1) * n if first else (blk - 1) * n
                r = last_row(sib, n)
                total = jnp.broadcast_to(pf[r:r + 1, :], (sl8, HGRN_DK))
                if first == later_first:
                    x = x * total
                else:
                    y = y * total
            else:
                pmul = jnp.ones((sl8, HGRN_DK), F32)
                smul = pmul
                for b0 in range(0, sl8, 2 * n):
                    lo_tot = jnp.broadcast_to(x[last_row(b0, n):last_row(b0, n) + 1, :], (sl8, HGRN_DK))
                    hi_tot = jnp.broadcast_to(x[last_row(b0 + n, n):last_row(b0 + n, n) + 1, :], (sl8, HGRN_DK))
                    in_lo = (row8 >= b0) & (row8 < b0 + n)
                    in_hi = (row8 >= b0 + n) & (row8 < b0 + 2 * n)
                    if later_first:
                        pmul, smul = jnp.where(in_lo, hi_tot, pmul), jnp.where(in_hi, lo_tot, smul)
                    else:
                        pmul, smul = jnp.where(in_hi, lo_tot, pmul), jnp.where(in_lo, hi_tot, smul)
                x, y = x * pmul, y * smul
            pf_rows.append(x)
            sf_rows.append(y)
        return jnp.concatenate(pf_rows, axis=0), jnp.concatenate(sf_rows, axis=0)

    def chunk(c):
        r0 = c * ch if isinstance(c, int) else pl.multiple_of(c * ch, ch)
        heads = range(HGRN_HEADS)
        sls = [slice(h * HGRN_DK, (h + 1) * HGRN_DK) for h in heads]
        qs, ivs, kfs, pfs, sfs, bands = [], [], [], [], [], []
        for sl in sls:
            z = z_ref[pl.ds(r0, ch), sl]
            q = q_ref[pl.ds(r0, ch), sl].astype(F32)
            lb = lb_ref[lb_row:lb_row + 1, sl]
            one_m_lb = 1.0 - lb
            e = jnp.exp(-jnp.abs(z))
            inv = 1.0 / (1.0 + e)
            pos = z >= 0
            f = lb + one_m_lb * (jnp.where(pos, 1.0, e) * inv)
            kf = one_m_lb * (jnp.where(pos, e, 1.0) * inv)
            pf = f
            sf = jnp.where(tau == sb - 1, 1.0, nxt(f, 1))
            for k in (k for k in (1, 2, 4) if k < sb):
                pf = pf * jnp.where(tau >= k, prev(pf, k), 1.0)
                sf = sf * jnp.where(tau <= sb - 1 - k, nxt(sf, k), 1.0)
            g = kf
            terms = [q * g]
            for d in range(1, sb):
                g = prev(g, 1) * f
                terms.append(q * g)
            qs.append(q); ivs.append(i_ref[pl.ds(r0, ch), sl]); kfs.append(kf); pfs.append(pf); sfs.append(sf)
            bands.append(jnp.concatenate(terms, axis=1).astype(BF16))

        bands = [jnp.dot(x, wsel_ref[...], preferred_element_type=F32) for x in bands]

        levels = []
        for n in level_sizes:
            levels.append([lax.dot_general((qs[h] * pfs[h]).astype(BF16), (kfs[h] * sfs[h]).astype(BF16), nt,
                                           preferred_element_type=F32) for h in heads])
            for h in heads:
                pfs[h], sfs[h] = widen(pfs[h], sfs[h], n)
        sts = [st_ref[h] for h in heads]
        inter = [lax.dot_general((qs[h] * pfs[h]).astype(BF16), sts[h].astype(BF16), nt, preferred_element_type=F32)
                 for h in heads]
        upd = [lax.dot_general(ivs[h], (kfs[h] * sfs[h]).astype(BF16), tn, preferred_element_type=F32)
               for h in heads]

        r_end = last_row(0, ch)
        for h in heads:
            skew = pltpu.roll(bands[h], 0, 1, stride=1, stride_axis=0)
            a = jnp.where(diag_mask, skew[:, 0:ch], 0.0)
            for lvl, mask in zip(levels, level_masks):
                a = jnp.where(mask, lvl[h], a)
            o = jnp.dot(a.astype(BF16), ivs[h], preferred_element_type=F32)
            o_ref[pl.ds(r0, ch), sls[h]] = (o + inter[h]).astype(o_ref.dtype)
            st_ref[h] = sts[h] * pfs[h][r_end:r_end + 1, :] + upd[h]

    return chunk


def _hgrn_body(z_ref, q_ref, i_ref, lb_ref, wsel_ref, o_ref, st_ref, *, rev, nch):
    @pl.when(pl.program_id(1) == 0)
    def _():
        st_ref[...] = jnp.zeros_like(st_ref)

    chunk = _hgrn_chunk_fn(z_ref, q_ref, i_ref, lb_ref, 0, wsel_ref, o_ref, st_ref, rev)

    def step(ci, carry):
        chunk((nch - 1 - ci) if rev else ci)
        return carry

    lax.fori_loop(0, nch, step, 0, unroll=4)


def _band_selector(rev):
    w = np.zeros((HGRN_SUB * HGRN_DK, V7X_LANES), np.float32)
    for d in range(HGRN_SUB):
        w[d * HGRN_DK:(d + 1) * HGRN_DK, d if rev else (V7X_LANES - d) % V7X_LANES] = 1.0
    return jnp.asarray(w, BF16)


def _hgrn_scan(z, bd, lb, rev):
    b, s, _ = z.shape
    ts = HGRN_TILE
    nb = s // ts
    order = (lambda j: nb - 1 - j) if rev else (lambda j: j)
    spec = lambda col: pl.BlockSpec((None, ts, HGRN_WIDTH), lambda bi, j: (bi, order(j), col))
    const = lambda shape: pl.BlockSpec(shape, lambda bi, j: (0, 0))
    body = functools.partial(_hgrn_body, rev=rev, nch=ts // HGRN_CHUNK)
    return pl.pallas_call(
        body,
        grid=(b, nb),
        in_specs=[spec(1 if rev else 0), spec(BD_Q), spec(BD_I), const((1, HGRN_WIDTH)),
                  const((HGRN_SUB * HGRN_DK, V7X_LANES))],
        out_specs=spec(0),
        out_shape=jax.ShapeDtypeStruct((b, s, HGRN_WIDTH), BF16),
        scratch_shapes=[pltpu.VMEM((HGRN_HEADS, HGRN_DK, HGRN_DK), F32)],
        compiler_params=_cparams(("parallel", "arbitrary")),
        name="hgrn_bwd" if rev else "hgrn_fwd",
    )(z, bd, bd, lb, _band_selector(rev))


def _fnet_a_body(u_ref, cs_ref, m_ref, o_ref, *, n1, nt):
    ngrp = FNET_WIDTH // FNET_GROUP_DIM
    for s in range(nt):
        zr, zi = [], []
        for g in range(ngrp):
            c0 = s * FNET_WIDTH + g * FNET_GROUP_DIM
            zz = jnp.dot(u_ref[:, c0:c0 + FNET_GROUP_DIM], cs_ref[...], preferred_element_type=F32)
            zr.append(zz[:, :FNET_GROUP_DIM])
            zi.append(zz[:, FNET_GROUP_DIM:])
        zcat = jnp.concatenate([jnp.concatenate(zr, axis=1), jnp.concatenate(zi, axis=1)], axis=0).astype(BF16)
        res = jnp.dot(m_ref[s], zcat, preferred_element_type=F32)
        o_ref[:, s * FNET_WIDTH:(s + 1) * FNET_WIDTH] = res.astype(o_ref.dtype)


def _fnet_c_body(b_ref, cs_ref, o_ref, *, kt):
    for k in range(kt):
        bcat = jnp.concatenate([b_ref[0, k], b_ref[1, k]], axis=0)
        x = jnp.dot(cs_ref[...], bcat, preferred_element_type=F32)
        o_ref[:, k * FNET_WIDTH:(k + 1) * FNET_WIDTH] = x.astype(o_ref.dtype)


def _fnet_tables(n1, n2):
    n = n1 * n2
    k = np.arange(FNET_GROUP_DIM)
    ang = 2.0 * np.pi * ((k[:, None] * k[None, :]) % FNET_GROUP_DIM) / FNET_GROUP_DIM
    cs_ch = np.concatenate([np.cos(ang), -np.sin(ang)], axis=1) / np.sqrt(FNET_GROUP_DIM)
    k1 = np.arange(n1)[None, :, None]
    m1 = np.arange(n1)[None, None, :]
    s2 = np.arange(n2)[:, None, None]
    idx = (k1 * m1 * n2 + s2 * k1) % n
    th = 2.0 * np.pi * idx / n
    gc, gs = np.cos(th) / np.sqrt(n1), np.sin(th) / np.sqrt(n1)
    m = np.concatenate([np.concatenate([gc, gs], axis=2), np.concatenate([-gs, gc], axis=2)], axis=1)
    k2 = np.arange(n2)
    ang2 = 2.0 * np.pi * ((k2[:, None] * k2[None, :]) % n2) / n2
    cs_seq = np.concatenate([np.cos(ang2), np.sin(ang2)], axis=1) / np.sqrt(n2)
    return (jnp.asarray(cs_ch, BF16), jnp.asarray(m, BF16), jnp.asarray(cs_seq, BF16))


def _fourier_branch(u, n1, n2):
    b, s, _ = u.shape
    cs_ch, m, cs_seq = _fnet_tables(n1, n2)
    nt = FNET_SLABS_PER_STEP
    kt = FNET_ROWS_PER_STEP
    u2 = u.reshape(b, n1, n2 * FNET_WIDTH)
    a = pl.pallas_call(
        functools.partial(_fnet_a_body, n1=n1, nt=nt),
        grid=(b, n2 // nt),
        in_specs=[
            pl.BlockSpec((None, n1, nt * FNET_WIDTH), lambda bi, j: (bi, 0, j)),
            pl.BlockSpec((FNET_GROUP_DIM, 2 * FNET_GROUP_DIM), lambda bi, j: (0, 0)),
            pl.BlockSpec((nt, 2 * n1, 2 * n1), lambda bi, j: (j, 0, 0)),
        ],
        out_specs=pl.BlockSpec((None, 2 * n1, nt * FNET_WIDTH), lambda bi, j: (bi, 0, j)),
        out_shape=jax.ShapeDtypeStruct((b, 2 * n1, n2 * FNET_WIDTH), BF16),
        compiler_params=_cparams(("parallel", "parallel")),
        name="fnet_a",
    )(u2, cs_ch, m)
    a5 = a.reshape(b, 2, n1, n2, FNET_WIDTH)
    y = pl.pallas_call(
        functools.partial(_fnet_c_body, kt=kt),
        grid=(b, n1 // kt),
        in_specs=[
            pl.BlockSpec((None, 2, kt, n2, FNET_WIDTH), lambda bi, j: (bi, 0, j, 0, 0)),
            pl.BlockSpec((n2, 2 * n2), lambda bi, j: (0, 0)),
        ],
        out_specs=pl.BlockSpec((None, n2, kt * FNET_WIDTH), lambda bi, j: (bi, 0, j)),
        out_shape=jax.ShapeDtypeStruct((b, n2, n1 * FNET_WIDTH), BF16),
        compiler_params=_cparams(("parallel", "parallel")),
        name="fnet_c",
    )(a5, cs_seq)
    return y.reshape(b, s, FNET_WIDTH)


def _merge_body(x_ref, o0_ref, o1_ref, o2_ref, l0_ref, l1_ref, l2_ref, hf_ref, hb_ref, gh_ref, y_ref,
                gm_ref, wg0_ref, wg1_ref, wg2_ref, wg3_ref, wg4_ref, wg5_ref,
                wa_ref, wh_ref, wf_ref, wo_ref, hg_ref, ex_ref, out_ref):
    lses = [l0_ref[...], l1_ref[...], l2_ref[...]]
    m = jnp.maximum(jnp.maximum(lses[0], lses[1]), lses[2])
    ws = [jnp.exp(l - m) for l in lses]
    inv = 1.0 / (ws[0] + ws[1] + ws[2])
    att = None
    for w, o_ref in zip(ws, (o0_ref, o1_ref, o2_ref)):
        alpha = jnp.dot((w * inv).astype(BF16), ex_ref[...], preferred_element_type=F32)
        term = alpha * o_ref[...].astype(F32)
        att = term if att is None else att + term
    a = jnp.dot(att.astype(BF16), wa_ref[...], preferred_element_type=F32)
    oh = hf_ref[...].astype(F32) + hb_ref[...].astype(F32)
    gh = gh_ref[...].astype(F32)
    heads = []
    for h in range(HGRN_HEADS):
        sl = slice(h * HGRN_DK, (h + 1) * HGRN_DK)
        gate = gh[:, sl]
        heads.append(_rms_rows(oh[:, sl], hg_ref[...]) * (gate * jax.nn.sigmoid(gate)))
    bh = jnp.dot(jnp.concatenate(heads, axis=1).astype(BF16), wh_ref[...], preferred_element_type=F32)
    c = jnp.dot(y_ref[...], wf_ref[...], preferred_element_type=F32)
    x = x_ref[...]
    hmix = _rms_rows(x, gm_ref[...]).astype(BF16)
    wg = (wg0_ref, wg1_ref, wg2_ref, wg3_ref, wg4_ref, wg5_ref)
    half = lambda w_ref: jax.nn.sigmoid(jnp.dot(hmix, w_ref[...], preferred_element_type=F32))
    gate = lambda k: jnp.concatenate([half(wg[2 * k]), half(wg[2 * k + 1])], axis=1)
    merged = gate(0) * a + gate(1) * bh + gate(2) * c
    out_ref[...] = x + jnp.dot(merged.astype(BF16), wo_ref[...], preferred_element_type=F32)


def _merge(x2d, att_outs, att_lses, hf, hb, bd, y, g_mix, w_in, wa, wh, wf, wo, hgain, expand, l):
    t, tm = x2d.shape[0], TOKEN_TILE
    row = lambda w: pl.BlockSpec((tm, w), lambda i: (i, 0))
    gate_w = lambda k: pl.BlockSpec((None, D_MODEL, PROJ_CHUNK), lambda i: (l, 0, GATE_COL0 // PROJ_CHUNK + k))
    in_specs = ([row(D_MODEL)] + [row(ATT_WIDTH)] * 3 + [row(V7X_LANES)] * 3 + [row(HGRN_WIDTH)] * 2
                + [pl.BlockSpec((tm, HGRN_WIDTH), lambda i: (i, BD_G)),
                   row(FNET_WIDTH), _of_layer(l, 1, D_MODEL)] + [gate_w(k) for k in range(GATE_COLS // PROJ_CHUNK)]
                + [_of_layer(l, ATT_WIDTH, D_MODEL), _of_layer(l, HGRN_WIDTH, D_MODEL),
                   _of_layer(l, FNET_WIDTH, D_MODEL), _of_layer(l, D_MODEL, D_MODEL), _of_layer(l, 1, HGRN_DK),
                   pl.BlockSpec((V7X_LANES, ATT_WIDTH), lambda i: (0, 0))])
    return pl.pallas_call(
        _merge_body,
        grid=(t // tm,),
        in_specs=in_specs,
        out_specs=row(D_MODEL),
        out_shape=jax.ShapeDtypeStruct((t, D_MODEL), F32),
        compiler_params=_cparams(("parallel",)),
        name="merge",
    )(x2d, *att_outs, *att_lses, hf, hb, bd, y, g_mix, *([w_in] * (GATE_COLS // PROJ_CHUNK)),
      wa, wh, wf, wo, hgain, expand)


def _trunk_consts(seq):
    half = HEAD_DIM // 2
    inv_freq = np.float32(ROPE_THETA) ** (-np.arange(half, dtype=np.float32) / np.float32(half))
    ang = np.arange(seq, dtype=np.float32)[:, None] * inv_freq[None, :]
    cos_nat, sin_nat = np.cos(ang.astype(np.float64)), np.sin(ang.astype(np.float64))
    rope = {}
    for _, r in WINDOW_DILATIONS:
        order = lambda tab: tab.reshape(seq // PERM, PERM // r, r, half).transpose(0, 2, 1, 3).reshape(seq, half)
        rope[r] = (jnp.asarray(order(cos_nat), F32), jnp.asarray(order(sin_nat), F32))
    hd = np.arange(V7X_MXU_DIM) // HEAD_DIM
    head_bd = jnp.asarray((hd[:, None] == hd[None, :]) / HEAD_DIM, BF16)
    perm, perm_t = {}, {}
    for _, r in WINDOW_DILATIONS:
        idx = np.arange(PERM)
        n = PERM // r
        p = np.zeros((PERM, PERM), np.float32)
        p[idx, (idx % n) * r + idx // n] = 1.0
        perm[r], perm_t[r] = jnp.asarray(p, BF16), jnp.asarray(p.T, BF16)
    expand = np.zeros((V7X_LANES, ATT_WIDTH), np.float32)
    for h in range(ATT_HEADS):
        expand[_lse_lane(h), h * HEAD_DIM:(h + 1) * HEAD_DIM] = 1.0
    return dict(rope=rope, head_bd=head_bd, perm=perm, perm_t=perm_t, expand=jnp.asarray(expand, BF16),
                n1=FNET_SLOW_LEN)


def _stacked_weights(hgrn_lb_logits, ffn1_norm, ffn1_w_gate, ffn1_w_up, ffn1_w_down, mix_norm, w_in, q_norm,
                     k_norm, w_att_out, hgrn_out_norm, w_hgrn_out, w_fnet_out, w_out, ffn2_norm, ffn2_w_gate,
                     ffn2_w_up, ffn2_w_down):
    bf = lambda w: w.astype(BF16)
    vec = lambda v: v.astype(F32)[:, None, :]
    heads_per_tile = ATT_WIDTH // HEAD_DIM
    p = jax.nn.softmax(hgrn_lb_logits.astype(F32), axis=0)
    cum = jnp.cumsum(p, axis=0)
    return dict(
        ffn1=(vec(ffn1_norm), bf(ffn1_w_gate), bf(ffn1_w_up), bf(ffn1_w_down)),
        ffn2=(vec(ffn2_norm), bf(ffn2_w_gate), bf(ffn2_w_up), bf(ffn2_w_down)),
        mix_norm=vec(mix_norm), w_in=bf(w_in),
        gq=vec(jnp.tile(q_norm, (1, heads_per_tile))), gk=vec(jnp.tile(k_norm, (1, heads_per_tile))),
        lb=cum - cum[0:1],
        wa=bf(w_att_out), wh=bf(w_hgrn_out), wf=bf(w_fnet_out), wo=bf(w_out), hgain=vec(hgrn_out_norm),
    )


def _encoder_layer(x, wts, consts, l):
    b, s, _ = x.shape
    t = b * s
    x2d = x.reshape(t, D_MODEL)
    x2d = _ffn(x2d, *wts["ffn1"], l)
    z, bd, u = _proj_rest(x2d, wts["mix_norm"], wts["w_in"], l)
    z3, bd3 = z.reshape(b, s, Z_COLS), bd.reshape(b, s, BD_COLS)
    qkv = _proj_qkv(x2d, wts["mix_norm"], wts["w_in"], wts["gq"], wts["gk"], consts, b, s, l)
    hf = _hgrn_scan(z3, bd3, wts["lb"][l, 0].reshape(1, -1), False).reshape(t, HGRN_WIDTH)
    hb = _hgrn_scan(z3, bd3, wts["lb"][l, 1].reshape(1, -1), True).reshape(t, HGRN_WIDTH)
    outs, lses = [], []
    for gi, (w, r) in enumerate(WINDOW_DILATIONS):
        assert w // (2 * r) == BAND
        o, lse = _attention_group(qkv[3 * gi], qkv[3 * gi + 1], qkv[3 * gi + 2], consts["perm_t"][r], r)
        outs.append(o.reshape(t, ATT_WIDTH))
        lses.append(lse.reshape(t, V7X_LANES))
    y = _fourier_branch(u.reshape(b, s, FNET_WIDTH), consts["n1"], s // consts["n1"]).reshape(t, FNET_WIDTH)
    x2d = _merge(x2d, outs, lses, hf, hb, bd, y, wts["mix_norm"], wts["w_in"], wts["wa"], wts["wh"], wts["wf"],
                 wts["wo"], wts["hgain"], consts["expand"], l)
    x2d = _ffn(x2d, *wts["ffn2"], l)
    return x2d.reshape(b, s, D_MODEL)


def kernel(x_prompt, x_sample, hgrn_lb_logits, ffn1_norm, ffn1_w_gate, ffn1_w_up, ffn1_w_down, mix_norm, w_in,
           q_norm, k_norm, w_att_out, hgrn_out_norm, w_hgrn_out, w_fnet_out, w_out, ffn2_norm, ffn2_w_gate,
           ffn2_w_up, ffn2_w_down):
    wts = _stacked_weights(hgrn_lb_logits, ffn1_norm, ffn1_w_gate, ffn1_w_up, ffn1_w_down, mix_norm, w_in, q_norm,
                           k_norm, w_att_out, hgrn_out_norm, w_hgrn_out, w_fnet_out, w_out, ffn2_norm,
                           ffn2_w_gate, ffn2_w_up, ffn2_w_down)
    consts = _trunk_consts(max(x_prompt.shape[1], x_sample.shape[1]))

    def trunk(x):
        for l in range(w_in.shape[0]):
            x = _encoder_layer(x, wts, consts, l)
        return x

    return (trunk(x_prompt), trunk(x_sample))
```

```python
import functools

import numpy as np
import jax
import jax.numpy as jnp
from jax import lax
from jax.experimental import pallas as pl
from jax.experimental.pallas import tpu as pltpu

F32 = jnp.float32
BF16 = jnp.bfloat16

D_MODEL = 1024
D_FF = 2816
HEAD_DIM = 64
ATT_HEADS = 8
WINDOW_DILATIONS = ((128, 1), (512, 4), (2048, 16))
N_GROUPS = len(WINDOW_DILATIONS)
ATT_WIDTH = ATT_HEADS * HEAD_DIM
ATT_COLS = N_GROUPS * ATT_WIDTH
ROPE_THETA = 10000.0
HGRN_HEADS = 4
HGRN_DK = 128
HGRN_WIDTH = 512
HGRN_CHUNK = 64
HGRN_SUB = 4
V7X_SUBLANES = 8
FNET_WIDTH = 512
FNET_GROUP_DIM = 128
EPS = 1e-6
NEG_INF = -1e30
IN_COLS = 10752
BAND = 64
PERM = 256

QKV_COLS = 3 * ATT_COLS
Z_COLS = 2 * HGRN_WIDTH
GATE_COLS = 3 * D_MODEL
BD_COLS = GATE_COLS + 3 * HGRN_WIDTH
BD_Q, BD_I, BD_G = (GATE_COLS // HGRN_WIDTH + k for k in range(3))

V7X_LANES = 128
V7X_MXU_DIM = 256
V7X_VMEM_LIMIT_BYTES = 56 * 1024 * 1024
PROJ_CHUNK = 512
ATT_UNIT_ROWS = 128
LSE_LANES_PER_HEAD = V7X_LANES // ATT_HEADS
FNET_SLOW_LEN = 128
FF_CHUNKS = ((0, 768), (768, 1536), (1536, 2304), (2304, 2816))
FFN_TOKEN_TILE = 1024
TOKEN_TILE = 512
ATT_CLASS_ROWS = {1: 2048, 4: 512, 16: 128}
HGRN_TILE = 1024
FNET_SLABS_PER_STEP = 8
FNET_ROWS_PER_STEP = 32
ATT_ROWS_PER_TILE = 64


def _cparams(sem):
    return pltpu.CompilerParams(dimension_semantics=sem, vmem_limit_bytes=V7X_VMEM_LIMIT_BYTES)


def _rms_rows(x, g):
    ms = jnp.mean(x * x, axis=-1, keepdims=True)
    return x * lax.rsqrt(ms + EPS) * g


def _ffn_body(x_ref, g_ref, wg_ref, wu_ref, wd_ref, o_ref):
    x = x_ref[...]
    h = _rms_rows(x, g_ref[...]).astype(BF16)
    acc = None
    for lo, hi in FF_CHUNKS:
        gate = jnp.dot(h, wg_ref[:, lo:hi], preferred_element_type=F32)
        up = jnp.dot(h, wu_ref[:, lo:hi], preferred_element_type=F32)
        a = (gate * jax.nn.sigmoid(gate) * up).astype(BF16)
        part = jnp.dot(a, wd_ref[lo:hi, :], preferred_element_type=F32)
        acc = part if acc is None else acc + part
    o_ref[...] = x + 0.5 * acc


def _of_layer(l, rows, cols, **kw):
    return pl.BlockSpec((None, rows, cols), lambda *_: (l, 0, 0), **kw)


def _ffn(x2d, g, wg, wu, wd, l):
    t, tm = x2d.shape[0], FFN_TOKEN_TILE
    one = pl.Buffered(1)
    return pl.pallas_call(
        _ffn_body,
        grid=(t // tm,),
        in_specs=[
            pl.BlockSpec((tm, D_MODEL), lambda i: (i, 0)),
            _of_layer(l, 1, D_MODEL),
            _of_layer(l, D_MODEL, D_FF, pipeline_mode=one),
            _of_layer(l, D_MODEL, D_FF, pipeline_mode=one),
            _of_layer(l, D_FF, D_MODEL, pipeline_mode=one),
        ],
        out_specs=pl.BlockSpec((tm, D_MODEL), lambda i: (i, 0)),
        out_shape=jax.ShapeDtypeStruct((t, D_MODEL), F32),
        compiler_params=_cparams(("parallel",)),
        name="ffn",
    )(x2d, g, wg, wu, wd)


def _head_norm_rope(acc, gain, cos, sin, bd_ref):
    tm = acc.shape[0]
    ln, mx, half = V7X_LANES, V7X_MXU_DIM, HEAD_DIM // 2
    lane = lax.broadcasted_iota(jnp.int32, (tm, ln), 1)
    first_half = (lane % HEAD_DIM) < half
    slabs = []
    for s in range(ATT_WIDTH // mx):
        a = acc[:, s * mx:(s + 1) * mx]
        ms = jnp.dot((a * a).astype(BF16), bd_ref[...], preferred_element_type=F32)
        y = a * lax.rsqrt(ms + EPS) * gain[:, s * mx:(s + 1) * mx]
        for t in range(mx // ln):
            y1 = y[:, t * ln:(t + 1) * ln]
            yr = jnp.where(first_half, pltpu.roll(y1, ln - half, 1), pltpu.roll(y1, half, 1))
            slabs.append(y1 * cos + yr * sin)
    return jnp.concatenate(slabs, axis=1)


def _qkv_pieces(x_ref, g_ref, w_ref, gq_ref, gk_ref, table_refs, bd_ref, p4_ref, p16_ref, out_refs):
    tm = x_ref.shape[0]
    h = _rms_rows(x_ref[...], g_ref[...]).astype(BF16)

    def permuted(p_ref):
        blocks = [jnp.dot(p_ref[...], h[b * PERM:(b + 1) * PERM], preferred_element_type=F32).astype(BF16)
                  for b in range(tm // PERM)]
        return jnp.concatenate(blocks, axis=0)

    lhs = (h, permuted(p4_ref), permuted(p16_ref))
    gains = (gq_ref[...] * HEAD_DIM ** -0.5, gk_ref[...])

    def lanes128(c_ref, s_ref):
        c, s = c_ref[...], s_ref[...]
        return jnp.concatenate([c, c], axis=1), jnp.concatenate([s, s], axis=1)

    tables = [lanes128(table_refs[2 * gi], table_refs[2 * gi + 1]) for gi in range(N_GROUPS)]

    def piece(gi, ti):
        def run():
            r = WINDOW_DILATIONS[gi][1]
            n = PERM // r
            col = ti * ATT_COLS + gi * ATT_WIDTH
            y = jnp.dot(lhs[gi], w_ref[:, col:col + ATT_WIDTH], preferred_element_type=F32)
            if ti < 2:
                y = _head_norm_rope(y, gains[ti], *tables[gi], bd_ref)
            y = y.astype(BF16)
            o_ref = out_refs[3 * gi + ti]
            if r == 1:
                o_ref[0] = y
            else:
                for b in range(tm // PERM):
                    for c in range(r):
                        o_ref[c, b * n:(b + 1) * n, :] = y[b * PERM + c * n:b * PERM + (c + 1) * n, :]
        return run

    return [piece(gi, ti) for gi in range(N_GROUPS) for ti in range(3)]


def _proj_qkv_body(x_ref, g_ref, w_ref, gq_ref, gk_ref, c1_ref, s1_ref, c4_ref, s4_ref, c16_ref, s16_ref,
                   bd_ref, p4_ref, p16_ref, *out_refs):
    tables = (c1_ref, s1_ref, c4_ref, s4_ref, c16_ref, s16_ref)
    for run in _qkv_pieces(x_ref, g_ref, w_ref, gq_ref, gk_ref, tables, bd_ref, p4_ref, p16_ref, out_refs):
        run()


def _proj_rest_body(x_ref, g_ref, w_ref, z_ref, bd_ref, u_ref):
    h = _rms_rows(x_ref[...], g_ref[...]).astype(BF16)
    tn = PROJ_CHUNK
    proj = lambda c0: jnp.dot(h, w_ref[:, c0:c0 + tn], preferred_element_type=F32)
    for k in range(Z_COLS // tn):
        z_ref[:, k * tn:(k + 1) * tn] = proj(QKV_COLS + k * tn)
    hgrn0 = QKV_COLS + Z_COLS
    u0 = hgrn0 + 3 * HGRN_WIDTH
    gate0 = u0 + FNET_WIDTH
    for k in range(GATE_COLS // tn):
        bd_ref[:, k * tn:(k + 1) * tn] = proj(gate0 + k * tn).astype(BF16)
    for k in range(3 * HGRN_WIDTH // tn):
        bd_ref[:, GATE_COLS + k * tn:GATE_COLS + (k + 1) * tn] = proj(hgrn0 + k * tn).astype(BF16)
    u_ref[...] = proj(u0).astype(BF16)


def _proj_qkv(x2d, g, w_in, gq, gk, consts, batch, seq, l):
    t, tm = x2d.shape[0], TOKEN_TILE
    nsb = seq // tm
    const = lambda i: (0, 0)
    tab = pl.BlockSpec((tm, HEAD_DIM), lambda i: (i % nsb, 0))
    in_specs = [
        pl.BlockSpec((tm, D_MODEL), lambda i: (i, 0)),
        _of_layer(l, 1, D_MODEL),
        _of_layer(l, D_MODEL, IN_COLS, pipeline_mode=pl.Buffered(1)),
        _of_layer(l, 1, ATT_WIDTH), _of_layer(l, 1, ATT_WIDTH),
        tab, tab, tab, tab, tab, tab,
        pl.BlockSpec((V7X_MXU_DIM, V7X_MXU_DIM), const),
        pl.BlockSpec((PERM, PERM), const), pl.BlockSpec((PERM, PERM), const),
    ]
    out_specs, out_shape = [], []
    for _, r in WINDOW_DILATIONS:
        for _ in range(3):
            out_specs.append(pl.BlockSpec((None, r, tm // r, ATT_WIDTH), lambda i: (i // nsb, 0, i % nsb, 0)))
            out_shape.append(jax.ShapeDtypeStruct((batch, r, seq // r, ATT_WIDTH), BF16))
    rope = consts["rope"]
    return pl.pallas_call(
        _proj_qkv_body,
        grid=(t // tm,),
        in_specs=in_specs,
        out_specs=out_specs,
        out_shape=out_shape,
        compiler_params=_cparams(("parallel",)),
        name="proj_qkv",
    )(x2d, g, w_in, gq, gk, rope[1][0], rope[1][1], rope[4][0], rope[4][1], rope[16][0], rope[16][1],
      consts["head_bd"], consts["perm"][4], consts["perm"][16])


def _proj_rest(x2d, g, w_in, l):
    t, tm = x2d.shape[0], TOKEN_TILE
    row = lambda w: pl.BlockSpec((tm, w), lambda i: (i, 0))
    return pl.pallas_call(
        _proj_rest_body,
        grid=(t // tm,),
        in_specs=[row(D_MODEL), _of_layer(l, 1, D_MODEL),
                  _of_layer(l, D_MODEL, IN_COLS, pipeline_mode=pl.Buffered(1))],
        out_specs=[row(Z_COLS), row(BD_COLS), row(FNET_WIDTH)],
        out_shape=[jax.ShapeDtypeStruct((t, Z_COLS), F32), jax.ShapeDtypeStruct((t, BD_COLS), BF16),
                   jax.ShapeDtypeStruct((t, FNET_WIDTH), BF16)],
        compiler_params=_cparams(("parallel",)),
        name="proj_rest",
    )(x2d, g, w_in)


def _lse_lane(h):
    return HEAD_DIM * (h % 2) + LSE_LANES_PER_HEAD * (h // 2)


def _attn_body(q_ref, kp_ref, kc_ref, kn_ref, vp_ref, vc_ref, vn_ref, pt_ref, o_ref, lse_ref,
               ks, vs, os_, ls, *, r, tl, seq_l):
    blk = pl.program_id(1)
    ln = V7X_LANES
    sq, kw = ATT_UNIT_ROWS, ATT_UNIT_ROWS + 2 * BAND
    ks[:, 0:BAND, :] = kp_ref[...]
    ks[:, BAND:BAND + tl, :] = kc_ref[...]
    ks[:, BAND + tl:2 * BAND + tl, :] = kn_ref[...]
    vs[:, 0:BAND, :] = vp_ref[...]
    vs[:, BAND:BAND + tl, :] = vc_ref[...]
    vs[:, BAND + tl:2 * BAND + tl, :] = vn_ref[...]

    nsub = tl // sq
    hq = ATT_ROWS_PER_TILE
    qi = lax.broadcasted_iota(jnp.int32, (sq, kw), 0)
    ki = lax.broadcasted_iota(jnp.int32, (sq, kw), 1)
    in_band = jnp.abs(ki - BAND - qi) <= BAND
    lane = lax.broadcasted_iota(jnp.int32, (hq, ln), 1)
    lane_lo = lane < HEAD_DIM
    lane_pair = (lane % HEAD_DIM) // LSE_LANES_PER_HEAD
    lane_k = lax.broadcasted_iota(jnp.int32, (kw, ln), 1) < HEAD_DIM
    nt = (((1,), (1,)), ((), ()))

    def unit(u, carry):
        c = u // nsub
        q0 = pl.multiple_of((u % nsub) * sq, sq)
        kpos = blk * tl + q0 - BAND + ki
        valid = in_band & (kpos >= 0) & (kpos < seq_l)
        q2 = q_ref[c, pl.ds(q0, sq), :]
        k2 = ks[c, pl.ds(q0, kw), :]
        v2 = vs[c, pl.ds(q0, kw), :]
        for half in range(sq // hq):
            r0, r1 = half * hq, (half + 1) * hq
            vmask = jnp.concatenate([valid[r0:r1], valid[r0:r1]], axis=0)
            lse_tile = jnp.zeros((hq, ln), F32)
            for hp in range(ATT_HEADS // 2):
                sl = slice(hp * ln, (hp + 1) * ln)
                q128, k128, v128 = q2[r0:r1, sl], k2[:, sl], v2[:, sl]
                zero = jnp.zeros_like(q128)
                qstack = jnp.concatenate([jnp.where(lane_lo, q128, zero), jnp.where(lane_lo, zero, q128)], axis=0)
                s = lax.dot_general(qstack, k128, nt, preferred_element_type=F32)
                s = jnp.where(vmask, s, NEG_INF)
                m = jnp.max(s, axis=-1, keepdims=True)
                p = jnp.exp(s - m).astype(BF16)
                one = jnp.ones_like(v128)
                pv0 = jnp.dot(p[0:hq], jnp.where(lane_k, v128, one), preferred_element_type=F32)
                pv1 = jnp.dot(p[hq:2 * hq], jnp.where(lane_k, one, v128), preferred_element_type=F32)
                num = jnp.where(lane_lo, pv0, pv1)
                den = pltpu.roll(jnp.where(lane_lo, pv1, pv0), HEAD_DIM, 1)
                os_[c, pl.ds(q0 + r0, hq), sl] = (num / den).astype(BF16)
                lse_pair = jnp.where(lane_lo, m[0:hq], m[hq:2 * hq]) + jnp.log(den)
                lse_tile = jnp.where(lane_pair == hp, lse_pair, lse_tile)
            ls[c, pl.ds(q0 + r0, hq), :] = lse_tile
        return carry

    lax.fori_loop(0, r * nsub, unit, 0, unroll=2)

    if r == 1:
        o_ref[...] = os_[0]
        lse_ref[...] = ls[0]
    else:
        n = PERM // r
        for b in range(r * tl // PERM):
            cat = jnp.concatenate([os_[c, b * n:(b + 1) * n, :] for c in range(r)], axis=0)
            o_ref[b * PERM:(b + 1) * PERM, :] = jnp.dot(pt_ref[...], cat, preferred_element_type=F32).astype(BF16)
        for c in range(r):
            lse_ref[pl.ds(c, tl, stride=r), :] = ls[c]


def _attention_group(q, k, v, perm_t, r):
    b, _, l, _ = q.shape
    tl = min(ATT_CLASS_ROWS[r], l)
    ts = r * tl
    nh = tl // BAND
    last = l // BAND - 1
    cur = pl.BlockSpec((None, r, tl, ATT_WIDTH), lambda bi, i: (bi, 0, i, 0))
    prv = pl.BlockSpec((None, r, BAND, ATT_WIDTH), lambda bi, i: (bi, 0, jnp.maximum(i * nh - 1, 0), 0))
    nxt = pl.BlockSpec((None, r, BAND, ATT_WIDTH), lambda bi, i: (bi, 0, jnp.minimum((i + 1) * nh, last), 0))
    body = functools.partial(_attn_body, r=r, tl=tl, seq_l=l)
    return pl.pallas_call(
        body,
        grid=(b, l // tl),
        in_specs=[cur, prv, cur, nxt, prv, cur, nxt, pl.BlockSpec((PERM, PERM), lambda bi, i: (0, 0))],
        out_specs=[pl.BlockSpec((None, ts, ATT_WIDTH), lambda bi, i: (bi, i, 0)),
                   pl.BlockSpec((None, ts, V7X_LANES), lambda bi, i: (bi, i, 0))],
        out_shape=[jax.ShapeDtypeStruct((b, r * l, ATT_WIDTH), BF16),
                   jax.ShapeDtypeStruct((b, r * l, V7X_LANES), F32)],
        scratch_shapes=[
            pltpu.VMEM((r, tl + 2 * BAND, ATT_WIDTH), BF16),
            pltpu.VMEM((r, tl + 2 * BAND, ATT_WIDTH), BF16),
            pltpu.VMEM((r, tl, ATT_WIDTH), BF16),
            pltpu.VMEM((r, tl, V7X_LANES), F32),
        ],
        compiler_params=_cparams(("parallel", "parallel")),
        name=f"attn_r{r}",
    )(q, k, k, k, v, v, v, perm_t)


def _hgrn_chunk_fn(z_ref, q_ref, i_ref, lb_ref, lb_row, wsel_ref, o_ref, st_ref, rev):
    ch, sb = HGRN_CHUNK, HGRN_SUB
    row = lax.broadcasted_iota(jnp.int32, (ch, ch), 0)
    col = lax.broadcasted_iota(jnp.int32, (ch, ch), 1)
    dist = (col - row) if rev else (row - col)
    same = lambda n: (row // n) == (col // n)
    diag_mask = same(sb) & (dist >= 0)
    level_sizes = tuple(n for n in (1, 2, 4, 8, 16, 32) if n >= sb)
    level_masks = [same(2 * n) & jnp.logical_not(same(n)) & (dist > 0) for n in level_sizes]
    sl8 = V7X_SUBLANES
    row8 = lax.broadcasted_iota(jnp.int32, (sl8, HGRN_DK), 0)
    sub = lax.broadcasted_iota(jnp.int32, (ch, HGRN_DK), 0) % sb
    tau = (sb - 1 - sub) if rev else sub
    nt = (((1,), (1,)), ((), ()))
    tn = (((0,), (0,)), ((), ()))

    def roll8(x, k):
        return pltpu.roll(x.reshape(ch // sl8, sl8, HGRN_DK), k % sl8, 1).reshape(ch, HGRN_DK)

    prev = lambda x, k: roll8(x, -k if rev else k)
    nxt = lambda x, k: roll8(x, k if rev else -k)
    last_row = lambda start, n: start if rev else start + n - 1

    def widen(pf, sf, n):
        later_first = rev
        pf_rows, sf_rows = [], []
        for v in range(ch // sl8):
            x, y = pf[sl8 * v:sl8 * (v + 1)], sf[sl8 * v:sl8 * (v + 1)]
            if n >= sl8:
                blk = (sl8 * v) // n
                first = blk % 2 == 0
                sib = (blk + 1) * n if first else (blk - 1) * n
                r = last_row(sib, n)
                total = jnp.broadcast_to(pf[r:r + 1, :], (sl8, HGRN_DK))
                if first == later_first:
                    x = x * total
                else:
                    y = y * total
            else:
                pmul = jnp.ones((sl8, HGRN_DK), F32)
                smul = pmul
                for b0 in range(0, sl8, 2 * n):
                    lo_tot = jnp.broadcast_to(x[last_row(b0, n):last_row(b0, n) + 1, :], (sl8, HGRN_DK))
                    hi_tot = jnp.broadcast_to(x[last_row(b0 + n, n):last_row(b0 + n, n) + 1, :], (sl8, HGRN_DK))
                    in_lo = (row8 >= b0) & (row8 < b0 + n)
                    in_hi = (row8 >= b0 + n) & (row8 < b0 + 2 * n)
                    if later_first:
                        pmul, smul = jnp.where(in_lo, hi_tot, pmul), jnp.where(in_hi, lo_tot, smul)
                    else:
                        pmul, smul = jnp.where(in_hi, lo_tot, pmul), jnp.where(in_lo, hi_tot, smul)
                x, y = x * pmul, y * smul
            pf_rows.append(x)
            sf_rows.append(y)
        return jnp.concatenate(pf_rows, axis=0), jnp.concatenate(sf_rows, axis=0)

    def chunk(c):
        r0 = c * ch if isinstance(c, int) else pl.multiple_of(c * ch, ch)
        heads = range(HGRN_HEADS)
        sls = [slice(h * HGRN_DK, (h + 1) * HGRN_DK) for h in heads]
        qs, ivs, kfs, pfs, sfs, bands = [], [], [], [], [], []
        for sl in sls:
            z = z_ref[pl.ds(r0, ch), sl]
            q = q_ref[pl.ds(r0, ch), sl].astype(F32)
            lb = lb_ref[lb_row:lb_row + 1, sl]
            one_m_lb = 1.0 - lb
            e = jnp.exp(-jnp.abs(z))
            inv = 1.0 / (1.0 + e)
            pos = z >= 0
            f = lb + one_m_lb * (jnp.where(pos, 1.0, e) * inv)
            kf = one_m_lb * (jnp.where(pos, e, 1.0) * inv)
            pf = f
            sf = jnp.where(tau == sb - 1, 1.0, nxt(f, 1))
            for k in (k for k in (1, 2, 4) if k < sb):
                pf = pf * jnp.where(tau >= k, prev(pf, k), 1.0)
                sf = sf * jnp.where(tau <= sb - 1 - k, nxt(sf, k), 1.0)
            g = kf
            terms = [q * g]
            for d in range(1, sb):
                g = prev(g, 1) * f
                terms.append(q * g)
            qs.append(q); ivs.append(i_ref[pl.ds(r0, ch), sl]); kfs.append(kf); pfs.append(pf); sfs.append(sf)
            bands.append(jnp.concatenate(terms, axis=1).astype(BF16))

        bands = [jnp.dot(x, wsel_ref[...], preferred_element_type=F32) for x in bands]

        levels = []
        for n in level_sizes:
            levels.append([lax.dot_general((qs[h] * pfs[h]).astype(BF16), (kfs[h] * sfs[h]).astype(BF16), nt,
                                           preferred_element_type=F32) for h in heads])
            for h in heads:
                pfs[h], sfs[h] = widen(pfs[h], sfs[h], n)
        sts = [st_ref[h] for h in heads]
        inter = [lax.dot_general((qs[h] * pfs[h]).astype(BF16), sts[h].astype(BF16), nt, preferred_element_type=F32)
                 for h in heads]
        upd = [lax.dot_general(ivs[h], (kfs[h] * sfs[h]).astype(BF16), tn, preferred_element_type=F32)
               for h in heads]

        r_end = last_row(0, ch)
        for h in heads:
            skew = pltpu.roll(bands[h], 0, 1, stride=1, stride_axis=0)
            a = jnp.where(diag_mask, skew[:, 0:ch], 0.0)
            for lvl, mask in zip(levels, level_masks):
                a = jnp.where(mask, lvl[h], a)
            o = jnp.dot(a.astype(BF16), ivs[h], preferred_element_type=F32)
            o_ref[pl.ds(r0, ch), sls[h]] = (o + inter[h]).astype(o_ref.dtype)
            st_ref[h] = sts[h] * pfs[h][r_end:r_end + 1, :] + upd[h]

    return chunk


def _hgrn_body(z_ref, q_ref, i_ref, lb_ref, wsel_ref, o_ref, st_ref, *, rev, nch):
    @pl.when(pl.program_id(1) == 0)
    def _():
        st_ref[...] = jnp.zeros_like(st_ref)

    chunk = _hgrn_chunk_fn(z_ref, q_ref, i_ref, lb_ref, 0, wsel_ref, o_ref, st_ref, rev)

    def step(ci, carry):
        chunk((nch - 1 - ci) if rev else ci)
        return carry

    lax.fori_loop(0, nch, step, 0, unroll=4)


def _band_selector(rev):
    w = np.zeros((HGRN_SUB * HGRN_DK, V7X_LANES), np.float32)
    for d in range(HGRN_SUB):
        w[d * HGRN_DK:(d + 1) * HGRN_DK, d if rev else (V7X_LANES - d) % V7X_LANES] = 1.0
    return jnp.asarray(w, BF16)


def _hgrn_scan(z, bd, lb, rev):
    b, s, _ = z.shape
    ts = HGRN_TILE
    nb = s // ts
    order = (lambda j: nb - 1 - j) if rev else (lambda j: j)
    spec = lambda col: pl.BlockSpec((None, ts, HGRN_WIDTH), lambda bi, j: (bi, order(j), col))
    const = lambda shape: pl.BlockSpec(shape, lambda bi, j: (0, 0))
    body = functools.partial(_hgrn_body, rev=rev, nch=ts // HGRN_CHUNK)
    return pl.pallas_call(
        body,
        grid=(b, nb),
        in_specs=[spec(1 if rev else 0), spec(BD_Q), spec(BD_I), const((1, HGRN_WIDTH)),
                  const((HGRN_SUB * HGRN_DK, V7X_LANES))],
        out_specs=spec(0),
        out_shape=jax.ShapeDtypeStruct((b, s, HGRN_WIDTH), BF16),
        scratch_shapes=[pltpu.VMEM((HGRN_HEADS, HGRN_DK, HGRN_DK), F32)],
        compiler_params=_cparams(("parallel", "arbitrary")),
        name="hgrn_bwd" if rev else "hgrn_fwd",
    )(z, bd, bd, lb, _band_selector(rev))


def _fnet_a_body(u_ref, cs_ref, m_ref, o_ref, *, n1, nt):
    ngrp = FNET_WIDTH // FNET_GROUP_DIM
    for s in range(nt):
        zr, zi = [], []
        for g in range(ngrp):
            c0 = s * FNET_WIDTH + g * FNET_GROUP_DIM
            zz = jnp.dot(u_ref[:, c0:c0 + FNET_GROUP_DIM], cs_ref[...], preferred_element_type=F32)
            zr.append(zz[:, :FNET_GROUP_DIM])
            zi.append(zz[:, FNET_GROUP_DIM:])
        zcat = jnp.concatenate([jnp.concatenate(zr, axis=1), jnp.concatenate(zi, axis=1)], axis=0).astype(BF16)
        res = jnp.dot(m_ref[s], zcat, preferred_element_type=F32)
        o_ref[:, s * FNET_WIDTH:(s + 1) * FNET_WIDTH] = res.astype(o_ref.dtype)


def _fnet_c_body(b_ref, cs_ref, o_ref, *, kt):
    for k in range(kt):
        bcat = jnp.concatenate([b_ref[0, k], b_ref[1, k]], axis=0)
        x = jnp.dot(cs_ref[...], bcat, preferred_element_type=F32)
        o_ref[:, k * FNET_WIDTH:(k + 1) * FNET_WIDTH] = x.astype(o_ref.dtype)


def _fnet_tables(n1, n2):
    n = n1 * n2
    k = np.arange(FNET_GROUP_DIM)
    ang = 2.0 * np.pi * ((k[:, None] * k[None, :]) % FNET_GROUP_DIM) / FNET_GROUP_DIM
    cs_ch = np.concatenate([np.cos(ang), -np.sin(ang)], axis=1) / np.sqrt(FNET_GROUP_DIM)
    k1 = np.arange(n1)[None, :, None]
    m1 = np.arange(n1)[None, None, :]
    s2 = np.arange(n2)[:, None, None]
    idx = (k1 * m1 * n2 + s2 * k1) % n
    th = 2.0 * np.pi * idx / n
    gc, gs = np.cos(th) / np.sqrt(n1), np.sin(th) / np.sqrt(n1)
    m = np.concatenate([np.concatenate([gc, gs], axis=2), np.concatenate([-gs, gc], axis=2)], axis=1)
    k2 = np.arange(n2)
    ang2 = 2.0 * np.pi * ((k2[:, None] * k2[None, :]) % n2) / n2
    cs_seq = np.concatenate([np.cos(ang2), np.sin(ang2)], axis=1) / np.sqrt(n2)
    return (jnp.asarray(cs_ch, BF16), jnp.asarray(m, BF16), jnp.asarray(cs_seq, BF16))


def _fourier_branch(u, n1, n2):
    b, s, _ = u.shape
    cs_ch, m, cs_seq = _fnet_tables(n1, n2)
    nt = FNET_SLABS_PER_STEP
    kt = FNET_ROWS_PER_STEP
    u2 = u.reshape(b, n1, n2 * FNET_WIDTH)
    a = pl.pallas_call(
        functools.partial(_fnet_a_body, n1=n1, nt=nt),
        grid=(b, n2 // nt),
        in_specs=[
            pl.BlockSpec((None, n1, nt * FNET_WIDTH), lambda bi, j: (bi, 0, j)),
            pl.BlockSpec((FNET_GROUP_DIM, 2 * FNET_GROUP_DIM), lambda bi, j: (0, 0)),
            pl.BlockSpec((nt, 2 * n1, 2 * n1), lambda bi, j: (j, 0, 0)),
        ],
        out_specs=pl.BlockSpec((None, 2 * n1, nt * FNET_WIDTH), lambda bi, j: (bi, 0, j)),
        out_shape=jax.ShapeDtypeStruct((b, 2 * n1, n2 * FNET_WIDTH), BF16),
        compiler_params=_cparams(("parallel", "parallel")),
        name="fnet_a",
    )(u2, cs_ch, m)
    a5 = a.reshape(b, 2, n1, n2, FNET_WIDTH)
    y = pl.pallas_call(
        functools.partial(_fnet_c_body, kt=kt),
        grid=(b, n1 // kt),
        in_specs=[
            pl.BlockSpec((None, 2, kt, n2, FNET_WIDTH), lambda bi, j: (bi, 0, j, 0, 0)),
            pl.BlockSpec((n2, 2 * n2), lambda bi, j: (0, 0)),
        ],
        out_specs=pl.BlockSpec((None, n2, kt * FNET_WIDTH), lambda bi, j: (bi, 0, j)),
        out_shape=jax.ShapeDtypeStruct((b, n2, n1 * FNET_WIDTH), BF16),
        compiler_params=_cparams(("parallel", "parallel")),
        name="fnet_c",
    )(a5, cs_seq)
    return y.reshape(b, s, FNET_WIDTH)


def _merge_body(x_ref, o0_ref, o1_ref, o2_ref, l0_ref, l1_ref, l2_ref, hf_ref, hb_ref, gh_ref, y_ref,
                ga_ref, gb_ref, gc_ref, wa_ref, wh_ref, wf_ref, wo_ref, hg_ref, ex_ref, out_ref):
    lses = [l0_ref[...], l1_ref[...], l2_ref[...]]
    m = jnp.maximum(jnp.maximum(lses[0], lses[1]), lses[2])
    ws = [jnp.exp(l - m) for l in lses]
    inv = 1.0 / (ws[0] + ws[1] + ws[2])
    att = None
    for w, o_ref in zip(ws, (o0_ref, o1_ref, o2_ref)):
        alpha = jnp.dot((w * inv).astype(BF16), ex_ref[...], preferred_element_type=F32)
        term = alpha * o_ref[...].astype(F32)
        att = term if att is None else att + term
    a = jnp.dot(att.astype(BF16), wa_ref[...], preferred_element_type=F32)
    oh = hf_ref[...].astype(F32) + hb_ref[...].astype(F32)
    gh = gh_ref[...].astype(F32)
    heads = []
    for h in range(HGRN_HEADS):
        sl = slice(h * HGRN_DK, (h + 1) * HGRN_DK)
        gate = gh[:, sl]
        heads.append(_rms_rows(oh[:, sl], hg_ref[...]) * (gate * jax.nn.sigmoid(gate)))
    bh = jnp.dot(jnp.concatenate(heads, axis=1).astype(BF16), wh_ref[...], preferred_element_type=F32)
    c = jnp.dot(y_ref[...], wf_ref[...], preferred_element_type=F32)
    sig = lambda ref: jax.nn.sigmoid(ref[...].astype(F32))
    merged = sig(ga_ref) * a + sig(gb_ref) * bh + sig(gc_ref) * c
    out_ref[...] = x_ref[...] + jnp.dot(merged.astype(BF16), wo_ref[...], preferred_element_type=F32)


def _merge(x2d, att_outs, att_lses, hf, hb, bd, y, wa, wh, wf, wo, hgain, expand, l):
    t, tm = x2d.shape[0], TOKEN_TILE
    row = lambda w: pl.BlockSpec((tm, w), lambda i: (i, 0))
    gate = lambda k: pl.BlockSpec((tm, D_MODEL), lambda i: (i, k))
    in_specs = ([row(D_MODEL)] + [row(ATT_WIDTH)] * 3 + [row(V7X_LANES)] * 3 + [row(HGRN_WIDTH)] * 2
                + [pl.BlockSpec((tm, HGRN_WIDTH), lambda i: (i, BD_G)),
                   row(FNET_WIDTH), gate(0), gate(1), gate(2),
                   _of_layer(l, ATT_WIDTH, D_MODEL), _of_layer(l, HGRN_WIDTH, D_MODEL),
                   _of_layer(l, FNET_WIDTH, D_MODEL), _of_layer(l, D_MODEL, D_MODEL), _of_layer(l, 1, HGRN_DK),
                   pl.BlockSpec((V7X_LANES, ATT_WIDTH), lambda i: (0, 0))])
    return pl.pallas_call(
        _merge_body,
        grid=(t // tm,),
        in_specs=in_specs,
        out_specs=row(D_MODEL),
        out_shape=jax.ShapeDtypeStruct((t, D_MODEL), F32),
        compiler_params=_cparams(("parallel",)),
        name="merge",
    )(x2d, *att_outs, *att_lses, hf, hb, bd, y, bd, bd, bd, wa, wh, wf, wo, hgain, expand)


def _trunk_consts(seq):
    half = HEAD_DIM // 2
    inv_freq = np.float32(ROPE_THETA) ** (-np.arange(half, dtype=np.float32) / np.float32(half))
    ang = np.arange(seq, dtype=np.float32)[:, None] * inv_freq[None, :]
    cos_nat, sin_nat = np.cos(ang.astype(np.float64)), np.sin(ang.astype(np.float64))
    rope = {}
    for _, r in WINDOW_DILATIONS:
        order = lambda tab: tab.reshape(seq // PERM, PERM // r, r, half).transpose(0, 2, 1, 3).reshape(seq, half)
        cos, sin = order(cos_nat), order(sin_nat)
        rope[r] = (jnp.asarray(np.concatenate([cos, cos], axis=1), F32),
                   jnp.asarray(np.concatenate([-sin, sin], axis=1), F32))
    hd = np.arange(V7X_MXU_DIM) // HEAD_DIM
    head_bd = jnp.asarray((hd[:, None] == hd[None, :]) / HEAD_DIM, BF16)
    perm, perm_t = {}, {}
    for _, r in WINDOW_DILATIONS:
        idx = np.arange(PERM)
        n = PERM // r
        p = np.zeros((PERM, PERM), np.float32)
        p[idx, (idx % n) * r + idx // n] = 1.0
        perm[r], perm_t[r] = jnp.asarray(p, BF16), jnp.asarray(p.T, BF16)
    expand = np.zeros((V7X_LANES, ATT_WIDTH), np.float32)
    for h in range(ATT_HEADS):
        expand[_lse_lane(h), h * HEAD_DIM:(h + 1) * HEAD_DIM] = 1.0
    return dict(rope=rope, head_bd=head_bd, perm=perm, perm_t=perm_t, expand=jnp.asarray(expand, BF16),
                n1=FNET_SLOW_LEN)


def _stacked_weights(hgrn_lb_logits, ffn1_norm, ffn1_w_gate, ffn1_w_up, ffn1_w_down, mix_norm, w_in, q_norm,
                     k_norm, w_att_out, hgrn_out_norm, w_hgrn_out, w_fnet_out, w_out, ffn2_norm, ffn2_w_gate,
                     ffn2_w_up, ffn2_w_down):
    bf = lambda w: w.astype(BF16)
    vec = lambda v: v.astype(F32)[:, None, :]
    heads_per_tile = ATT_WIDTH // HEAD_DIM
    p = jax.nn.softmax(hgrn_lb_logits.astype(F32), axis=0)
    cum = jnp.cumsum(p, axis=0)
    return dict(
        ffn1=(vec(ffn1_norm), bf(ffn1_w_gate), bf(ffn1_w_up), bf(ffn1_w_down)),
        ffn2=(vec(ffn2_norm), bf(ffn2_w_gate), bf(ffn2_w_up), bf(ffn2_w_down)),
        mix_norm=vec(mix_norm), w_in=bf(w_in),
        gq=vec(jnp.tile(q_norm, (1, heads_per_tile))), gk=vec(jnp.tile(k_norm, (1, heads_per_tile))),
        lb=cum - cum[0:1],
        wa=bf(w_att_out), wh=bf(w_hgrn_out), wf=bf(w_fnet_out), wo=bf(w_out), hgain=vec(hgrn_out_norm),
    )


def _encoder_layer(x, wts, consts, l):
    b, s, _ = x.shape
    t = b * s
    x2d = x.reshape(t, D_MODEL)
    x2d = _ffn(x2d, *wts["ffn1"], l)
    z, bd, u = _proj_rest(x2d, wts["mix_norm"], wts["w_in"], l)
    z3, bd3 = z.reshape(b, s, Z_COLS), bd.reshape(b, s, BD_COLS)
    qkv = _proj_qkv(x2d, wts["mix_norm"], wts["w_in"], wts["gq"], wts["gk"], consts, b, s, l)
    hf = _hgrn_scan(z3, bd3, wts["lb"][l, 0].reshape(1, -1), False).reshape(t, HGRN_WIDTH)
    hb = _hgrn_scan(z3, bd3, wts["lb"][l, 1].reshape(1, -1), True).reshape(t, HGRN_WIDTH)
    outs, lses = [], []
    for gi, (w, r) in enumerate(WINDOW_DILATIONS):
        assert w // (2 * r) == BAND
        o, lse = _attention_group(qkv[3 * gi], qkv[3 * gi + 1], qkv[3 * gi + 2], consts["perm_t"][r], r)
        outs.append(o.reshape(t, ATT_WIDTH))
        lses.append(lse.reshape(t, V7X_LANES))
    y = _fourier_branch(u.reshape(b, s, FNET_WIDTH), consts["n1"], s // consts["n1"]).reshape(t, FNET_WIDTH)
    x2d = _merge(x2d, outs, lses, hf, hb, bd, y, wts["wa"], wts["wh"], wts["wf"], wts["wo"], wts["hgain"],
                 consts["expand"], l)
    x2d = _ffn(x2d, *wts["ffn2"], l)
    return x2d.reshape(b, s, D_MODEL)


def kernel(x_prompt, x_sample, hgrn_lb_logits, ffn1_norm, ffn1_w_gate, ffn1_w_up, ffn1_w_down, mix_norm, w_in,
           q_norm, k_norm, w_att_out, hgrn_out_norm, w_hgrn_out, w_fnet_out, w_out, ffn2_norm, ffn2_w_gate,
           ffn2_w_up, ffn2_w_down):
    wts = _stacked_weights(hgrn_lb_logits, ffn1_norm, ffn1_w_gate, ffn1_w_up, ffn1_w_down, mix_norm, w_in, q_norm,
                           k_norm, w_att_out, hgrn_out_norm, w_hgrn_out, w_fnet_out, w_out, ffn2_norm,
                           ffn2_w_gate, ffn2_w_up, ffn2_w_down)
    consts = _trunk_consts(max(x_prompt.shape[1], x_sample.shape[1]))

    def trunk(x):
        for l in range(w_in.shape[0]):
            x = _encoder_layer(x, wts, consts, l)
        return x

    return (trunk(x_prompt), trunk(x_sample))
```

```python
import functools

import numpy as np
import jax
import jax.numpy as jnp
from jax import lax
from jax.experimental import pallas as pl
from jax.experimental.pallas import tpu as pltpu

F32 = jnp.float32
BF16 = jnp.bfloat16

D_MODEL = 1024
D_FF = 2816
HEAD_DIM = 64
ATT_HEADS = 8
WINDOW_DILATIONS = ((128, 1), (512, 4), (2048, 16))
N_GROUPS = len(WINDOW_DILATIONS)
ATT_WIDTH = ATT_HEADS * HEAD_DIM
ATT_COLS = N_GROUPS * ATT_WIDTH
ROPE_THETA = 10000.0
HGRN_HEADS = 4
HGRN_DK = 128
HGRN_WIDTH = 512
HGRN_CHUNK = 64
HGRN_SUB = 4
V7X_SUBLANES = 8
FNET_WIDTH = 512
FNET_GROUP_DIM = 128
EPS = 1e-6
NEG_INF = -1e30
IN_COLS = 10752
BAND = 64
PERM = 256

QKV_COLS = 3 * ATT_COLS
Z_COLS = 2 * HGRN_WIDTH
GATE_COLS = 3 * D_MODEL
BD_COLS = GATE_COLS + 3 * HGRN_WIDTH
BD_Q, BD_I, BD_G = (GATE_COLS // HGRN_WIDTH + k for k in range(3))

V7X_LANES = 128
V7X_MXU_DIM = 256
V7X_VMEM_LIMIT_BYTES = 56 * 1024 * 1024
PROJ_CHUNK = 512
ATT_UNIT_ROWS = 128
LSE_LANES_PER_HEAD = V7X_LANES // ATT_HEADS
FNET_SLOW_LEN = 128
FF_CHUNKS = ((0, 768), (768, 1536), (1536, 2304), (2304, 2816))
FFN_TOKEN_TILE = 1024
TOKEN_TILE = 512
ATT_CLASS_ROWS = {1: 2048, 4: 512, 16: 128}
HGRN_TILE = 1024
FNET_SLABS_PER_STEP = 8
FNET_ROWS_PER_STEP = 32
ATT_ROWS_PER_TILE = 64


def _cparams(sem):
    return pltpu.CompilerParams(dimension_semantics=sem, vmem_limit_bytes=V7X_VMEM_LIMIT_BYTES)


def _rms_rows(x, g):
    ms = jnp.mean(x * x, axis=-1, keepdims=True)
    return x * lax.rsqrt(ms + EPS) * g


def _ffn_body(x_ref, g_ref, wg_ref, wu_ref, wd_ref, o_ref):
    x = x_ref[...]
    h = _rms_rows(x, g_ref[...]).astype(BF16)
    acc = None
    for lo, hi in FF_CHUNKS:
        gate = jnp.dot(h, wg_ref[:, lo:hi], preferred_element_type=F32)
        up = jnp.dot(h, wu_ref[:, lo:hi], preferred_element_type=F32)
        a = (gate * jax.nn.sigmoid(gate) * up).astype(BF16)
        part = jnp.dot(a, wd_ref[lo:hi, :], preferred_element_type=F32)
        acc = part if acc is None else acc + part
    o_ref[...] = x + 0.5 * acc


def _of_layer(l, rows, cols, **kw):
    return pl.BlockSpec((None, rows, cols), lambda *_: (l, 0, 0), **kw)


def _ffn(x2d, g, wg, wu, wd, l):
    t, tm = x2d.shape[0], FFN_TOKEN_TILE
    one = pl.Buffered(1)
    return pl.pallas_call(
        _ffn_body,
        grid=(t // tm,),
        in_specs=[
            pl.BlockSpec((tm, D_MODEL), lambda i: (i, 0)),
            _of_layer(l, 1, D_MODEL),
            _of_layer(l, D_MODEL, D_FF, pipeline_mode=one),
            _of_layer(l, D_MODEL, D_FF, pipeline_mode=one),
            _of_layer(l, D_FF, D_MODEL, pipeline_mode=one),
        ],
        out_specs=pl.BlockSpec((tm, D_MODEL), lambda i: (i, 0)),
        out_shape=jax.ShapeDtypeStruct((t, D_MODEL), F32),
        compiler_params=_cparams(("parallel",)),
        name="ffn",
    )(x2d, g, wg, wu, wd)


def _head_norm_rope(acc, gain, cos, sin, bd_ref):
    tm = acc.shape[0]
    ln, mx, half = V7X_LANES, V7X_MXU_DIM, HEAD_DIM // 2
    lane = lax.broadcasted_iota(jnp.int32, (tm, ln), 1)
    first_half = (lane % HEAD_DIM) < half
    slabs = []
    for s in range(ATT_WIDTH // mx):
        a = acc[:, s * mx:(s + 1) * mx]
        ms = jnp.dot((a * a).astype(BF16), bd_ref[...], preferred_element_type=F32)
        y = a * lax.rsqrt(ms + EPS) * gain[:, s * mx:(s + 1) * mx]
        for t in range(mx // ln):
            y1 = y[:, t * ln:(t + 1) * ln]
            yr = jnp.where(first_half, pltpu.roll(y1, ln - half, 1), pltpu.roll(y1, half, 1))
            slabs.append(y1 * cos + yr * sin)
    return jnp.concatenate(slabs, axis=1)


def _qkv_pieces(x_ref, g_ref, w_ref, gq_ref, gk_ref, table_refs, bd_ref, p4_ref, p16_ref, out_refs):
    tm = x_ref.shape[0]
    h = _rms_rows(x_ref[...], g_ref[...]).astype(BF16)

    def permuted(p_ref):
        blocks = [jnp.dot(p_ref[...], h[b * PERM:(b + 1) * PERM], preferred_element_type=F32).astype(BF16)
                  for b in range(tm // PERM)]
        return jnp.concatenate(blocks, axis=0)

    lhs = (h, permuted(p4_ref), permuted(p16_ref))
    gains = (gq_ref[...] * HEAD_DIM ** -0.5, gk_ref[...])

    def lanes128(c_ref, s_ref):
        c, s = c_ref[...], s_ref[...]
        return jnp.concatenate([c, c], axis=1), jnp.concatenate([s, s], axis=1)

    tables = [lanes128(table_refs[2 * gi], table_refs[2 * gi + 1]) for gi in range(N_GROUPS)]

    def piece(gi, ti):
        def run():
            r = WINDOW_DILATIONS[gi][1]
            n = PERM // r
            col = ti * ATT_COLS + gi * ATT_WIDTH
            y = jnp.dot(lhs[gi], w_ref[:, col:col + ATT_WIDTH], preferred_element_type=F32)
            if ti < 2:
                y = _head_norm_rope(y, gains[ti], *tables[gi], bd_ref)
            y = y.astype(BF16)
            o_ref = out_refs[3 * gi + ti]
            if r == 1:
                o_ref[0] = y
            else:
                for b in range(tm // PERM):
                    for c in range(r):
                        o_ref[c, b * n:(b + 1) * n, :] = y[b * PERM + c * n:b * PERM + (c + 1) * n, :]
        return run

    return [piece(gi, ti) for gi in range(N_GROUPS) for ti in range(3)]


def _proj_qkv_body(x_ref, g_ref, w_ref, gq_ref, gk_ref, c1_ref, s1_ref, c4_ref, s4_ref, c16_ref, s16_ref,
                   bd_ref, p4_ref, p16_ref, *out_refs):
    tables = (c1_ref, s1_ref, c4_ref, s4_ref, c16_ref, s16_ref)
    for run in _qkv_pieces(x_ref, g_ref, w_ref, gq_ref, gk_ref, tables, bd_ref, p4_ref, p16_ref, out_refs):
        run()


def _proj_rest_body(x_ref, g_ref, w_ref, z_ref, bd_ref, u_ref):
    h = _rms_rows(x_ref[...], g_ref[...]).astype(BF16)
    tn = PROJ_CHUNK
    proj = lambda c0: jnp.dot(h, w_ref[:, c0:c0 + tn], preferred_element_type=F32)
    for k in range(Z_COLS // tn):
        z_ref[:, k * tn:(k + 1) * tn] = proj(QKV_COLS + k * tn)
    hgrn0 = QKV_COLS + Z_COLS
    u0 = hgrn0 + 3 * HGRN_WIDTH
    gate0 = u0 + FNET_WIDTH
    for k in range(GATE_COLS // tn):
        bd_ref[:, k * tn:(k + 1) * tn] = proj(gate0 + k * tn).astype(BF16)
    for k in range(3 * HGRN_WIDTH // tn):
        bd_ref[:, GATE_COLS + k * tn:GATE_COLS + (k + 1) * tn] = proj(hgrn0 + k * tn).astype(BF16)
    u_ref[...] = proj(u0).astype(BF16)


def _proj_qkv(x2d, g, w_in, gq, gk, consts, batch, seq, l):
    t, tm = x2d.shape[0], TOKEN_TILE
    nsb = seq // tm
    const = lambda i: (0, 0)
    tab = pl.BlockSpec((tm, HEAD_DIM), lambda i: (i % nsb, 0))
    in_specs = [
        pl.BlockSpec((tm, D_MODEL), lambda i: (i, 0)),
        _of_layer(l, 1, D_MODEL),
        _of_layer(l, D_MODEL, IN_COLS, pipeline_mode=pl.Buffered(1)),
        _of_layer(l, 1, ATT_WIDTH), _of_layer(l, 1, ATT_WIDTH),
        tab, tab, tab, tab, tab, tab,
        pl.BlockSpec((V7X_MXU_DIM, V7X_MXU_DIM), const),
        pl.BlockSpec((PERM, PERM), const), pl.BlockSpec((PERM, PERM), const),
    ]
    out_specs, out_shape = [], []
    for _, r in WINDOW_DILATIONS:
        for _ in range(3):
            out_specs.append(pl.BlockSpec((None, r, tm // r, ATT_WIDTH), lambda i: (i // nsb, 0, i % nsb, 0)))
            out_shape.append(jax.ShapeDtypeStruct((batch, r, seq // r, ATT_WIDTH), BF16))
    rope = consts["rope"]
    return pl.pallas_call(
        _proj_qkv_body,
        grid=(t // tm,),
        in_specs=in_specs,
        out_specs=out_specs,
        out_shape=out_shape,
        compiler_params=_cparams(("parallel",)),
        name="proj_qkv",
    )(x2d, g, w_in, gq, gk, rope[1][0], rope[1][1], rope[4][0], rope[4][1], rope[16][0], rope[16][1],
      consts["head_bd"], consts["perm"][4], consts["perm"][16])


def _proj_rest(x2d, g, w_in, l):
    t, tm = x2d.shape[0], TOKEN_TILE
    row = lambda w: pl.BlockSpec((tm, w), lambda i: (i, 0))
    return pl.pallas_call(
        _proj_rest_body,
        grid=(t // tm,),
        in_specs=[row(D_MODEL), _of_layer(l, 1, D_MODEL),
                  _of_layer(l, D_MODEL, IN_COLS, pipeline_mode=pl.Buffered(1))],
        out_specs=[row(Z_COLS), row(BD_COLS), row(FNET_WIDTH)],
        out_shape=[jax.ShapeDtypeStruct((t, Z_COLS), F32), jax.ShapeDtypeStruct((t, BD_COLS), BF16),
                   jax.ShapeDtypeStruct((t, FNET_WIDTH), BF16)],
        compiler_params=_cparams(("parallel",)),
        name="proj_rest",
    )(x2d, g, w_in)


def _lse_lane(h):
    return HEAD_DIM * (h % 2) + LSE_LANES_PER_HEAD * (h // 2)


def _attn_body(q_ref, kp_ref, kc_ref, kn_ref, vp_ref, vc_ref, vn_ref, pt_ref, o_ref, lse_ref,
               ks, vs, os_, ls, *, r, tl, seq_l):
    blk = pl.program_id(1)
    ln = V7X_LANES
    sq, kw = ATT_UNIT_ROWS, ATT_UNIT_ROWS + 2 * BAND
    ks[:, 0:BAND, :] = kp_ref[...]
    ks[:, BAND:BAND + tl, :] = kc_ref[...]
    ks[:, BAND + tl:2 * BAND + tl, :] = kn_ref[...]
    vs[:, 0:BAND, :] = vp_ref[...]
    vs[:, BAND:BAND + tl, :] = vc_ref[...]
    vs[:, BAND + tl:2 * BAND + tl, :] = vn_ref[...]

    nsub = tl // sq
    hq = ATT_ROWS_PER_TILE
    qi = lax.broadcasted_iota(jnp.int32, (sq, kw), 0)
    ki = lax.broadcasted_iota(jnp.int32, (sq, kw), 1)
    in_band = jnp.abs(ki - BAND - qi) <= BAND
    lane = lax.broadcasted_iota(jnp.int32, (hq, ln), 1)
    lane_lo = lane < HEAD_DIM
    lane_pair = (lane % HEAD_DIM) // LSE_LANES_PER_HEAD
    lane_k = lax.broadcasted_iota(jnp.int32, (kw, ln), 1) < HEAD_DIM
    nt = (((1,), (1,)), ((), ()))

    def unit(u, carry):
        c = u // nsub
        q0 = pl.multiple_of((u % nsub) * sq, sq)
        kpos = blk * tl + q0 - BAND + ki
        valid = in_band & (kpos >= 0) & (kpos < seq_l)
        q2 = q_ref[c, pl.ds(q0, sq), :]
        k2 = ks[c, pl.ds(q0, kw), :]
        v2 = vs[c, pl.ds(q0, kw), :]
        for half in range(sq // hq):
            r0, r1 = half * hq, (half + 1) * hq
            vmask = jnp.concatenate([valid[r0:r1], valid[r0:r1]], axis=0)
            lse_tile = jnp.zeros((hq, ln), F32)
            for hp in range(ATT_HEADS // 2):
                sl = slice(hp * ln, (hp + 1) * ln)
                q128, k128, v128 = q2[r0:r1, sl], k2[:, sl], v2[:, sl]
                zero = jnp.zeros_like(q128)
                qstack = jnp.concatenate([jnp.where(lane_lo, q128, zero), jnp.where(lane_lo, zero, q128)], axis=0)
                s = lax.dot_general(qstack, k128, nt, preferred_element_type=F32)
                s = jnp.where(vmask, s, NEG_INF)
                m = jnp.max(s, axis=-1, keepdims=True)
                p = jnp.exp(s - m).astype(BF16)
                one = jnp.ones_like(v128)
                pv0 = jnp.dot(p[0:hq], jnp.where(lane_k, v128, one), preferred_element_type=F32)
                pv1 = jnp.dot(p[hq:2 * hq], jnp.where(lane_k, one, v128), preferred_element_type=F32)
                num = jnp.where(lane_lo, pv0, pv1)
                den = pltpu.roll(jnp.where(lane_lo, pv1, pv0), HEAD_DIM, 1)
                os_[c, pl.ds(q0 + r0, hq), sl] = (num / den).astype(BF16)
                lse_pair = jnp.where(lane_lo, m[0:hq], m[hq:2 * hq]) + jnp.log(den)
                lse_tile = jnp.where(lane_pair == hp, lse_pair, lse_tile)
            ls[c, pl.ds(q0 + r0, hq), :] = lse_tile
        return carry

    lax.fori_loop(0, r * nsub, unit, 0, unroll=2)

    if r == 1:
        o_ref[...] = os_[0]
        lse_ref[...] = ls[0]
    else:
        n = PERM // r
        for b in range(r * tl // PERM):
            cat = jnp.concatenate([os_[c, b * n:(b + 1) * n, :] for c in range(r)], axis=0)
            o_ref[b * PERM:(b + 1) * PERM, :] = jnp.dot(pt_ref[...], cat, preferred_element_type=F32).astype(BF16)
        for c in range(r):
            lse_ref[pl.ds(c, tl, stride=r), :] = ls[c]


def _attention_group(q, k, v, perm_t, r):
    b, _, l, _ = q.shape
    tl = min(ATT_CLASS_ROWS[r], l)
    ts = r * tl
    nh = tl // BAND
    last = l // BAND - 1
    cur = pl.BlockSpec((None, r, tl, ATT_WIDTH), lambda bi, i: (bi, 0, i, 0))
    prv = pl.BlockSpec((None, r, BAND, ATT_WIDTH), lambda bi, i: (bi, 0, jnp.maximum(i * nh - 1, 0), 0))
    nxt = pl.BlockSpec((None, r, BAND, ATT_WIDTH), lambda bi, i: (bi, 0, jnp.minimum((i + 1) * nh, last), 0))
    body = functools.partial(_attn_body, r=r, tl=tl, seq_l=l)
    return pl.pallas_call(
        body,
        grid=(b, l // tl),
        in_specs=[cur, prv, cur, nxt, prv, cur, nxt, pl.BlockSpec((PERM, PERM), lambda bi, i: (0, 0))],
        out_specs=[pl.BlockSpec((None, ts, ATT_WIDTH), lambda bi, i: (bi, i, 0)),
                   pl.BlockSpec((None, ts, V7X_LANES), lambda bi, i: (bi, i, 0))],
        out_shape=[jax.ShapeDtypeStruct((b, r * l, ATT_WIDTH), BF16),
                   jax.ShapeDtypeStruct((b, r * l, V7X_LANES), F32)],
        scratch_shapes=[
            pltpu.VMEM((r, tl + 2 * BAND, ATT_WIDTH), BF16),
            pltpu.VMEM((r, tl + 2 * BAND, ATT_WIDTH), BF16),
            pltpu.VMEM((r, tl, ATT_WIDTH), BF16),
            pltpu.VMEM((r, tl, V7X_LANES), F32),
        ],
        compiler_params=_cparams(("parallel", "parallel")),
        name=f"attn_r{r}",
    )(q, k, k, k, v, v, v, perm_t)


def _hgrn_chunk_fn(z_ref, q_ref, i_ref, lb_ref, lb_row, wsel_ref, o_ref, st_ref, rev):
    ch, sb = HGRN_CHUNK, HGRN_SUB
    row = lax.broadcasted_iota(jnp.int32, (ch, ch), 0)
    col = lax.broadcasted_iota(jnp.int32, (ch, ch), 1)
    dist = (col - row) if rev else (row - col)
    same = lambda n: (row // n) == (col // n)
    diag_mask = same(sb) & (dist >= 0)
    level_sizes = tuple(n for n in (1, 2, 4, 8, 16, 32) if n >= sb)
    level_masks = [same(2 * n) & jnp.logical_not(same(n)) & (dist > 0) for n in level_sizes]
    sl8 = V7X_SUBLANES
    row8 = lax.broadcasted_iota(jnp.int32, (sl8, HGRN_DK), 0)
    sub = lax.broadcasted_iota(jnp.int32, (ch, HGRN_DK), 0) % sb
    tau = (sb - 1 - sub) if rev else sub
    nt = (((1,), (1,)), ((), ()))
    tn = (((0,), (0,)), ((), ()))

    def roll8(x, k):
        return pltpu.roll(x.reshape(ch // sl8, sl8, HGRN_DK), k % sl8, 1).reshape(ch, HGRN_DK)

    prev = lambda x, k: roll8(x, -k if rev else k)
    nxt = lambda x, k: roll8(x, k if rev else -k)
    last_row = lambda start, n: start if rev else start + n - 1

    def widen(pf, sf, n):
        later_first = rev
        pf_rows, sf_rows = [], []
        for v in range(ch // sl8):
            x, y = pf[sl8 * v:sl8 * (v + 1)], sf[sl8 * v:sl8 * (v + 1)]
            if n >= sl8:
                blk = (sl8 * v) // n
                first = blk % 2 == 0
                sib = (blk + 1) * n if first else (blk - 1) * n
                r = last_row(sib, n)
                total = jnp.broadcast_to(pf[r:r + 1, :], (sl8, HGRN_DK))
                if first == later_first:
                    x = x * total
                else:
                    y = y * total
            else:
                pmul = jnp.ones((sl8, HGRN_DK), F32)
                smul = pmul
                for b0 in range(0, sl8, 2 * n):
                    lo_tot = jnp.broadcast_to(x[last_row(b0, n):last_row(b0, n) + 1, :], (sl8, HGRN_DK))
                    hi_tot = jnp.broadcast_to(x[last_row(b0 + n, n):last_row(b0 + n, n) + 1, :], (sl8, HGRN_DK))
                    in_lo = (row8 >= b0) & (row8 < b0 + n)
                    in_hi = (row8 >= b0 + n) & (row8 < b0 + 2 * n)
                    if later_first:
                        pmul, smul = jnp.where(in_lo, hi_tot, pmul), jnp.where(in_hi, lo_tot, smul)
                    else:
                        pmul, smul = jnp.where(in_hi, lo_tot, pmul), jnp.where(in_lo, hi_tot, smul)
                x, y = x * pmul, y * smul
            pf_rows.append(x)
            sf_rows.append(y)
        return jnp.concatenate(pf_rows, axis=0), jnp.concatenate(sf_rows, axis=0)

    def chunk(c):
        r0 = c * ch if isinstance(c, int) else pl.multiple_of(c * ch, ch)
        heads = range(HGRN_HEADS)
        sls = [slice(h * HGRN_DK, (h + 1) * HGRN_DK) for h in heads]
        qs, ivs, kfs, pfs, sfs, bands = [], [], [], [], [], []
        for sl in sls:
            z = z_ref[pl.ds(r0, ch), sl]
            q = q_ref[pl.ds(r0, ch), sl].astype(F32)
            lb = lb_ref[lb_row:lb_row + 1, sl]
            one_m_lb = 1.0 - lb
            e = jnp.exp(-jnp.abs(z))
            inv = 1.0 / (1.0 + e)
            pos = z >= 0
            f = lb + one_m_lb * (jnp.where(pos, 1.0, e) * inv)
            kf = one_m_lb * (jnp.where(pos, e, 1.0) * inv)
            pf = f
            sf = jnp.where(tau == sb - 1, 1.0, nxt(f, 1))
            for k in (k for k in (1, 2, 4) if k < sb):
                pf = pf * jnp.where(tau >= k, prev(pf, k), 1.0)
                sf = sf * jnp.where(tau <= sb - 1 - k, nxt(sf, k), 1.0)
            g = kf
            terms = [q * g]
            for d in range(1, sb):
                g = prev(g, 1) * f
                terms.append(q * g)
            qs.append(q); ivs.append(i_ref[pl.ds(r0, ch), sl]); kfs.append(kf); pfs.append(pf); sfs.append(sf)
            bands.append(jnp.concatenate(terms, axis=1).astype(BF16))

        bands = [jnp.dot(x, wsel_ref[...], preferred_element_type=F32) for x in bands]

        levels = []
        for n in level_sizes:
            levels.append([lax.dot_general((qs[h] * pfs[h]).astype(BF16), (kfs[h] * sfs[h]).astype(BF16), nt,
                                           preferred_element_type=F32) for h in heads])
            for h in heads:
                pfs[h], sfs[h] = widen(pfs[h], sfs[h], n)
        sts = [st_ref[h] for h in heads]
        inter = [lax.dot_general((qs[h] * pfs[h]).astype(BF16), sts[h].astype(BF16), nt, preferred_element_type=F32)
                 for h in heads]
        upd = [lax.dot_general(ivs[h], (kfs[h] * sfs[h]).astype(BF16), tn, preferred_element_type=F32)
               for h in heads]

        r_end = last_row(0, ch)
        for h in heads:
            skew = pltpu.roll(bands[h], 0, 1, stride=1, stride_axis=0)
            a = jnp.where(diag_mask, skew[:, 0:ch], 0.0)
            for lvl, mask in zip(levels, level_masks):
                a = jnp.where(mask, lvl[h], a)
            o = jnp.dot(a.astype(BF16), ivs[h], preferred_element_type=F32)
            o_ref[pl.ds(r0, ch), sls[h]] = (o + inter[h]).astype(o_ref.dtype)
            st_ref[h] = sts[h] * pfs[h][r_end:r_end + 1, :] + upd[h]

    return chunk


def _hgrn_body(z_ref, q_ref, i_ref, lb_ref, wsel_ref, o_ref, st_ref, *, rev, nch):
    @pl.when(pl.program_id(1) == 0)
    def _():
        st_ref[...] = jnp.zeros_like(st_ref)

    chunk = _hgrn_chunk_fn(z_ref, q_ref, i_ref, lb_ref, 0, wsel_ref, o_ref, st_ref, rev)

    def step(ci, carry):
        chunk((nch - 1 - ci) if rev else ci)
        return carry

    lax.fori_loop(0, nch, step, 0, unroll=4)


def _hgrn_bidir_body(zf_ref, zb_ref, qf_ref, if_ref, qb_ref, ib_ref, lb_ref, wf_ref, wb_ref, of_ref, ob_ref,
                     stf_ref, stb_ref, *, nch):
    @pl.when(pl.program_id(1) == 0)
    def _():
        stf_ref[...] = jnp.zeros_like(stf_ref)
        stb_ref[...] = jnp.zeros_like(stb_ref)

    fwd = _hgrn_chunk_fn(zf_ref, qf_ref, if_ref, lb_ref, 0, wf_ref, of_ref, stf_ref, False)
    bwd = _hgrn_chunk_fn(zb_ref, qb_ref, ib_ref, lb_ref, 1, wb_ref, ob_ref, stb_ref, True)

    def step(ci, carry):
        fwd(ci)
        bwd(nch - 1 - ci)
        return carry

    lax.fori_loop(0, nch, step, 0, unroll=2)


def _hgrn_scan_bidir(z, bd, lb):
    b, s, _ = z.shape
    ts = HGRN_TILE
    nb = s // ts
    blk = lambda col, rev=False: pl.BlockSpec((None, ts, HGRN_WIDTH),
                                              lambda bi, j: (bi, (nb - 1 - j) if rev else j, col))
    const = lambda shape: pl.BlockSpec(shape, lambda bi, j: (0, 0))
    sel = (HGRN_SUB * HGRN_DK, V7X_LANES)
    out = jax.ShapeDtypeStruct((b, s, HGRN_WIDTH), BF16)
    state = pltpu.VMEM((HGRN_HEADS, HGRN_DK, HGRN_DK), F32)
    return pl.pallas_call(
        functools.partial(_hgrn_bidir_body, nch=ts // HGRN_CHUNK),
        grid=(b, nb),
        in_specs=[blk(0), blk(1, True), blk(BD_Q), blk(BD_I), blk(BD_Q, True), blk(BD_I, True),
                  const((2, HGRN_WIDTH)), const(sel), const(sel)],
        out_specs=[blk(0), blk(0, True)],
        out_shape=[out, out],
        scratch_shapes=[state, state],
        compiler_params=_cparams(("parallel", "arbitrary")),
        name="hgrn_bidir",
    )(z, z, bd, bd, bd, bd, lb, _band_selector(False), _band_selector(True))


def _band_selector(rev):
    w = np.zeros((HGRN_SUB * HGRN_DK, V7X_LANES), np.float32)
    for d in range(HGRN_SUB):
        w[d * HGRN_DK:(d + 1) * HGRN_DK, d if rev else (V7X_LANES - d) % V7X_LANES] = 1.0
    return jnp.asarray(w, BF16)


def _hgrn_scan(z, bd, lb, rev):
    b, s, _ = z.shape
    ts = HGRN_TILE
    nb = s // ts
    order = (lambda j: nb - 1 - j) if rev else (lambda j: j)
    spec = lambda col: pl.BlockSpec((None, ts, HGRN_WIDTH), lambda bi, j: (bi, order(j), col))
    const = lambda shape: pl.BlockSpec(shape, lambda bi, j: (0, 0))
    body = functools.partial(_hgrn_body, rev=rev, nch=ts // HGRN_CHUNK)
    return pl.pallas_call(
        body,
        grid=(b, nb),
        in_specs=[spec(1 if rev else 0), spec(BD_Q), spec(BD_I), const((1, HGRN_WIDTH)),
                  const((HGRN_SUB * HGRN_DK, V7X_LANES))],
        out_specs=spec(0),
        out_shape=jax.ShapeDtypeStruct((b, s, HGRN_WIDTH), BF16),
        scratch_shapes=[pltpu.VMEM((HGRN_HEADS, HGRN_DK, HGRN_DK), F32)],
        compiler_params=_cparams(("parallel", "arbitrary")),
        name="hgrn_bwd" if rev else "hgrn_fwd",
    )(z, bd, bd, lb, _band_selector(rev))


def _fnet_a_body(u_ref, cs_ref, m_ref, o_ref, *, n1, nt):
    ngrp = FNET_WIDTH // FNET_GROUP_DIM
    for s in range(nt):
        zr, zi = [], []
        for g in range(ngrp):
            c0 = s * FNET_WIDTH + g * FNET_GROUP_DIM
            zz = jnp.dot(u_ref[:, c0:c0 + FNET_GROUP_DIM], cs_ref[...], preferred_element_type=F32)
            zr.append(zz[:, :FNET_GROUP_DIM])
            zi.append(zz[:, FNET_GROUP_DIM:])
        zcat = jnp.concatenate([jnp.concatenate(zr, axis=1), jnp.concatenate(zi, axis=1)], axis=0).astype(BF16)
        res = jnp.dot(m_ref[s], zcat, preferred_element_type=F32)
        o_ref[:, s * FNET_WIDTH:(s + 1) * FNET_WIDTH] = res.astype(o_ref.dtype)


def _fnet_c_body(b_ref, cs_ref, o_ref, *, kt):
    for k in range(kt):
        bcat = jnp.concatenate([b_ref[0, k], b_ref[1, k]], axis=0)
        x = jnp.dot(cs_ref[...], bcat, preferred_element_type=F32)
        o_ref[:, k * FNET_WIDTH:(k + 1) * FNET_WIDTH] = x.astype(o_ref.dtype)


def _fnet_tables(n1, n2):
    n = n1 * n2
    k = np.arange(FNET_GROUP_DIM)
    ang = 2.0 * np.pi * ((k[:, None] * k[None, :]) % FNET_GROUP_DIM) / FNET_GROUP_DIM
    cs_ch = np.concatenate([np.cos(ang), -np.sin(ang)], axis=1) / np.sqrt(FNET_GROUP_DIM)
    k1 = np.arange(n1)[None, :, None]
    m1 = np.arange(n1)[None, None, :]
    s2 = np.arange(n2)[:, None, None]
    idx = (k1 * m1 * n2 + s2 * k1) % n
    th = 2.0 * np.pi * idx / n
    gc, gs = np.cos(th) / np.sqrt(n1), np.sin(th) / np.sqrt(n1)
    m = np.concatenate([np.concatenate([gc, gs], axis=2), np.concatenate([-gs, gc], axis=2)], axis=1)
    k2 = np.arange(n2)
    ang2 = 2.0 * np.pi * ((k2[:, None] * k2[None, :]) % n2) / n2
    cs_seq = np.concatenate([np.cos(ang2), np.sin(ang2)], axis=1) / np.sqrt(n2)
    return (jnp.asarray(cs_ch, BF16), jnp.asarray(m, BF16), jnp.asarray(cs_seq, BF16))


def _fourier_branch(u, n1, n2):
    b, s, _ = u.shape
    cs_ch, m, cs_seq = _fnet_tables(n1, n2)
    nt = FNET_SLABS_PER_STEP
    kt = FNET_ROWS_PER_STEP
    u2 = u.reshape(b, n1, n2 * FNET_WIDTH)
    a = pl.pallas_call(
        functools.partial(_fnet_a_body, n1=n1, nt=nt),
        grid=(b, n2 // nt),
        in_specs=[
            pl.BlockSpec((None, n1, nt * FNET_WIDTH), lambda bi, j: (bi, 0, j)),
            pl.BlockSpec((FNET_GROUP_DIM, 2 * FNET_GROUP_DIM), lambda bi, j: (0, 0)),
            pl.BlockSpec((nt, 2 * n1, 2 * n1), lambda bi, j: (j, 0, 0)),
        ],
        out_specs=pl.BlockSpec((None, 2 * n1, nt * FNET_WIDTH), lambda bi, j: (bi, 0, j)),
        out_shape=jax.ShapeDtypeStruct((b, 2 * n1, n2 * FNET_WIDTH), BF16),
        compiler_params=_cparams(("parallel", "parallel")),
        name="fnet_a",
    )(u2, cs_ch, m)
    a5 = a.reshape(b, 2, n1, n2, FNET_WIDTH)
    y = pl.pallas_call(
        functools.partial(_fnet_c_body, kt=kt),
        grid=(b, n1 // kt),
        in_specs=[
            pl.BlockSpec((None, 2, kt, n2, FNET_WIDTH), lambda bi, j: (bi, 0, j, 0, 0)),
            pl.BlockSpec((n2, 2 * n2), lambda bi, j: (0, 0)),
        ],
        out_specs=pl.BlockSpec((None, n2, kt * FNET_WIDTH), lambda bi, j: (bi, 0, j)),
        out_shape=jax.ShapeDtypeStruct((b, n2, n1 * FNET_WIDTH), BF16),
        compiler_params=_cparams(("parallel", "parallel")),
        name="fnet_c",
    )(a5, cs_seq)
    return y.reshape(b, s, FNET_WIDTH)


def _merge_body(x_ref, o0_ref, o1_ref, o2_ref, l0_ref, l1_ref, l2_ref, hf_ref, hb_ref, gh_ref, y_ref,
                ga_ref, gb_ref, gc_ref, wa_ref, wh_ref, wf_ref, wo_ref, hg_ref, ex_ref, out_ref):
    lses = [l0_ref[...], l1_ref[...], l2_ref[...]]
    m = jnp.maximum(jnp.maximum(lses[0], lses[1]), lses[2])
    ws = [jnp.exp(l - m) for l in lses]
    inv = 1.0 / (ws[0] + ws[1] + ws[2])
    att = None
    for w, o_ref in zip(ws, (o0_ref, o1_ref, o2_ref)):
        alpha = jnp.dot((w * inv).astype(BF16), ex_ref[...], preferred_element_type=F32)
        term = alpha * o_ref[...].astype(F32)
        att = term if att is None else att + term
    a = jnp.dot(att.astype(BF16), wa_ref[...], preferred_element_type=F32)
    oh = hf_ref[...].astype(F32) + hb_ref[...].astype(F32)
    gh = gh_ref[...].astype(F32)
    heads = []
    for h in range(HGRN_HEADS):
        sl = slice(h * HGRN_DK, (h + 1) * HGRN_DK)
        gate = gh[:, sl]
        heads.append(_rms_rows(oh[:, sl], hg_ref[...]) * (gate * jax.nn.sigmoid(gate)))
    bh = jnp.dot(jnp.concatenate(heads, axis=1).astype(BF16), wh_ref[...], preferred_element_type=F32)
    c = jnp.dot(y_ref[...], wf_ref[...], preferred_element_type=F32)
    sig = lambda ref: jax.nn.sigmoid(ref[...].astype(F32))
    merged = sig(ga_ref) * a + sig(gb_ref) * bh + sig(gc_ref) * c
    out_ref[...] = x_ref[...] + jnp.dot(merged.astype(BF16), wo_ref[...], preferred_element_type=F32)


def _merge(x2d, att_outs, att_lses, hf, hb, bd, y, wa, wh, wf, wo, hgain, expand, l):
    t, tm = x2d.shape[0], TOKEN_TILE
    row = lambda w: pl.BlockSpec((tm, w), lambda i: (i, 0))
    gate = lambda k: pl.BlockSpec((tm, D_MODEL), lambda i: (i, k))
    in_specs = ([row(D_MODEL)] + [row(ATT_WIDTH)] * 3 + [row(V7X_LANES)] * 3 + [row(HGRN_WIDTH)] * 2
                + [pl.BlockSpec((tm, HGRN_WIDTH), lambda i: (i, BD_G)),
                   row(FNET_WIDTH), gate(0), gate(1), gate(2),
                   _of_layer(l, ATT_WIDTH, D_MODEL), _of_layer(l, HGRN_WIDTH, D_MODEL),
                   _of_layer(l, FNET_WIDTH, D_MODEL), _of_layer(l, D_MODEL, D_MODEL), _of_layer(l, 1, HGRN_DK),
                   pl.BlockSpec((V7X_LANES, ATT_WIDTH), lambda i: (0, 0))])
    return pl.pallas_call(
        _merge_body,
        grid=(t // tm,),
        in_specs=in_specs,
        out_specs=row(D_MODEL),
        out_shape=jax.ShapeDtypeStruct((t, D_MODEL), F32),
        compiler_params=_cparams(("parallel",)),
        name="merge",
    )(x2d, *att_outs, *att_lses, hf, hb, bd, y, bd, bd, bd, wa, wh, wf, wo, hgain, expand)


def _trunk_consts(seq):
    half = HEAD_DIM // 2
    inv_freq = np.float32(ROPE_THETA) ** (-np.arange(half, dtype=np.float32) / np.float32(half))
    ang = np.arange(seq, dtype=np.float32)[:, None] * inv_freq[None, :]
    cos_nat, sin_nat = np.cos(ang.astype(np.float64)), np.sin(ang.astype(np.float64))
    rope = {}
    for _, r in WINDOW_DILATIONS:
        order = lambda tab: tab.reshape(seq // PERM, PERM // r, r, half).transpose(0, 2, 1, 3).reshape(seq, half)
        cos, sin = order(cos_nat), order(sin_nat)
        rope[r] = (jnp.asarray(np.concatenate([cos, cos], axis=1), F32),
                   jnp.asarray(np.concatenate([-sin, sin], axis=1), F32))
    hd = np.arange(V7X_MXU_DIM) // HEAD_DIM
    head_bd = jnp.asarray((hd[:, None] == hd[None, :]) / HEAD_DIM, BF16)
    perm, perm_t = {}, {}
    for _, r in WINDOW_DILATIONS:
        idx = np.arange(PERM)
        n = PERM // r
        p = np.zeros((PERM, PERM), np.float32)
        p[idx, (idx % n) * r + idx // n] = 1.0
        perm[r], perm_t[r] = jnp.asarray(p, BF16), jnp.asarray(p.T, BF16)
    expand = np.zeros((V7X_LANES, ATT_WIDTH), np.float32)
    for h in range(ATT_HEADS):
        expand[_lse_lane(h), h * HEAD_DIM:(h + 1) * HEAD_DIM] = 1.0
    return dict(rope=rope, head_bd=head_bd, perm=perm, perm_t=perm_t, expand=jnp.asarray(expand, BF16),
                n1=FNET_SLOW_LEN)


def _stacked_weights(hgrn_lb_logits, ffn1_norm, ffn1_w_gate, ffn1_w_up, ffn1_w_down, mix_norm, w_in, q_norm,
                     k_norm, w_att_out, hgrn_out_norm, w_hgrn_out, w_fnet_out, w_out, ffn2_norm, ffn2_w_gate,
                     ffn2_w_up, ffn2_w_down):
    bf = lambda w: w.astype(BF16)
    vec = lambda v: v.astype(F32)[:, None, :]
    heads_per_tile = ATT_WIDTH // HEAD_DIM
    p = jax.nn.softmax(hgrn_lb_logits.astype(F32), axis=0)
    cum = jnp.cumsum(p, axis=0)
    return dict(
        ffn1=(vec(ffn1_norm), bf(ffn1_w_gate), bf(ffn1_w_up), bf(ffn1_w_down)),
        ffn2=(vec(ffn2_norm), bf(ffn2_w_gate), bf(ffn2_w_up), bf(ffn2_w_down)),
        mix_norm=vec(mix_norm), w_in=bf(w_in),
        gq=vec(jnp.tile(q_norm, (1, heads_per_tile))), gk=vec(jnp.tile(k_norm, (1, heads_per_tile))),
        lb=cum - cum[0:1],
        wa=bf(w_att_out), wh=bf(w_hgrn_out), wf=bf(w_fnet_out), wo=bf(w_out), hgain=vec(hgrn_out_norm),
    )


def _encoder_layer(x, wts, consts, l):
    b, s, _ = x.shape
    t = b * s
    x2d = x.reshape(t, D_MODEL)
    x2d = _ffn(x2d, *wts["ffn1"], l)
    z, bd, u = _proj_rest(x2d, wts["mix_norm"], wts["w_in"], l)
    z3, bd3 = z.reshape(b, s, Z_COLS), bd.reshape(b, s, BD_COLS)
    qkv = _proj_qkv(x2d, wts["mix_norm"], wts["w_in"], wts["gq"], wts["gk"], consts, b, s, l)
    hf, hb = _hgrn_scan_bidir(z3, bd3, wts["lb"][l])
    hf, hb = hf.reshape(t, HGRN_WIDTH), hb.reshape(t, HGRN_WIDTH)
    outs, lses = [], []
    for gi, (w, r) in enumerate(WINDOW_DILATIONS):
        assert w // (2 * r) == BAND
        o, lse = _attention_group(qkv[3 * gi], qkv[3 * gi + 1], qkv[3 * gi + 2], consts["perm_t"][r], r)
        outs.append(o.reshape(t, ATT_WIDTH))
        lses.append(lse.reshape(t, V7X_LANES))
    y = _fourier_branch(u.reshape(b, s, FNET_WIDTH), consts["n1"], s // consts["n1"]).reshape(t, FNET_WIDTH)
    x2d = _merge(x2d, outs, lses, hf, hb, bd, y, wts["wa"], wts["wh"], wts["wf"], wts["wo"], wts["hgain"],
                 consts["expand"], l)
    x2d = _ffn(x2d, *wts["ffn2"], l)
    return x2d.reshape(b, s, D_MODEL)


def kernel(x_prompt, x_sample, hgrn_lb_logits, ffn1_norm, ffn1_w_gate, ffn1_w_up, ffn1_w_down, mix_norm, w_in,
           q_norm, k_norm, w_att_out, hgrn_out_norm, w_hgrn_out, w_fnet_out, w_out, ffn2_norm, ffn2_w_gate,
           ffn2_w_up, ffn2_w_down):
    wts = _stacked_weights(hgrn_lb_logits, ffn1_norm, ffn1_w_gate, ffn1_w_up, ffn1_w_down, mix_norm, w_in, q_norm,
                           k_norm, w_att_out, hgrn_out_norm, w_hgrn_out, w_fnet_out, w_out, ffn2_norm,
                           ffn2_w_gate, ffn2_w_up, ffn2_w_down)
    consts = _trunk_consts(max(x_prompt.shape[1], x_sample.shape[1]))

    def trunk(x):
        for l in range(w_in.shape[0]):
            x = _encoder_layer(x, wts, consts, l)
        return x

    return (trunk(x_prompt), trunk(x_sample))
```
